```python
import jax, jax.numpy as jnp
from jax import lax
import numpy as np

D_MODEL = 2048
BATCH = 2
SEQ = 4096
DEPTH = 2
DEC_BATCH = 8
DEC_SEQ = 32
PAST_LEN = 4096

CHUNK = 64
Q_BLOCK = 128
D_CONV = D_MODEL // 2
CONV_K = 31
D_POOL = D_MODEL // 2
POOL_WINDOWS = (2, 4, 8, 16)
POOL_MAX = 16
N_POOL_GROUPS = 4
POOL_GROUP = D_POOL // N_POOL_GROUPS
POOL_OUT = D_MODEL // N_POOL_GROUPS
N_HEADS = D_MODEL // 128
NOPE_DIM = 128
ROPE_DIM = 64
V_DIM = 128
Q_RANK = D_MODEL // 4
KV_RANK = D_MODEL // 4
ROPE_THETA = 10000.0
ATTN_SCALE = (NOPE_DIM + ROPE_DIM) ** -0.5
D_FF = 256 * ((8 * D_MODEL // 3 + 255) // 256)
FFN_K = 3
N_BRANCH = 3
EPS = 1e-6
NEG = -1e30
OFF_A = 0
OFF_B = OFF_A + 2 * D_CONV
OFF_Q = OFF_B + D_POOL
OFF_KV = OFF_Q + Q_RANK
OFF_R = OFF_KV + KV_RANK
OFF_G = OFF_R + ROPE_DIM
N_IN = OFF_G + N_BRANCH * D_MODEL

kernel_name = 'hybrid_streaming_encoder_step'


def rms_norm(x, g):
    x32 = x.astype(jnp.float32)
    y = x32 * lax.rsqrt(jnp.mean(x32 * x32, axis=-1, keepdims=True) + EPS)
    return (y * g.astype(jnp.float32)).astype(x.dtype)


def layer_norm(x, g, b):
    x32 = x.astype(jnp.float32)
    mu = jnp.mean(x32, axis=-1, keepdims=True)
    var = jnp.mean(jnp.square(x32 - mu), axis=-1, keepdims=True)
    y = (x32 - mu) * lax.rsqrt(var + EPS) * g.astype(jnp.float32) + b.astype(jnp.float32)
    return y.astype(x.dtype)


def rope(x, pos):
    half = ROPE_DIM // 2
    inv = ROPE_THETA ** (-jnp.arange(half, dtype=jnp.float32) / half)
    ang = pos[:, None] * inv[None, :]
    ang = ang.reshape((ang.shape[0],) + (1,) * (x.ndim - 3) + (half,))
    cos, sin = jnp.cos(ang), jnp.sin(ang)
    x32 = x.astype(jnp.float32)
    x1, x2 = x32[..., :half], x32[..., half:]
    return jnp.concatenate([x1 * cos - x2 * sin, x2 * cos + x1 * sin], axis=-1).astype(x.dtype)


def causal_dwconv(u_ext, w, b):
    c = u_ext.shape[-1]
    out = lax.conv_general_dilated(u_ext, w[:, None, :].astype(u_ext.dtype), (1,), 'VALID',
                                   dimension_numbers=('NWC', 'WIO', 'NWC'), feature_group_count=c)
    return out + b


def multiscale_pool(u_ext, pos, w_pool):
    t = pos.shape[0]
    h0 = POOL_MAX - 1
    u32 = u_ext.astype(jnp.float32)
    cs = jnp.concatenate([jnp.zeros_like(u32[:, :1]), jnp.cumsum(u32, axis=1)], axis=1)
    u_new = u32[:, h0:]
    groups = []
    for g, w in enumerate(POOL_WINDOWS):
        sl = slice(g * POOL_GROUP, (g + 1) * POOL_GROUP)
        win_sum = cs[:, h0 + 1:, sl] - cs[:, h0 + 1 - w:h0 + 1 - w + t, sl]
        cnt = jnp.minimum(w, pos + 1).astype(jnp.float32)[None, :, None]
        groups.append(win_sum / cnt - u_new[..., sl])
    m = jnp.stack(groups, axis=2).astype(u_ext.dtype)
    out = jnp.einsum('btgc,gcd->btgd', m, w_pool)
    return out.reshape(out.shape[0], t, D_MODEL)


def mla_attend(q_abs, q_rope, ckv_all, krope_all, q_pos, w_uv):
    s_len = ckv_all.shape[1]
    k_chunk = jnp.arange(s_len, dtype=jnp.int32) // CHUNK

    def block(args):
        qa, qr, qp = args
        s = (jnp.einsum('bqhr,bsr->bhqs', qa, ckv_all)
             + jnp.einsum('bqhe,bse->bhqs', qr, krope_all)).astype(jnp.float32) * ATTN_SCALE
        allowed = k_chunk[None, :] <= (qp // CHUNK)[:, None]
        s = jnp.where(allowed[None, None], s, NEG)
        pr = jax.nn.softmax(s, axis=-1).astype(ckv_all.dtype)
        o_lat = jnp.einsum('bhqs,bsr->bqhr', pr, ckv_all)
        return jnp.einsum('bqhr,rhv->bqhv', o_lat, w_uv)

    b, t = q_abs.shape[0], q_abs.shape[1]
    if t <= Q_BLOCK:
        return block((q_abs, q_rope, q_pos))
    nb = t // Q_BLOCK

    def split(a):
        return jnp.moveaxis(a.reshape((b, nb, Q_BLOCK) + a.shape[2:]), 1, 0)

    out = lax.map(block, (split(q_abs), split(q_rope), q_pos.reshape(nb, Q_BLOCK)))
    return jnp.moveaxis(out, 0, 1).reshape(b, t, N_HEADS, V_DIM)


def trunk_layer(x, c, past_ckv, past_krope, hist_conv, hist_pool, hist_ffn, pos0, p):
    b, t, _ = x.shape
    pos_i = pos0 + jnp.arange(t, dtype=jnp.int32)
    pos_f = pos_i.astype(jnp.float32)
    mod = (jax.nn.silu(c) @ p['w_mod'] + p['b_mod']).reshape(b, 6, 1, D_MODEL)
    shift1, scale1, gate1 = mod[:, 0], mod[:, 1], mod[:, 2]
    shift2, scale2, gate2 = mod[:, 3], mod[:, 4], mod[:, 5]

    h = rms_norm(x, p['g_pre_mix']) * (1 + scale1) + shift1
    z = h @ p['w_in']
    za, zb = z[..., OFF_A:OFF_B], z[..., OFF_B:OFF_Q]
    zq, zkv, zr = z[..., OFF_Q:OFF_KV], z[..., OFF_KV:OFF_R], z[..., OFF_R:OFF_G]
    gates = jax.nn.sigmoid(z[..., OFF_G:].reshape(b, t, N_BRANCH, D_MODEL))

    u_a = za[..., :D_CONV] * jax.nn.sigmoid(za[..., D_CONV:])
    ua_ext = jnp.concatenate([hist_conv, u_a], axis=1)
    a = causal_dwconv(ua_ext, p['w_dwa'], p['b_dwa'])
    out_a = jax.nn.silu(layer_norm(a, p['ln_a_g'], p['ln_a_b'])) @ p['w_pa']

    ub_ext = jnp.concatenate([hist_pool, zb], axis=1)
    out_b = multiscale_pool(ub_ext, pos_i, p['w_pool']) * p['pool_scale']

    q = (rms_norm(zq, p['g_q_lat']) @ p['w_uq']).reshape(b, t, N_HEADS, NOPE_DIM + ROPE_DIM)
    q_abs = jnp.einsum('bthn,rhn->bthr', q[..., :NOPE_DIM], p['w_uk'])
    q_rope = rope(q[..., NOPE_DIM:], pos_f)
    ckv = rms_norm(zkv, p['g_kv_lat'])
    krope = rope(zr, pos_f)
    ckv_all = jnp.concatenate([past_ckv, ckv], axis=1)
    krope_all = jnp.concatenate([past_krope, krope], axis=1)
    o = mla_attend(q_abs, q_rope, ckv_all, krope_all, pos_i, p['w_uv'])
    out_c = o.reshape(b, t, N_HEADS * V_DIM) @ p['w_oc']

    merged = gates[:, :, 0] * out_a + gates[:, :, 1] * out_b + gates[:, :, 2] * out_c
    y = merged @ p['w_out']
    x = x + gate1 * rms_norm(y, p['g_post_mix'])

    h2 = rms_norm(x, p['g_pre_ffn']) * (1 + scale2) + shift2
    up = h2 @ p['w_up']
    af_ext = jnp.concatenate([hist_ffn, up[..., :D_FF]], axis=1)
    act = jax.nn.silu(causal_dwconv(af_ext, p['w_dwf'], p['b_dwf'])) * up[..., D_FF:]
    x = x + gate2 * rms_norm(act @ p['w_down'], p['g_post_ffn'])

    new_state = (ckv, krope, ua_ext[:, -(CONV_K - 1):], ub_ext[:, -(POOL_MAX - 1):], af_ext[:, -(FFN_K - 1):])
    return x, new_state


def setup_inputs(seed: int = 0) -> dict:
    key = jax.random.key(seed)
    ks = jax.random.split(key, 40)
    cnt = [0]

    def nrm(shape, scale=1.0):
        k = ks[cnt[0]]
        cnt[0] += 1
        return jax.random.normal(k, shape, jnp.float32) * scale

    def gain(shape):
        return 1.0 + 0.05 * nrm(shape)

    L = DEPTH
    return {
        'x_prompt': nrm((BATCH, SEQ, D_MODEL)),
        'x_sample': nrm((DEC_BATCH, DEC_SEQ, D_MODEL)),
        'cache_ckv': nrm((L, DEC_BATCH, PAST_LEN, KV_RANK)),
        'cache_krope': nrm((L, DEC_BATCH, PAST_LEN, ROPE_DIM)),
        'state_conv': nrm((L, DEC_BATCH, CONV_K - 1, D_CONV), 0.5),
        'state_pool': nrm((L, DEC_BATCH, POOL_MAX - 1, D_POOL)),
        'state_ffn': nrm((L, DEC_BATCH, FFN_K - 1, D_FF)),
        'c_prompt': nrm((BATCH, D_MODEL)),
        'c_sample': nrm((DEC_BATCH, D_MODEL)),
        'w_mod': nrm((L, D_MODEL, 6 * D_MODEL), 0.5 * D_MODEL ** -0.5),
        'b_mod': nrm((L, 6 * D_MODEL), 0.02),
        'g_pre_mix': gain((L, D_MODEL)),
        'g_post_mix': gain((L, D_MODEL)),
        'w_in': nrm((L, D_MODEL, N_IN), D_MODEL ** -0.5),
        'w_dwa': nrm((L, CONV_K, D_CONV), CONV_K ** -0.5),
        'b_dwa': nrm((L, D_CONV), 0.02),
        'ln_a_g': gain((L, D_CONV)),
        'ln_a_b': nrm((L, D_CONV), 0.02),
        'w_pa': nrm((L, D_CONV, D_MODEL), D_CONV ** -0.5),
        'w_pool': nrm((L, N_POOL_GROUPS, POOL_GROUP, POOL_OUT), POOL_GROUP ** -0.5),
        'pool_scale': gain((L, D_MODEL)),
        'g_q_lat': gain((L, Q_RANK)),
        'g_kv_lat': gain((L, KV_RANK)),
        'w_uq': nrm((L, Q_RANK, N_HEADS * (NOPE_DIM + ROPE_DIM)), Q_RANK ** -0.5),
        'w_uk': nrm((L, KV_RANK, N_HEADS, NOPE_DIM), KV_RANK ** -0.5),
        'w_uv': nrm((L, KV_RANK, N_HEADS, V_DIM), KV_RANK ** -0.5),
        'w_oc': nrm((L, N_HEADS * V_DIM, D_MODEL), (N_HEADS * V_DIM) ** -0.5),
        'w_out': nrm((L, D_MODEL, D_MODEL), D_MODEL ** -0.5),
        'g_pre_ffn': gain((L, D_MODEL)),
        'g_post_ffn': gain((L, D_MODEL)),
        'w_up': nrm((L, D_MODEL, 2 * D_FF), D_MODEL ** -0.5),
        'w_dwf': nrm((L, FFN_K, D_FF), FFN_K ** -0.5),
        'b_dwf': nrm((L, D_FF), 0.02),
        'w_down': nrm((L, D_FF, D_MODEL), D_FF ** -0.5),
    }


def reference(x_prompt, x_sample, cache_ckv, cache_krope, state_conv, state_pool, state_ffn,
              c_prompt, c_sample, w_mod, b_mod, g_pre_mix, g_post_mix, w_in, w_dwa, b_dwa,
              ln_a_g, ln_a_b, w_pa, w_pool, pool_scale, g_q_lat, g_kv_lat, w_uq, w_uk, w_uv,
              w_oc, w_out, g_pre_ffn, g_post_ffn, w_up, w_dwf, b_dwf, w_down):
    xp, xs = x_prompt, x_sample
    bp = x_prompt.shape[0]
    st_p = [[], [], [], [], []]
    st_s = [[], [], [], [], []]
    for l in range(DEPTH):
        p = dict(w_mod=w_mod[l], b_mod=b_mod[l], g_pre_mix=g_pre_mix[l], g_post_mix=g_post_mix[l],
                 w_in=w_in[l], w_dwa=w_dwa[l], b_dwa=b_dwa[l], ln_a_g=ln_a_g[l], ln_a_b=ln_a_b[l],
                 w_pa=w_pa[l], w_pool=w_pool[l], pool_scale=pool_scale[l], g_q_lat=g_q_lat[l],
                 g_kv_lat=g_kv_lat[l], w_uq=w_uq[l], w_uk=w_uk[l], w_uv=w_uv[l], w_oc=w_oc[l],
                 w_out=w_out[l], g_pre_ffn=g_pre_ffn[l], g_post_ffn=g_post_ffn[l], w_up=w_up[l],
                 w_dwf=w_dwf[l], b_dwf=b_dwf[l], w_down=w_down[l])
        xp, sp = trunk_layer(
            xp, c_prompt,
            jnp.zeros((bp, 0, KV_RANK), xp.dtype), jnp.zeros((bp, 0, ROPE_DIM), xp.dtype),
            jnp.zeros((bp, CONV_K - 1, D_CONV), xp.dtype), jnp.zeros((bp, POOL_MAX - 1, D_POOL), xp.dtype),
            jnp.zeros((bp, FFN_K - 1, D_FF), xp.dtype), 0, p)
        xs, ss = trunk_layer(xs, c_sample, cache_ckv[l], cache_krope[l], state_conv[l], state_pool[l],
                             state_ffn[l], PAST_LEN, p)
        for i in range(5):
            st_p[i].append(sp[i])
            st_s[i].append(ss[i])
    return (xp, xs,
            jnp.stack(st_p[0]), jnp.stack(st_p[1]), jnp.stack(st_p[2]), jnp.stack(st_p[3]), jnp.stack(st_p[4]),
            jnp.stack(st_s[0]), jnp.stack(st_s[1]), jnp.stack(st_s[2]), jnp.stack(st_s[3]), jnp.stack(st_s[4]))
```

```python
import functools

import jax
import jax.numpy as jnp
from jax import lax
from jax.experimental import pallas as pl
from jax.experimental.pallas import tpu as pltpu

F32 = jnp.float32
BF16 = jnp.bfloat16

D_MODEL = 2048
CHUNK = 64
CHUNK_SHIFT = 6
assert 1 << CHUNK_SHIFT == CHUNK
D_CONV = D_MODEL // 2
CONV_K = 31
D_POOL = D_MODEL // 2
POOL_WINDOWS = (2, 4, 8, 16)
POOL_MAX = 16
N_POOL_GROUPS = 4
POOL_GROUP = D_POOL // N_POOL_GROUPS
POOL_OUT = D_MODEL // N_POOL_GROUPS
N_HEADS = D_MODEL // 128
NOPE_DIM = 128
ROPE_DIM = 64
V_DIM = 128
Q_RANK = D_MODEL // 4
KV_RANK = D_MODEL // 4
ROPE_THETA = 10000.0
ATTN_SCALE = (NOPE_DIM + ROPE_DIM) ** -0.5
D_FF = 256 * ((8 * D_MODEL // 3 + 255) // 256)
FFN_K = 3
EPS = 1e-6
NEG = -1e30
OFF_B = 2 * D_CONV
OFF_R = OFF_B + D_POOL + Q_RANK + KV_RANK
OFF_G = OFF_R + ROPE_DIM

LANES = 128
SUBLANES = 8
VMEM_LIMIT_BYTES = 56 * 1024 * 1024

MID_Q = D_POOL
MID_KV = MID_Q + Q_RANK
MID_R = MID_KV + KV_RANK
MID_W = MID_R + 2 * LANES
MID_TN = MID_W // 3
CONV_HALO = 32
POOL_HALO = 16
FFN_HALO = 8


def _params(*sem):
    return pltpu.CompilerParams(dimension_semantics=sem, vmem_limit_bytes=VMEM_LIMIT_BYTES)


def _row_tiles(b, t):
    if t >= 512:
        return 1, 512
    return b, t


def _mod_kernel(c_ref, w_ref, b_ref, o_ref):
    c = c_ref[...]
    a = (c * jax.nn.sigmoid(c)).astype(BF16)
    o_ref[...] = jnp.dot(a, w_ref[...].astype(BF16), preferred_element_type=F32) + b_ref[...]


def _mod(c_all, w_mod, b_mod):
    nl, d, n = w_mod.shape
    bp = c_all.shape[0]
    tn = 1024
    return pl.pallas_call(
        _mod_kernel,
        grid=(nl, n // tn),
        in_specs=[pl.BlockSpec((bp, d), lambda l, j: (0, 0)),
                  pl.BlockSpec((None, d, tn), lambda l, j: (l, 0, j)),
                  pl.BlockSpec((None, 1, tn), lambda l, j: (l, 0, j))],
        out_specs=pl.BlockSpec((None, bp, tn), lambda l, j: (l, 0, j)),
        out_shape=jax.ShapeDtypeStruct((nl, bp, n), F32),
        compiler_params=_params("parallel", "parallel"),
        name="mod",
    )(c_all, w_mod, b_mod.reshape(nl, 1, n))


def _norm_mod_kernel(x_ref, g_ref, sc_ref, sh_ref, o_ref):
    x = x_ref[...]
    y = x * lax.rsqrt(jnp.mean(x * x, axis=-1, keepdims=True) + EPS) * g_ref[...]
    o_ref[...] = (y * (1.0 + sc_ref[...]) + sh_ref[...]).astype(o_ref.dtype)


def _norm_mod(x, g, scale, shift):
    b, t, d = x.shape
    nb, tt = _row_tiles(b, t)
    return pl.pallas_call(
        _norm_mod_kernel,
        grid=(b // nb, t // tt),
        in_specs=[pl.BlockSpec((nb, tt, d), lambda i, j: (i, j, 0)),
                  pl.BlockSpec((1, d), lambda i, j: (0, 0)),
                  pl.BlockSpec((nb, 1, d), lambda i, j: (i, 0, 0)),
                  pl.BlockSpec((nb, 1, d), lambda i, j: (i, 0, 0))],
        out_specs=pl.BlockSpec((nb, tt, d), lambda i, j: (i, j, 0)),
        out_shape=jax.ShapeDtypeStruct((b, t, d), BF16),
        compiler_params=_params("parallel", "parallel"),
        name="norm_mod",
    )(x, g.reshape(1, d), scale, shift)


def _mm_kernel(*refs, epi):
    a_ref, w_ref, o_ref = refs[0], refs[1], refs[-1]
    a = a_ref[...]
    z = jnp.dot(a, w_ref[...], preferred_element_type=F32)
    if epi == "sigmoid":
        z = jax.nn.sigmoid(z)
    elif epi == "glu":
        z = z * jax.nn.sigmoid(jnp.dot(a, refs[2][...], preferred_element_type=F32))
    o_ref[...] = z.astype(o_ref.dtype)


def _mm(a, w, out_dtype, *, tn, epi="none", w2=None, name="mm"):
    r, k = a.shape
    n = w.shape[1]
    tm = min(r, 1024)
    ws = [w] if w2 is None else [w, w2]
    return pl.pallas_call(
        functools.partial(_mm_kernel, epi=epi),
        grid=(r // tm, n // tn),
        in_specs=[pl.BlockSpec((tm, k), lambda i, j: (i, 0))]
        + [pl.BlockSpec((k, tn), lambda i, j: (0, j)) for _ in ws],
        out_specs=pl.BlockSpec((tm, tn), lambda i, j: (i, j)),
        out_shape=jax.ShapeDtypeStruct((r, n), out_dtype),
        compiler_params=_params("parallel", "parallel"),
        name=name,
    )(a, *ws)


def _conv_a_kernel(u_ref, prev_ref, hist_ref, w_ref, b_ref, g_ref, be_ref, o_ref, ext_ref, a_ref, *, tt):
    j = pl.program_id(1)
    ext_ref[0:CONV_HALO, :] = jnp.where(j == 0, hist_ref[...], prev_ref[...])
    ext_ref[CONV_HALO:, :] = u_ref[...]
    rc = 32
    cc = 512
    lead = CONV_HALO - (CONV_K - 1)

    def body(r, carry):
        r0 = pl.multiple_of(r * rc, rc)
        for c0 in range(0, D_CONV, cc):
            win = ext_ref[pl.ds(r0, rc + CONV_HALO), c0:c0 + cc]
            acc = jnp.zeros((rc, cc), F32)
            for k in range(CONV_K):
                acc = acc + w_ref[k:k + 1, c0:c0 + cc] * win[lead + k:lead + k + rc, :]
            a_ref[pl.ds(r0, rc), c0:c0 + cc] = acc
        return carry

    lax.fori_loop(0, tt // rc, body, 0)
    a = a_ref[...] + b_ref[...]
    mu = jnp.mean(a, axis=-1, keepdims=True)
    ac = a - mu
    var = jnp.mean(ac * ac, axis=-1, keepdims=True)
    y = ac * lax.rsqrt(var + EPS) * g_ref[...] + be_ref[...]
    o_ref[...] = (y * jax.nn.sigmoid(y)).astype(o_ref.dtype)


def _conv_a(u, hist, w_dw, b_dw, ln_g, ln_b):
    b, t, c = u.shape
    tt = min(t, 256)
    hb = tt // CONV_HALO
    vec = lambda i, j: (0, 0)
    return pl.pallas_call(
        functools.partial(_conv_a_kernel, tt=tt),
        grid=(b, t // tt),
        in_specs=[pl.BlockSpec((None, tt, c), lambda i, j: (i, j, 0)),
                  pl.BlockSpec((None, CONV_HALO, c), lambda i, j: (i, jnp.maximum(j * hb - 1, 0), 0)),
                  pl.BlockSpec((None, CONV_HALO, c), lambda i, j: (i, 0, 0)),
                  pl.BlockSpec((CONV_K, c), vec),
                  pl.BlockSpec((1, c), vec), pl.BlockSpec((1, c), vec), pl.BlockSpec((1, c), vec)],
        out_specs=pl.BlockSpec((None, tt, c), lambda i, j: (i, j, 0)),
        out_shape=jax.ShapeDtypeStruct((b, t, c), BF16),
        scratch_shapes=[pltpu.VMEM((tt + CONV_HALO, c), F32), pltpu.VMEM((tt, c), F32)],
        compiler_params=_params("parallel", "parallel"),
        name="conv_a",
    )(u, u, hist, w_dw, b_dw.reshape(1, c), ln_g.reshape(1, c), ln_b.reshape(1, c))


def _pool_kernel(z_ref, prev_ref, hist_ref, o_ref, ext_ref, *, tt, pos0):
    j = pl.program_id(1)
    ext_ref[0:POOL_HALO, :] = jnp.where(j == 0, hist_ref[...], prev_ref[...])
    ext_ref[POOL_HALO:, :] = z_ref[...]
    pos = pos0 + j * tt + lax.broadcasted_iota(jnp.int32, (tt, 1), 0)
    for g, w in enumerate(POOL_WINDOWS):
        c0 = g * POOL_GROUP
        cur = ext_ref[:, c0:c0 + POOL_GROUP]
        sh = 1
        while sh < w:
            cur = cur + pltpu.roll(cur, sh, 0)
            sh *= 2
        win = cur[POOL_HALO:, :]
        cnt = jnp.minimum(w, pos + 1).astype(F32)
        m = win / cnt - ext_ref[POOL_HALO:, c0:c0 + POOL_GROUP]
        o_ref[:, c0:c0 + POOL_GROUP] = m.astype(o_ref.dtype)


def _pool(zmid, hist, pos0):
    b, t, _ = zmid.shape
    c = D_POOL
    tt = min(t, 256)
    hb = tt // POOL_HALO
    return pl.pallas_call(
        functools.partial(_pool_kernel, tt=tt, pos0=pos0),
        grid=(b, t // tt),
        in_specs=[pl.BlockSpec((None, tt, c), lambda i, j: (i, j, 0)),
                  pl.BlockSpec((None, POOL_HALO, c), lambda i, j: (i, jnp.maximum(j * hb - 1, 0), 0)),
                  pl.BlockSpec((None, POOL_HALO, c), lambda i, j: (i, 0, 0))],
        out_specs=pl.BlockSpec((None, tt, c), lambda i, j: (i, j, 0)),
        out_shape=jax.ShapeDtypeStruct((b, t, c), BF16),
        scratch_shapes=[pltpu.VMEM((tt + POOL_HALO, c), F32)],
        compiler_params=_params("parallel", "parallel"),
        name="pool",
    )(zmid, zmid, hist)


def _pool_mm_kernel(a_ref, w_ref, s_ref, o_ref):
    z = jnp.dot(a_ref[...], w_ref[...], preferred_element_type=F32)
    o_ref[...] = (z * s_ref[...]).astype(o_ref.dtype)


def _pool_mm(m, w_pool, pool_scale):
    r = m.shape[0]
    tm = min(r, 1024)
    return pl.pallas_call(
        _pool_mm_kernel,
        grid=(r // tm, N_POOL_GROUPS),
        in_specs=[pl.BlockSpec((tm, POOL_GROUP), lambda i, g: (i, g)),
                  pl.BlockSpec((None, POOL_GROUP, POOL_OUT), lambda i, g: (g, 0, 0)),
                  pl.BlockSpec((1, POOL_OUT), lambda i, g: (0, g))],
        out_specs=pl.BlockSpec((tm, POOL_OUT), lambda i, g: (i, g)),
        out_shape=jax.ShapeDtypeStruct((r, D_MODEL), BF16),
        compiler_params=_params("parallel", "parallel"),
        name="pool_mm",
    )(m, w_pool, pool_scale.reshape(1, D_MODEL))


def _rot_half(x):
    lane = lax.broadcasted_iota(jnp.int32, x.shape, 1)
    first = (lane % ROPE_DIM) < (ROPE_DIM // 2)
    return jnp.where(first, pltpu.roll(x, LANES - ROPE_DIM // 2, 1), pltpu.roll(x, ROPE_DIM // 2, 1))


def _lat_kernel(z_ref, gq_ref, gkv_ref, cos_ref, sin_ref, ql_ref, ckv_ref, ckvb_ref, kr_ref, krb_ref):
    nb, tt, _ = z_ref.shape
    zq = z_ref[:, :, MID_Q:MID_KV]
    ql = zq * lax.rsqrt(jnp.mean(zq * zq, axis=-1, keepdims=True) + EPS) * gq_ref[...]
    ql_ref[...] = ql.astype(ql_ref.dtype)
    zkv = z_ref[:, :, MID_KV:MID_R]
    ckv = zkv * lax.rsqrt(jnp.mean(zkv * zkv, axis=-1, keepdims=True) + EPS) * gkv_ref[...]
    ckv_ref[...] = ckv
    ckvb_ref[...] = ckv.astype(BF16)
    zr = z_ref[:, :, MID_R:MID_R + LANES]
    rot = _rot_half(zr.reshape(nb * tt, LANES)).reshape(nb, tt, LANES)
    kr = (zr * cos_ref[...] + rot * sin_ref[...])[:, :, :ROPE_DIM]
    kr_ref[...] = kr
    krb_ref[...] = kr.astype(BF16)


def _lat(zmid, g_q, g_kv, cos, sin):
    b, t, _ = zmid.shape
    nb, tt = _row_tiles(b, t)
    row = lambda i, j: (i, j, 0)
    vec = lambda i, j: (0, 0)
    shapes = (jax.ShapeDtypeStruct((b, t, Q_RANK), BF16),
              jax.ShapeDtypeStruct((b, t, KV_RANK), F32),
              jax.ShapeDtypeStruct((b, t, KV_RANK), BF16),
              jax.ShapeDtypeStruct((b, t, ROPE_DIM), F32),
              jax.ShapeDtypeStruct((b, t, ROPE_DIM), BF16))
    return pl.pallas_call(
        _lat_kernel,
        grid=(b // nb, t // tt),
        in_specs=[pl.BlockSpec((nb, tt, MID_W), row),
                  pl.BlockSpec((1, Q_RANK), vec), pl.BlockSpec((1, KV_RANK), vec),
                  pl.BlockSpec((tt, LANES), lambda i, j: (j, 0)),
                  pl.BlockSpec((tt, LANES), lambda i, j: (j, 0))],
        out_specs=(pl.BlockSpec((nb, tt, Q_RANK), row), pl.BlockSpec((nb, tt, KV_RANK), row),
                   pl.BlockSpec((nb, tt, KV_RANK), row), pl.BlockSpec((nb, tt, ROPE_DIM), row),
                   pl.BlockSpec((nb, tt, ROPE_DIM), row)),
        out_shape=shapes,
        compiler_params=_params("parallel", "parallel"),
        name="lat",
    )(zmid, g_q.reshape(1, Q_RANK), g_kv.reshape(1, KV_RANK), cos, sin)


def _qprep_kernel(q_ref, wuk_ref, cos_ref, sin_ref, qa_ref, qr_ref):
    nb, tt, _ = q_ref.shape
    rows = nb * tt
    nope = N_HEADS * NOPE_DIM
    for h in range(N_HEADS):
        qn = q_ref[:, :, h * NOPE_DIM:(h + 1) * NOPE_DIM].reshape(rows, NOPE_DIM).astype(BF16)
        qa = jnp.dot(qn, wuk_ref[h], preferred_element_type=F32) * ATTN_SCALE
        qa_ref[h] = qa.reshape(nb, tt, KV_RANK).astype(qa_ref.dtype)
    cos = cos_ref[...]
    sin = sin_ref[...]
    for c in range(N_HEADS * ROPE_DIM // LANES):
        x = q_ref[:, :, nope + c * LANES:nope + (c + 1) * LANES]
        rot = _rot_half(x.reshape(rows, LANES)).reshape(nb, tt, LANES)
        r = ((x * cos + rot * sin) * ATTN_SCALE).astype(qr_ref.dtype)
        qr_ref[2 * c] = r[:, :, :ROPE_DIM]
        qr_ref[2 * c + 1] = r[:, :, ROPE_DIM:]


def _qprep(q, w_ukt, cos, sin):
    b, t, qw = q.shape
    nb, tt = (1, 256) if t >= 256 else (b, t)
    return pl.pallas_call(
        _qprep_kernel,
        grid=(b // nb, t // tt),
        in_specs=[pl.BlockSpec((nb, tt, qw), lambda i, j: (i, j, 0)),
                  pl.BlockSpec((N_HEADS, NOPE_DIM, KV_RANK), lambda i, j: (0, 0, 0)),
                  pl.BlockSpec((tt, LANES), lambda i, j: (j, 0)),
                  pl.BlockSpec((tt, LANES), lambda i, j: (j, 0))],
        out_specs=(pl.BlockSpec((N_HEADS, nb, tt, KV_RANK), lambda i, j: (0, i, j, 0)),
                   pl.BlockSpec((N_HEADS, nb, tt, ROPE_DIM), lambda i, j: (0, i, j, 0))),
        out_shape=(jax.ShapeDtypeStruct((N_HEADS, b, t, KV_RANK), BF16),
                   jax.ShapeDtypeStruct((N_HEADS, b, t, ROPE_DIM), BF16)),
        compiler_params=_params("parallel", "parallel"),
        name="qprep",
    )(q, w_ukt, cos, sin)


def _last_kv_block(i, *, tq, tk, pos0, nk):
    last_pos = (pos0 + (i + 1) * tq - 1) // CHUNK * CHUNK + CHUNK - 1
    return jnp.minimum(last_pos // tk, nk - 1)


def _attn_kernel(qa_ref, qr_ref, k_ref, kr_ref, wuv_ref, o_ref, m_ref, l_ref, acc_ref,
                 *, tq, tk, pos0, s_valid, nk):
    i = pl.program_id(1)
    kk = pl.program_id(2)
    rows = N_HEADS * tq

    @pl.when(kk == 0)
    def _():
        m_ref[...] = jnp.full(m_ref.shape, NEG, F32)
        l_ref[...] = jnp.zeros(l_ref.shape, F32)
        acc_ref[...] = jnp.zeros(acc_ref.shape, F32)

    @pl.when(kk <= _last_kv_block(i, tq=tq, tk=tk, pos0=pos0, nk=nk))
    def _():
        qa = qa_ref[...].reshape(rows, KV_RANK)
        qr = qr_ref[...].reshape(rows, ROPE_DIM)
        k = k_ref[...]
        dn = (((1,), (1,)), ((), ()))
        s = lax.dot_general(qa, k, dn, preferred_element_type=F32)
        s = s + lax.dot_general(qr, kr_ref[...], dn, preferred_element_type=F32)
        qpos = pos0 + i * tq + lax.broadcasted_iota(jnp.int32, (tq, tk), 0)
        kpos = kk * tk + lax.broadcasted_iota(jnp.int32, (tq, tk), 1)
        ok = jnp.logical_and((kpos >> CHUNK_SHIFT) <= (qpos >> CHUNK_SHIFT), kpos < s_valid)
        s = jnp.where(ok[None], s.reshape(N_HEADS, tq, tk), NEG).reshape(rows, tk)
        m_old = m_ref[...]
        m_new = jnp.maximum(m_old, jnp.max(s, axis=-1, keepdims=True))
        alpha = jnp.exp(m_old - m_new)
        p = jnp.exp(s - m_new)
        l_ref[...] = alpha * l_ref[...] + jnp.sum(p, axis=-1, keepdims=True)
        acc_ref[...] = alpha * acc_ref[...] + jnp.dot(p.astype(BF16), k, preferred_element_type=F32)
        m_ref[...] = m_new

    @pl.when(kk == nk - 1)
    def _():
        o_lat = (acc_ref[...] / l_ref[...]).astype(BF16).reshape(N_HEADS, tq, KV_RANK)
        for h in range(N_HEADS):
            o_h = jnp.dot(o_lat[h], wuv_ref[h], preferred_element_type=F32)
            o_ref[:, h * V_DIM:(h + 1) * V_DIM] = o_h.astype(o_ref.dtype)


def _attn(q_abs, q_rope, ckv_all, krope_all, w_uv, *, pos0, s_valid):
    _, b, t, _ = q_abs.shape
    s_len = ckv_all.shape[1]
    tq = min(t, 128)
    tk = 512
    nk = s_len // tk
    last = functools.partial(_last_kv_block, tq=tq, tk=tk, pos0=pos0, nk=nk)
    kv_idx = lambda bi, i, kk: (bi, jnp.minimum(kk, last(i)), 0)
    return pl.pallas_call(
        functools.partial(_attn_kernel, tq=tq, tk=tk, pos0=pos0, s_valid=s_valid, nk=nk),
        grid=(b, t // tq, nk),
        in_specs=[pl.BlockSpec((N_HEADS, None, tq, KV_RANK), lambda bi, i, kk: (0, bi, i, 0)),
                  pl.BlockSpec((N_HEADS, None, tq, ROPE_DIM), lambda bi, i, kk: (0, bi, i, 0)),
                  pl.BlockSpec((None, tk, KV_RANK), kv_idx),
                  pl.BlockSpec((None, tk, ROPE_DIM), kv_idx),
                  pl.BlockSpec((N_HEADS, KV_RANK, V_DIM), lambda bi, i, kk: (0, 0, 0))],
        out_specs=pl.BlockSpec((None, tq, N_HEADS * V_DIM), lambda bi, i, kk: (bi, i, 0)),
        out_shape=jax.ShapeDtypeStruct((b, t, N_HEADS * V_DIM), BF16),
        scratch_shapes=[pltpu.VMEM((N_HEADS * tq, 1), F32), pltpu.VMEM((N_HEADS * tq, 1), F32),
                        pltpu.VMEM((N_HEADS * tq, KV_RANK), F32)],
        compiler_params=_params("parallel", "parallel", "arbitrary"),
        name="attn",
    )(q_abs, q_rope, ckv_all, krope_all, w_uv)


def _mm_res_kernel(*refs, merge):
    if merge:
        g0, g1, g2, oa, ob, oc, w_ref, x_ref, gate_ref, gn_ref, o_ref, acc_ref = refs
    else:
        a_ref, w_ref, x_ref, gate_ref, gn_ref, o_ref, acc_ref = refs
    kk = pl.program_id(2)
    nb, tt, d = x_ref.shape

    @pl.when(kk == 0)
    def _():
        acc_ref[...] = jnp.zeros(acc_ref.shape, F32)

    if merge:
        a = (g0[...].astype(F32) * oa[...].astype(F32) + g1[...].astype(F32) * ob[...].astype(F32)
             + g2[...].astype(F32) * oc[...].astype(F32)).astype(BF16)
    else:
        a = a_ref[...]
    a = a.reshape(nb * tt, a.shape[-1])
    acc_ref[...] += jnp.dot(a, w_ref[...], preferred_element_type=F32)

    @pl.when(kk == pl.num_programs(2) - 1)
    def _():
        y = acc_ref[...].reshape(nb, tt, d)
        yn = y * lax.rsqrt(jnp.mean(y * y, axis=-1, keepdims=True) + EPS) * gn_ref[...]
        o_ref[...] = x_ref[...] + gate_ref[...] * yn


def _mm_res(acts, w, x, gate, g_norm, *, merge, name):
    b, t, d = x.shape
    k = w.shape[0]
    nb, tt = _row_tiles(b, t)
    tk = 512
    nkb = k // tk
    row = lambda i, j, kk: (i, j, kk)
    if merge:
        gates, oa, ob, oc = acts
        ins = [gates, gates, gates, oa, ob, oc]
        branch = lambda o, i, j, kk: (i, j, o * nkb + kk)
        specs = [pl.BlockSpec((nb, tt, tk), functools.partial(branch, o)) for o in range(3)]
        specs += [pl.BlockSpec((nb, tt, tk), row)] * 3
    else:
        ins = [acts]
        specs = [pl.BlockSpec((nb, tt, tk), row)]
    full = lambda i, j, kk: (i, j, 0)
    return pl.pallas_call(
        functools.partial(_mm_res_kernel, merge=merge),
        grid=(b // nb, t // tt, nkb),
        in_specs=specs + [pl.BlockSpec((tk, d), lambda i, j, kk: (kk, 0)),
                          pl.BlockSpec((nb, tt, d), full),
                          pl.BlockSpec((nb, 1, d), lambda i, j, kk: (i, 0, 0)),
                          pl.BlockSpec((1, d), lambda i, j, kk: (0, 0))],
        out_specs=pl.BlockSpec((nb, tt, d), full),
        out_shape=jax.ShapeDtypeStruct((b, t, d), F32),
        scratch_shapes=[pltpu.VMEM((nb * tt, d), F32)],
        compiler_params=_params("parallel", "parallel", "arbitrary"),
        name=name,
    )(*ins, w, x, gate, g_norm.reshape(1, d))


def _ffn_act_kernel(ug_ref, uv_ref, prev_ref, hist_ref, w_ref, b_ref, o_ref, ext_ref, *, tt):
    j = pl.program_id(1)
    ext_ref[0:FFN_HALO, :] = jnp.where(j == 0, hist_ref[...], prev_ref[...].astype(F32))
    ext_ref[FFN_HALO:, :] = ug_ref[...].astype(F32)
    lead = FFN_HALO - (FFN_K - 1)
    conv = b_ref[...]
    for k in range(FFN_K):
        conv = conv + w_ref[k:k + 1, :] * ext_ref[lead + k:lead + k + tt, :]
    o_ref[...] = (conv * jax.nn.sigmoid(conv) * uv_ref[...].astype(F32)).astype(o_ref.dtype)


def _ffn_act(up, hist, w_dw, b_dw):
    b, t, _ = up.shape
    tt = min(t, 512)
    tc = 512 if t >= 512 else D_FF
    ncb = D_FF // tc
    hb = tt // FFN_HALO
    return pl.pallas_call(
        functools.partial(_ffn_act_kernel, tt=tt),
        grid=(b, t // tt, ncb),
        in_specs=[pl.BlockSpec((None, tt, tc), lambda i, j, c: (i, j, c)),
                  pl.BlockSpec((None, tt, tc), lambda i, j, c: (i, j, ncb + c)),
                  pl.BlockSpec((None, FFN_HALO, tc), lambda i, j, c: (i, jnp.maximum(j * hb - 1, 0), c)),
                  pl.BlockSpec((None, FFN_HALO, tc), lambda i, j, c: (i, 0, c)),
                  pl.BlockSpec((FFN_K, tc), lambda i, j, c: (0, c)),
                  pl.BlockSpec((1, tc), lambda i, j, c: (0, c))],
        out_specs=pl.BlockSpec((None, tt, tc), lambda i, j, c: (i, j, c)),
        out_shape=jax.ShapeDtypeStruct((b, t, D_FF), BF16),
        scratch_shapes=[pltpu.VMEM((tt + FFN_HALO, tc), F32)],
        compiler_params=_params("parallel", "parallel", "parallel"),
        name="ffn_act",
    )(up, up, up, hist, w_dw, b_dw.reshape(1, D_FF))


def _rope_tables(pos0, t):
    half = ROPE_DIM // 2
    inv = ROPE_THETA ** (-jnp.arange(half, dtype=F32) / half)
    pos = (pos0 + jnp.arange(t, dtype=jnp.int32)).astype(F32)
    ang = pos[:, None] * inv[None, :]
    cos, sin = jnp.cos(ang), jnp.sin(ang)
    return (jnp.concatenate([cos, cos, cos, cos], axis=-1),
            jnp.concatenate([-sin, sin, -sin, sin], axis=-1))


def _front_pad(a, rows):
    return jnp.pad(a, ((0, 0), (rows - a.shape[1], 0), (0, 0)))


def _tail(hist, new, n):
    keep = min(new.shape[1], n)
    return jnp.concatenate([hist, new[:, new.shape[1] - keep:].astype(F32)], axis=1)[:, -n:]


def _layer(x, mod, past_ckv, past_krope, hist_conv, hist_pool, hist_ffn, pos0, p):
    b, t, d = x.shape
    r = b * t
    shift1, scale1, gate1, shift2, scale2, gate2 = (mod[:, i] for i in range(6))
    cos, sin = _rope_tables(pos0, t)

    h = _norm_mod(x, p["g_pre_mix"], scale1, shift1).reshape(r, d)
    u_a = _mm(h, p["w_a1"], F32, tn=512, epi="glu", w2=p["w_a2"], name="mm_glu").reshape(b, t, D_CONV)
    zmid = _mm(h, p["w_mid"], F32, tn=MID_TN, name="mm_mid").reshape(b, t, MID_W)
    gates = _mm(h, p["w_g"], BF16, tn=512, epi="sigmoid", name="mm_gates").reshape(b, t, 3 * d)

    a_act = _conv_a(u_a, _front_pad(hist_conv, CONV_HALO), p["w_dwa"], p["b_dwa"], p["ln_a_g"], p["ln_a_b"])
    out_a = _mm(a_act.reshape(r, D_CONV), p["w_pa"], BF16, tn=512, name="mm_pa").reshape(b, t, d)

    m = _pool(zmid, _front_pad(hist_pool, POOL_HALO), pos0)
    out_b = _pool_mm(m.reshape(r, D_POOL), p["w_pool"], p["pool_scale"]).reshape(b, t, d)

    q_lat, ckv, ckv_b, krope, krope_b = _lat(zmid, p["g_q_lat"], p["g_kv_lat"], cos, sin)
    q = _mm(q_lat.reshape(r, Q_RANK), p["w_uq"], F32, tn=512, name="mm_uq").reshape(b, t, -1)
    q_abs, q_rope = _qprep(q, p["w_ukt"], cos, sin)
    s_valid = past_ckv.shape[1] + t
    s_pad = -(-s_valid // 512) * 512
    kv_pad = ((0, 0), (0, s_pad - s_valid), (0, 0))
    ckv_all = jnp.pad(jnp.concatenate([past_ckv.astype(BF16), ckv_b], axis=1), kv_pad)
    krope_all = jnp.pad(jnp.concatenate([past_krope.astype(BF16), krope_b], axis=1), kv_pad)
    o = _attn(q_abs, q_rope, ckv_all, krope_all, p["w_uv"], pos0=pos0, s_valid=s_valid)
    out_c = _mm(o.reshape(r, d), p["w_oc"], BF16, tn=512, name="mm_oc").reshape(b, t, d)

    x = _mm_res((gates, out_a, out_b, out_c), p["w_out"], x, gate1, p["g_post_mix"], merge=True, name="mm_out")

    h2 = _norm_mod(x, p["g_pre_ffn"], scale2, shift2).reshape(r, d)
    up = _mm(h2, p["w_up"], BF16, tn=512, name="mm_up").reshape(b, t, 2 * D_FF)
    act = _ffn_act(up, _front_pad(hist_ffn, FFN_HALO), p["w_dwf"], p["b_dwf"])
    x = _mm_res(act, p["w_down"], x, gate2, p["g_post_ffn"], merge=False, name="mm_down")

    state = (ckv, krope,
             _tail(hist_conv, u_a, CONV_K - 1),
             _tail(hist_pool, zmid[:, :, :D_POOL], POOL_MAX - 1),
             _tail(hist_ffn, up[:, :, :D_FF], FFN_K - 1))
    return x, state


def _prep_weights(l, w):
    w_in = w["w_in"][l]
    w_mid = jnp.pad(w_in[:, OFF_B:OFF_G], ((0, 0), (0, MID_W - (OFF_G - OFF_B))))
    w_uq = w["w_uq"][l].reshape(Q_RANK, N_HEADS, NOPE_DIM + ROPE_DIM)
    w_uq = jnp.concatenate([w_uq[:, :, :NOPE_DIM].reshape(Q_RANK, -1), w_uq[:, :, NOPE_DIM:].reshape(Q_RANK, -1)], axis=1)
    p = dict(
        w_a1=w_in[:, :D_CONV].astype(BF16), w_a2=w_in[:, D_CONV:OFF_B].astype(BF16),
        w_mid=w_mid.astype(BF16), w_g=w_in[:, OFF_G:].astype(BF16),
        w_pa=w["w_pa"][l].astype(BF16), w_pool=w["w_pool"][l].astype(BF16),
        w_uq=w_uq.astype(BF16),
        w_ukt=jnp.transpose(w["w_uk"][l], (1, 2, 0)).astype(BF16),
        w_uv=jnp.transpose(w["w_uv"][l], (1, 0, 2)).astype(BF16),
        w_oc=w["w_oc"][l].astype(BF16), w_out=w["w_out"][l].astype(BF16),
        w_up=w["w_up"][l].astype(BF16), w_down=w["w_down"][l].astype(BF16),
    )
    for name in ("g_pre_mix", "g_post_mix", "w_dwa", "b_dwa", "ln_a_g", "ln_a_b", "pool_scale", "g_q_lat",
                 "g_kv_lat", "g_pre_ffn", "g_post_ffn", "w_dwf", "b_dwf"):
        p[name] = w[name][l]
    return p


def kernel(x_prompt, x_sample, cache_ckv, cache_krope, state_conv, state_pool, state_ffn, c_prompt, c_sample, w_mod, b_mod, g_pre_mix, g_post_mix, w_in, w_dwa, b_dwa, ln_a_g, ln_a_b, w_pa, w_pool, pool_scale, g_q_lat, g_kv_lat, w_uq, w_uk, w_uv, w_oc, w_out, g_pre_ffn, g_post_ffn, w_up, w_dwf, b_dwf, w_down):
    weights = dict(g_pre_mix=g_pre_mix, g_post_mix=g_post_mix, w_in=w_in, w_dwa=w_dwa, b_dwa=b_dwa,
                   ln_a_g=ln_a_g, ln_a_b=ln_a_b, w_pa=w_pa, w_pool=w_pool, pool_scale=pool_scale,
                   g_q_lat=g_q_lat, g_kv_lat=g_kv_lat, w_uq=w_uq, w_uk=w_uk, w_uv=w_uv, w_oc=w_oc,
                   w_out=w_out, g_pre_ffn=g_pre_ffn, g_post_ffn=g_post_ffn, w_up=w_up, w_dwf=w_dwf,
                   b_dwf=b_dwf, w_down=w_down)
    depth = w_mod.shape[0]
    bp, bs = x_prompt.shape[0], x_sample.shape[0]
    past_len = cache_ckv.shape[2]
    d = x_prompt.shape[-1]

    rows = -(-(bp + bs) // SUBLANES) * SUBLANES
    c_all = jnp.pad(jnp.concatenate([c_prompt, c_sample], axis=0), ((0, rows - bp - bs), (0, 0)))
    mod_all = _mod(c_all, w_mod, b_mod)

    xp, xs = x_prompt, x_sample
    st_p = [[] for _ in range(5)]
    st_s = [[] for _ in range(5)]
    for l in range(depth):
        p = _prep_weights(l, weights)
        mod_p = mod_all[l, :bp].reshape(bp, 6, 1, d)
        mod_s = mod_all[l, bp:bp + bs].reshape(bs, 6, 1, d)
        xp, sp = _layer(xp, mod_p,
                        jnp.zeros((bp, 0, KV_RANK), F32), jnp.zeros((bp, 0, ROPE_DIM), F32),
                        jnp.zeros((bp, CONV_K - 1, D_CONV), F32), jnp.zeros((bp, POOL_MAX - 1, D_POOL), F32),
                        jnp.zeros((bp, FFN_K - 1, D_FF), F32), 0, p)
        xs, ss = _layer(xs, mod_s, cache_ckv[l], cache_krope[l], state_conv[l], state_pool[l], state_ffn[l],
                        past_len, p)
        for i in range(5):
            st_p[i].append(sp[i])
            st_s[i].append(ss[i])
    return (xp, xs) + tuple(jnp.stack(s) for s in st_p) + tuple(jnp.stack(s) for s in st_s)
```

```python
import functools

import jax
import jax.numpy as jnp
from jax import lax
from jax.experimental import pallas as pl
from jax.experimental.pallas import tpu as pltpu

F32 = jnp.float32
BF16 = jnp.bfloat16

D_MODEL = 2048
CHUNK = 64
CHUNK_SHIFT = 6
assert 1 << CHUNK_SHIFT == CHUNK
D_CONV = D_MODEL // 2
CONV_K = 31
D_POOL = D_MODEL // 2
POOL_WINDOWS = (2, 4, 8, 16)
POOL_MAX = 16
N_POOL_GROUPS = 4
POOL_GROUP = D_POOL // N_POOL_GROUPS
POOL_OUT = D_MODEL // N_POOL_GROUPS
N_HEADS = D_MODEL // 128
NOPE_DIM = 128
ROPE_DIM = 64
V_DIM = 128
Q_RANK = D_MODEL // 4
KV_RANK = D_MODEL // 4
ROPE_THETA = 10000.0
ATTN_SCALE = (NOPE_DIM + ROPE_DIM) ** -0.5
LOG2E = 1.4426950408889634
QK_DIM = 256
V_AUG = 256
D_FF = 256 * ((8 * D_MODEL // 3 + 255) // 256)
FFN_K = 3
EPS = 1e-6
NEG = -1e30
OFF_B = 2 * D_CONV
OFF_R = OFF_B + D_POOL + Q_RANK + KV_RANK
OFF_G = OFF_R + ROPE_DIM

LANES = 128
SUBLANES = 8
VMEM_LIMIT_BYTES = 56 * 1024 * 1024

MID_Q = D_POOL
MID_KV = MID_Q + Q_RANK
MID_R = MID_KV + KV_RANK
MID_W = MID_R + 2 * LANES
MID_TN = MID_W // 3
CONV_HALO = 32
POOL_HALO = 16
FFN_HALO = 8


def _params(*sem):
    return pltpu.CompilerParams(dimension_semantics=sem, vmem_limit_bytes=VMEM_LIMIT_BYTES)


def _sigmoid(x):
    return 0.5 + 0.5 * jnp.tanh(0.5 * x)


def _silu(x):
    h = 0.5 * x
    return h + h * jnp.tanh(h)


def _row_tiles(b, t):
    if t >= 512:
        return 1, 512
    return b, t


def _mod_kernel(c_ref, w_ref, b_ref, o_ref):
    c = c_ref[...]
    a = _silu(c).astype(BF16)
    o_ref[...] = jnp.dot(a, w_ref[...].astype(BF16), preferred_element_type=F32) + b_ref[...]


def _mod(c_all, w_mod, b_mod):
    nl, d, n = w_mod.shape
    bp = c_all.shape[0]
    tn = 1024
    return pl.pallas_call(
        _mod_kernel,
        grid=(nl, n // tn),
        in_specs=[pl.BlockSpec((bp, d), lambda l, j: (0, 0)),
                  pl.BlockSpec((None, d, tn), lambda l, j: (l, 0, j)),
                  pl.BlockSpec((None, 1, tn), lambda l, j: (l, 0, j))],
        out_specs=pl.BlockSpec((None, bp, tn), lambda l, j: (l, 0, j)),
        out_shape=jax.ShapeDtypeStruct((nl, bp, n), F32),
        compiler_params=_params("parallel", "parallel"),
        name="mod",
    )(c_all, w_mod, b_mod.reshape(nl, 1, n))


def _norm_mod_kernel(x_ref, g_ref, sc_ref, sh_ref, o_ref):
    x = x_ref[...]
    y = x * lax.rsqrt(jnp.mean(x * x, axis=-1, keepdims=True) + EPS) * g_ref[...]
    o_ref[...] = (y * (1.0 + sc_ref[...]) + sh_ref[...]).astype(o_ref.dtype)


def _norm_mod(x, g, scale, shift):
    b, t, d = x.shape
    nb, tt = _row_tiles(b, t)
    return pl.pallas_call(
        _norm_mod_kernel,
        grid=(b // nb, t // tt),
        in_specs=[pl.BlockSpec((nb, tt, d), lambda i, j: (i, j, 0)),
                  pl.BlockSpec((1, d), lambda i, j: (0, 0)),
                  pl.BlockSpec((nb, 1, d), lambda i, j: (i, 0, 0)),
                  pl.BlockSpec((nb, 1, d), lambda i, j: (i, 0, 0))],
        out_specs=pl.BlockSpec((nb, tt, d), lambda i, j: (i, j, 0)),
        out_shape=jax.ShapeDtypeStruct((b, t, d), BF16),
        compiler_params=_params("parallel", "parallel"),
        name="norm_mod",
    )(x, g.reshape(1, d), scale, shift)


def _mm_kernel(*refs, epi):
    a_ref, w_ref, o_ref = refs[0], refs[1], refs[-1]
    a = a_ref[...]
    z = jnp.dot(a, w_ref[...], preferred_element_type=F32)
    if epi == "sigmoid":
        z = _sigmoid(z)
    elif epi == "glu":
        z = z * _sigmoid(jnp.dot(a, refs[2][...], preferred_element_type=F32))
    o_ref[...] = z.astype(o_ref.dtype)


def _mm(a, w, out_dtype, *, tn, epi="none", w2=None, name="mm"):
    r, k = a.shape
    n = w.shape[1]
    tm = min(r, 1024)
    ws = [w] if w2 is None else [w, w2]
    return pl.pallas_call(
        functools.partial(_mm_kernel, epi=epi),
        grid=(r // tm, n // tn),
        in_specs=[pl.BlockSpec((tm, k), lambda i, j: (i, 0))]
        + [pl.BlockSpec((k, tn), lambda i, j: (0, j)) for _ in ws],
        out_specs=pl.BlockSpec((tm, tn), lambda i, j: (i, j)),
        out_shape=jax.ShapeDtypeStruct((r, n), out_dtype),
        compiler_params=_params("parallel", "parallel"),
        name=name,
    )(a, *ws)


def _conv_a_kernel(u_ref, prev_ref, hist_ref, w_ref, b_ref, g_ref, be_ref, o_ref, ext_ref, sh_ref, a_ref, *, tt):
    j = pl.program_id(1)
    ext_ref[0:CONV_HALO, :] = jnp.where(j == 0, hist_ref[...], prev_ref[...])
    ext_ref[CONV_HALO:, :] = u_ref[...]
    sh_rows = sh_ref.shape[1]
    for s in range(1, SUBLANES):
        sh_ref[s - 1] = ext_ref[s:s + sh_rows, :]
    rc = 32
    cc = 512
    lead = CONV_HALO - (CONV_K - 1)

    def body(r, carry):
        r0 = pl.multiple_of(r * rc, rc)
        for c0 in range(0, D_CONV, cc):
            acc = jnp.zeros((rc, cc), F32)
            for k in range(CONV_K):
                s = (lead + k) % SUBLANES
                row = pl.multiple_of(r0 + (lead + k - s), SUBLANES)
                if s == 0:
                    x = ext_ref[pl.ds(row, rc), c0:c0 + cc]
                else:
                    x = sh_ref[s - 1, pl.ds(row, rc), c0:c0 + cc]
                acc = acc + w_ref[k:k + 1, c0:c0 + cc] * x
            a_ref[pl.ds(r0, rc), c0:c0 + cc] = acc
        return carry

    lax.fori_loop(0, tt // rc, body, 0)
    a = a_ref[...] + b_ref[...]
    mu = jnp.mean(a, axis=-1, keepdims=True)
    ac = a - mu
    var = jnp.mean(ac * ac, axis=-1, keepdims=True)
    y = ac * lax.rsqrt(var + EPS) * g_ref[...] + be_ref[...]
    o_ref[...] = _silu(y).astype(o_ref.dtype)


def _conv_a(u, hist, w_dw, b_dw, ln_g, ln_b):
    b, t, c = u.shape
    tt = min(t, 256)
    hb = tt // CONV_HALO
    vec = lambda i, j: (0, 0)
    return pl.pallas_call(
        functools.partial(_conv_a_kernel, tt=tt),
        grid=(b, t // tt),
        in_specs=[pl.BlockSpec((None, tt, c), lambda i, j: (i, j, 0)),
                  pl.BlockSpec((None, CONV_HALO, c), lambda i, j: (i, jnp.maximum(j * hb - 1, 0), 0)),
                  pl.BlockSpec((None, CONV_HALO, c), lambda i, j: (i, 0, 0)),
                  pl.BlockSpec((CONV_K, c), vec),
                  pl.BlockSpec((1, c), vec), pl.BlockSpec((1, c), vec), pl.BlockSpec((1, c), vec)],
        out_specs=pl.BlockSpec((None, tt, c), lambda i, j: (i, j, 0)),
        out_shape=jax.ShapeDtypeStruct((b, t, c), BF16),
        scratch_shapes=[pltpu.VMEM((tt + CONV_HALO, c), F32),
                        pltpu.VMEM((SUBLANES - 1, tt + CONV_HALO - SUBLANES, c), F32),
                        pltpu.VMEM((tt, c), F32)],
        compiler_params=_params("parallel", "parallel"),
        name="conv_a",
    )(u, u, hist, w_dw, b_dw.reshape(1, c), ln_g.reshape(1, c), ln_b.reshape(1, c))


def _pool_kernel(z_ref, prev_ref, hist_ref, o_ref, ext_ref, *, tt, pos0):
    j = pl.program_id(1)
    ext_ref[0:POOL_HALO, :] = jnp.where(j == 0, hist_ref[...], prev_ref[...])
    ext_ref[POOL_HALO:, :] = z_ref[...]
    pos = pos0 + j * tt + lax.broadcasted_iota(jnp.int32, (tt, 1), 0)
    for g, w in enumerate(POOL_WINDOWS):
        c0 = g * POOL_GROUP
        cur = ext_ref[:, c0:c0 + POOL_GROUP]
        sh = 1
        while sh < w:
            cur = cur + pltpu.roll(cur, sh, 0)
            sh *= 2
        win = cur[POOL_HALO:, :]
        cnt = jnp.minimum(w, pos + 1).astype(F32)
        m = win / cnt - ext_ref[POOL_HALO:, c0:c0 + POOL_GROUP]
        o_ref[:, c0:c0 + POOL_GROUP] = m.astype(o_ref.dtype)


def _pool(zmid, hist, pos0):
    b, t, _ = zmid.shape
    c = D_POOL
    tt = min(t, 256)
    hb = tt // POOL_HALO
    return pl.pallas_call(
        functools.partial(_pool_kernel, tt=tt, pos0=pos0),
        grid=(b, t // tt),
        in_specs=[pl.BlockSpec((None, tt, c), lambda i, j: (i, j, 0)),
                  pl.BlockSpec((None, POOL_HALO, c), lambda i, j: (i, jnp.maximum(j * hb - 1, 0), 0)),
                  pl.BlockSpec((None, POOL_HALO, c), lambda i, j: (i, 0, 0))],
        out_specs=pl.BlockSpec((None, tt, c), lambda i, j: (i, j, 0)),
        out_shape=jax.ShapeDtypeStruct((b, t, c), BF16),
        scratch_shapes=[pltpu.VMEM((tt + POOL_HALO, c), F32)],
        compiler_params=_params("parallel", "parallel"),
        name="pool",
    )(zmid, zmid, hist)


def _pool_mm_kernel(a_ref, w_ref, s_ref, o_ref):
    z = jnp.dot(a_ref[...], w_ref[...], preferred_element_type=F32)
    o_ref[...] = (z * s_ref[...]).astype(o_ref.dtype)


def _pool_mm(m, w_pool, pool_scale):
    r = m.shape[0]
    tm = min(r, 1024)
    return pl.pallas_call(
        _pool_mm_kernel,
        grid=(r // tm, N_POOL_GROUPS),
        in_specs=[pl.BlockSpec((tm, POOL_GROUP), lambda i, g: (i, g)),
                  pl.BlockSpec((None, POOL_GROUP, POOL_OUT), lambda i, g: (g, 0, 0)),
                  pl.BlockSpec((1, POOL_OUT), lambda i, g: (0, g))],
        out_specs=pl.BlockSpec((tm, POOL_OUT), lambda i, g: (i, g)),
        out_shape=jax.ShapeDtypeStruct((r, D_MODEL), BF16),
        compiler_params=_params("parallel", "parallel"),
        name="pool_mm",
    )(m, w_pool, pool_scale.reshape(1, D_MODEL))


def _rot_half(x):
    lane = lax.broadcasted_iota(jnp.int32, x.shape, 1)
    first = (lane % ROPE_DIM) < (ROPE_DIM // 2)
    return jnp.where(first, pltpu.roll(x, LANES - ROPE_DIM // 2, 1), pltpu.roll(x, ROPE_DIM // 2, 1))


def _lat_kernel(z_ref, gq_ref, gkv_ref, cos_ref, sin_ref, ql_ref, ckv_ref, ckvb_ref, kr_ref, krb_ref):
    nb, tt, _ = z_ref.shape
    zq = z_ref[:, :, MID_Q:MID_KV]
    ql = zq * lax.rsqrt(jnp.mean(zq * zq, axis=-1, keepdims=True) + EPS) * gq_ref[...]
    ql_ref[...] = ql.astype(ql_ref.dtype)
    zkv = z_ref[:, :, MID_KV:MID_R]
    ckv = zkv * lax.rsqrt(jnp.mean(zkv * zkv, axis=-1, keepdims=True) + EPS) * gkv_ref[...]
    ckv_ref[...] = ckv
    ckvb_ref[...] = ckv.astype(BF16)
    zr = z_ref[:, :, MID_R:MID_R + LANES]
    rot = _rot_half(zr.reshape(nb * tt, LANES)).reshape(nb, tt, LANES)
    kr = zr * cos_ref[...] + rot * sin_ref[...]
    kr_ref[...] = kr[:, :, :ROPE_DIM]
    krb_ref[...] = kr.astype(BF16)


def _lat(zmid, g_q, g_kv, cos, sin):
    b, t, _ = zmid.shape
    nb, tt = _row_tiles(b, t)
    row = lambda i, j: (i, j, 0)
    vec = lambda i, j: (0, 0)
    shapes = (jax.ShapeDtypeStruct((b, t, Q_RANK), BF16),
              jax.ShapeDtypeStruct((b, t, KV_RANK), F32),
              jax.ShapeDtypeStruct((b, t, KV_RANK), BF16),
              jax.ShapeDtypeStruct((b, t, ROPE_DIM), F32),
              jax.ShapeDtypeStruct((b, t, LANES), BF16))
    return pl.pallas_call(
        _lat_kernel,
        grid=(b // nb, t // tt),
        in_specs=[pl.BlockSpec((nb, tt, MID_W), row),
                  pl.BlockSpec((1, Q_RANK), vec), pl.BlockSpec((1, KV_RANK), vec),
                  pl.BlockSpec((tt, LANES), lambda i, j: (j, 0)),
                  pl.BlockSpec((tt, LANES), lambda i, j: (j, 0))],
        out_specs=(pl.BlockSpec((nb, tt, Q_RANK), row), pl.BlockSpec((nb, tt, KV_RANK), row),
                   pl.BlockSpec((nb, tt, KV_RANK), row), pl.BlockSpec((nb, tt, ROPE_DIM), row),
                   pl.BlockSpec((nb, tt, LANES), row)),
        out_shape=shapes,
        compiler_params=_params("parallel", "parallel"),
        name="lat",
    )(zmid, g_q.reshape(1, Q_RANK), g_kv.reshape(1, KV_RANK), cos, sin)


def _qprep_kernel(q_ref, wuk_ref, cos_ref, sin_ref, qa_ref, qr_ref):
    nb, tt, _ = q_ref.shape
    rows = nb * tt
    nope = N_HEADS * NOPE_DIM
    for h in range(N_HEADS):
        qn = q_ref[:, :, h * NOPE_DIM:(h + 1) * NOPE_DIM].reshape(rows, NOPE_DIM).astype(BF16)
        qa = jnp.dot(qn, wuk_ref[h], preferred_element_type=F32) * ATTN_SCALE
        qa_ref[h] = qa.reshape(nb, tt, KV_RANK).astype(qa_ref.dtype)
    cos = cos_ref[...]
    sin = sin_ref[...]
    for c in range(N_HEADS * ROPE_DIM // LANES):
        x = q_ref[:, :, nope + c * LANES:nope + (c + 1) * LANES]
        rot = _rot_half(x.reshape(rows, LANES)).reshape(nb, tt, LANES)
        r = ((x * cos + rot * sin) * ATTN_SCALE).astype(qr_ref.dtype)
        qr_ref[2 * c] = r[:, :, :ROPE_DIM]
        qr_ref[2 * c + 1] = r[:, :, ROPE_DIM:]


def _qprep(q, w_ukt, cos, sin):
    b, t, qw = q.shape
    nb, tt = (1, 256) if t >= 256 else (b, t)
    return pl.pallas_call(
        _qprep_kernel,
        grid=(b // nb, t // tt),
        in_specs=[pl.BlockSpec((nb, tt, qw), lambda i, j: (i, j, 0)),
                  pl.BlockSpec((N_HEADS, NOPE_DIM, KV_RANK), lambda i, j: (0, 0, 0)),
                  pl.BlockSpec((tt, LANES), lambda i, j: (j, 0)),
                  pl.BlockSpec((tt, LANES), lambda i, j: (j, 0))],
        out_specs=(pl.BlockSpec((N_HEADS, nb, tt, KV_RANK), lambda i, j: (0, i, j, 0)),
                   pl.BlockSpec((N_HEADS, nb, tt, ROPE_DIM), lambda i, j: (0, i, j, 0))),
        out_shape=(jax.ShapeDtypeStruct((N_HEADS, b, t, KV_RANK), BF16),
                   jax.ShapeDtypeStruct((N_HEADS, b, t, ROPE_DIM), BF16)),
        compiler_params=_params("parallel", "parallel"),
        name="qprep",
    )(q, w_ukt, cos, sin)


def _qcat_kernel(q_ref, cos_ref, sin_ref, o_ref):
    tt = q_ref.shape[0]
    scale = ATTN_SCALE * LOG2E
    nope = N_HEADS * NOPE_DIM
    cos = cos_ref[...]
    sin = sin_ref[...]
    low = lax.broadcasted_iota(jnp.int32, (tt, LANES), 1) < ROPE_DIM
    for c in range(N_HEADS * ROPE_DIM // LANES):
        x = q_ref[:, nope + c * LANES:nope + (c + 1) * LANES]
        r = (x * cos + _rot_half(x) * sin) * scale
        o_ref[2 * c, :, NOPE_DIM:] = jnp.where(low, r, 0.0).astype(o_ref.dtype)
        o_ref[2 * c + 1, :, NOPE_DIM:] = jnp.where(low, pltpu.roll(r, ROPE_DIM, 1), 0.0).astype(o_ref.dtype)
    for h in range(N_HEADS):
        o_ref[h, :, :NOPE_DIM] = (q_ref[:, h * NOPE_DIM:(h + 1) * NOPE_DIM] * scale).astype(o_ref.dtype)


def _qcat(q, cos, sin):
    b, t, qw = q.shape
    tt = 256
    return pl.pallas_call(
        _qcat_kernel,
        grid=(b, t // tt),
        in_specs=[pl.BlockSpec((None, tt, qw), lambda i, j: (i, j, 0)),
                  pl.BlockSpec((tt, LANES), lambda i, j: (j, 0)),
                  pl.BlockSpec((tt, LANES), lambda i, j: (j, 0))],
        out_specs=pl.BlockSpec((None, N_HEADS, tt, QK_DIM), lambda i, j: (i, 0, j, 0)),
        out_shape=jax.ShapeDtypeStruct((b, N_HEADS, t, QK_DIM), BF16),
        compiler_params=_params("parallel", "parallel"),
        name="qcat",
    )(q, cos, sin)


def _kvcat_kernel(ckv_ref, kr_ref, wuk_ref, wuv_ref, k_ref, v_ref):
    c = ckv_ref[...]
    kn = jnp.dot(c, wuk_ref[...], preferred_element_type=F32)
    vv = jnp.dot(c, wuv_ref[...], preferred_element_type=F32)
    kr = kr_ref[...]
    one_col = (lax.broadcasted_iota(jnp.int32, kr.shape, 1) == 0).astype(v_ref.dtype)
    for h in range(N_HEADS):
        k_ref[h, :, :NOPE_DIM] = kn[:, h * NOPE_DIM:(h + 1) * NOPE_DIM].astype(k_ref.dtype)
        k_ref[h, :, NOPE_DIM:] = kr
        v_ref[h, :, :V_DIM] = vv[:, h * V_DIM:(h + 1) * V_DIM].astype(v_ref.dtype)
        v_ref[h, :, V_DIM:] = one_col


def _kvcat(ckv_b, krope_b, w_uk2, w_uv2):
    b, t, _ = ckv_b.shape
    tt = 512
    return pl.pallas_call(
        _kvcat_kernel,
        grid=(b, t // tt),
        in_specs=[pl.BlockSpec((None, tt, KV_RANK), lambda i, j: (i, j, 0)),
                  pl.BlockSpec((None, tt, LANES), lambda i, j: (i, j, 0)),
                  pl.BlockSpec((KV_RANK, N_HEADS * NOPE_DIM), lambda i, j: (0, 0)),
                  pl.BlockSpec((KV_RANK, N_HEADS * V_DIM), lambda i, j: (0, 0))],
        out_specs=(pl.BlockSpec((None, N_HEADS, tt, QK_DIM), lambda i, j: (i, 0, j, 0)),
                   pl.BlockSpec((None, N_HEADS, tt, V_AUG), lambda i, j: (i, 0, j, 0))),
        out_shape=(jax.ShapeDtypeStruct((b, N_HEADS, t, QK_DIM), BF16),
                   jax.ShapeDtypeStruct((b, N_HEADS, t, V_AUG), BF16)),
        compiler_params=_params("parallel", "parallel"),
        name="kvcat",
    )(ckv_b, krope_b, w_uk2, w_uv2)


def _mha_kernel(q_ref, k_ref, v_ref, o_ref, m_ref, acc_ref, *, tb, gh):
    i = pl.program_id(2)
    dn = (((1,), (1,)), ((), ()))
    m_ref[...] = jnp.full(m_ref.shape, NEG, F32)
    acc_ref[...] = jnp.zeros(acc_ref.shape, F32)

    def block(j, masked):
        k0 = pl.multiple_of(j * tb, tb)
        for g in range(gh):
            s = lax.dot_general(q_ref[g], k_ref[g, pl.ds(k0, tb), :], dn, preferred_element_type=F32)
            if masked:
                qc = lax.broadcasted_iota(jnp.int32, (tb, tb), 0) >> CHUNK_SHIFT
                kc = lax.broadcasted_iota(jnp.int32, (tb, tb), 1) >> CHUNK_SHIFT
                s = jnp.where(kc <= qc, s, NEG)
            m_old = m_ref[g]
            m_new = jnp.maximum(m_old, jnp.max(s, axis=-1, keepdims=True))
            alpha = jnp.exp2(m_old - m_new)
            p = jnp.exp2(s - jnp.tile(m_new, (1, tb // LANES)))
            pv = jnp.dot(p.astype(BF16), v_ref[g, pl.ds(k0, tb), :], preferred_element_type=F32)
            acc_ref[g] = jnp.tile(alpha, (1, V_AUG // LANES)) * acc_ref[g] + pv
            m_ref[g] = m_new

    def full_block(j, carry):
        block(j, False)
        return carry

    lax.fori_loop(0, i, full_block, 0)
    block(i, True)
    for g in range(gh):
        acc = acc_ref[g]
        o_ref[:, g * V_DIM:(g + 1) * V_DIM] = (acc[:, :V_DIM] / acc[:, V_DIM:V_DIM + 1]).astype(o_ref.dtype)


def _mha(qc, kc, vc):
    b, nh, t, _ = qc.shape
    tb = 512
    gh = 4
    return pl.pallas_call(
        functools.partial(_mha_kernel, tb=tb, gh=gh),
        grid=(b, nh // gh, t // tb),
        in_specs=[pl.BlockSpec((None, gh, tb, QK_DIM), lambda bi, hg, i: (bi, hg, i, 0)),
                  pl.BlockSpec((None, gh, t, QK_DIM), lambda bi, hg, i: (bi, hg, 0, 0)),
                  pl.BlockSpec((None, gh, t, V_AUG), lambda bi, hg, i: (bi, hg, 0, 0))],
        out_specs=pl.BlockSpec((None, tb, gh * V_DIM), lambda bi, hg, i: (bi, i, hg)),
        out_shape=jax.ShapeDtypeStruct((b, t, nh * V_DIM), BF16),
        scratch_shapes=[pltpu.VMEM((gh, tb, LANES), F32), pltpu.VMEM((gh, tb, V_AUG), F32)],
        compiler_params=_params("parallel", "parallel", "parallel"),
        name="mha",
    )(qc, kc, vc)


def _last_kv_block(i, *, tq, tk, pos0, nk):
    last_pos = (pos0 + (i + 1) * tq - 1) // CHUNK * CHUNK + CHUNK - 1
    return jnp.minimum(last_pos // tk, nk - 1)


def _attn_kernel(qa_ref, qr_ref, k_ref, kr_ref, wuv_ref, o_ref, m_ref, l_ref, acc_ref,
                 *, tq, tk, pos0, s_valid, nk):
    i = pl.program_id(1)
    kk = pl.program_id(2)
    rows = N_HEADS * tq

    @pl.when(kk == 0)
    def _():
        m_ref[...] = jnp.full(m_ref.shape, NEG, F32)
        l_ref[...] = jnp.zeros(l_ref.shape, F32)
        acc_ref[...] = jnp.zeros(acc_ref.shape, F32)

    @pl.when(kk <= _last_kv_block(i, tq=tq, tk=tk, pos0=pos0, nk=nk))
    def _():
        qa = qa_ref[...].reshape(rows, KV_RANK)
        qr = qr_ref[...].reshape(rows, ROPE_DIM)
        k = k_ref[...]
        dn = (((1,), (1,)), ((), ()))
        s = lax.dot_general(qa, k, dn, preferred_element_type=F32)
        s = s + lax.dot_general(qr, kr_ref[...], dn, preferred_element_type=F32)
        qpos = pos0 + i * tq + lax.broadcasted_iota(jnp.int32, (tq, tk), 0)
        kpos = kk * tk + lax.broadcasted_iota(jnp.int32, (tq, tk), 1)
        ok = jnp.logical_and((kpos >> CHUNK_SHIFT) <= (qpos >> CHUNK_SHIFT), kpos < s_valid)
        s = jnp.where(ok[None], s.reshape(N_HEADS, tq, tk), NEG).reshape(rows, tk)
        m_old = m_ref[...]
        m_new = jnp.maximum(m_old, jnp.max(s, axis=-1, keepdims=True))
        alpha = jnp.exp(m_old - m_new)
        p = jnp.exp(s - m_new)
        l_ref[...] = alpha * l_ref[...] + jnp.sum(p, axis=-1, keepdims=True)
        acc_ref[...] = alpha * acc_ref[...] + jnp.dot(p.astype(BF16), k, preferred_element_type=F32)
        m_ref[...] = m_new

    @pl.when(kk == nk - 1)
    def _():
        o_lat = (acc_ref[...] / l_ref[...]).astype(BF16).reshape(N_HEADS, tq, KV_RANK)
        for h in range(N_HEADS):
            o_h = jnp.dot(o_lat[h], wuv_ref[h], preferred_element_type=F32)
            o_ref[:, h * V_DIM:(h + 1) * V_DIM] = o_h.astype(o_ref.dtype)


def _attn(q_abs, q_rope, ckv_all, krope_all, w_uv, *, pos0, s_valid):
    _, b, t, _ = q_abs.shape
    s_len = ckv_all.shape[1]
    tq = min(t, 128)
    tk = next(c for c in (1536, 1024, 512) if s_len % c == 0) if t < 128 else 512
    nk = s_len // tk
    last = functools.partial(_last_kv_block, tq=tq, tk=tk, pos0=pos0, nk=nk)
    kv_idx = lambda bi, i, kk: (bi, jnp.minimum(kk, last(i)), 0)
    return pl.pallas_call(
        functools.partial(_attn_kernel, tq=tq, tk=tk, pos0=pos0, s_valid=s_valid, nk=nk),
        grid=(b, t // tq, nk),
        in_specs=[pl.BlockSpec((N_HEADS, None, tq, KV_RANK), lambda bi, i, kk: (0, bi, i, 0)),
                  pl.BlockSpec((N_HEADS, None, tq, ROPE_DIM), lambda bi, i, kk: (0, bi, i, 0)),
                  pl.BlockSpec((None, tk, KV_RANK), kv_idx),
                  pl.BlockSpec((None, tk, ROPE_DIM), kv_idx),
                  pl.BlockSpec((N_HEADS, KV_RANK, V_DIM), lambda bi, i, kk: (0, 0, 0))],
        out_specs=pl.BlockSpec((None, tq, N_HEADS * V_DIM), lambda bi, i, kk: (bi, i, 0)),
        out_shape=jax.ShapeDtypeStruct((b, t, N_HEADS * V_DIM), BF16),
        scratch_shapes=[pltpu.VMEM((N_HEADS * tq, 1), F32), pltpu.VMEM((N_HEADS * tq, 1), F32),
                        pltpu.VMEM((N_HEADS * tq, KV_RANK), F32)],
        compiler_params=_params("parallel", "parallel", "arbitrary"),
        name="attn",
    )(q_abs, q_rope, ckv_all, krope_all, w_uv)


def _mm_res_kernel(*refs, merge):
    if merge:
        g0, g1, g2, oa, ob, oc, w_ref, x_ref, gate_ref, gn_ref, o_ref, acc_ref = refs
    else:
        a_ref, w_ref, x_ref, gate_ref, gn_ref, o_ref, acc_ref = refs
    kk = pl.program_id(2)
    nb, tt, d = x_ref.shape

    @pl.when(kk == 0)
    def _():
        acc_ref[...] = jnp.zeros(acc_ref.shape, F32)

    if merge:
        a = (g0[...].astype(F32) * oa[...].astype(F32) + g1[...].astype(F32) * ob[...].astype(F32)
             + g2[...].astype(F32) * oc[...].astype(F32)).astype(BF16)
    else:
        a = a_ref[...]
    a = a.reshape(nb * tt, a.shape[-1])
    acc_ref[...] += jnp.dot(a, w_ref[...], preferred_element_type=F32)

    @pl.when(kk == pl.num_programs(2) - 1)
    def _():
        y = acc_ref[...].reshape(nb, tt, d)
        yn = y * lax.rsqrt(jnp.mean(y * y, axis=-1, keepdims=True) + EPS) * gn_ref[...]
        o_ref[...] = x_ref[...] + gate_ref[...] * yn


def _mm_res(acts, w, x, gate, g_norm, *, merge, name):
    b, t, d = x.shape
    k = w.shape[0]
    nb, tt = _row_tiles(b, t)
    tk = 512 if merge else k // 4
    nkb = k // tk
    row = lambda i, j, kk: (i, j, kk)
    if merge:
        gates, oa, ob, oc = acts
        ins = [gates, gates, gates, oa, ob, oc]
        branch = lambda o, i, j, kk: (i, j, o * nkb + kk)
        specs = [pl.BlockSpec((nb, tt, tk), functools.partial(branch, o)) for o in range(3)]
        specs += [pl.BlockSpec((nb, tt, tk), row)] * 3
    else:
        ins = [acts]
        specs = [pl.BlockSpec((nb, tt, tk), row)]
    full = lambda i, j, kk: (i, j, 0)
    return pl.pallas_call(
        functools.partial(_mm_res_kernel, merge=merge),
        grid=(b // nb, t // tt, nkb),
        in_specs=specs + [pl.BlockSpec((tk, d), lambda i, j, kk: (kk, 0)),
                          pl.BlockSpec((nb, tt, d), full),
                          pl.BlockSpec((nb, 1, d), lambda i, j, kk: (i, 0, 0)),
                          pl.BlockSpec((1, d), lambda i, j, kk: (0, 0))],
        out_specs=pl.BlockSpec((nb, tt, d), full),
        out_shape=jax.ShapeDtypeStruct((b, t, d), F32),
        scratch_shapes=[pltpu.VMEM((nb * tt, d), F32)],
        compiler_params=_params("parallel", "parallel", "arbitrary"),
        name=name,
    )(*ins, w, x, gate, g_norm.reshape(1, d))


def _ffn_act_kernel(ug_ref, uv_ref, prev_ref, hist_ref, w_ref, b_ref, o_ref, ext_ref, *, tt):
    j = pl.program_id(1)
    ext_ref[0:FFN_HALO, :] = jnp.where(j == 0, hist_ref[...], prev_ref[...].astype(F32))
    ext_ref[FFN_HALO:, :] = ug_ref[...].astype(F32)
    lead = FFN_HALO - (FFN_K - 1)
    conv = b_ref[...]
    for k in range(FFN_K):
        conv = conv + w_ref[k:k + 1, :] * ext_ref[lead + k:lead + k + tt, :]
    o_ref[...] = (_silu(conv) * uv_ref[...].astype(F32)).astype(o_ref.dtype)


def _ffn_act(up, hist, w_dw, b_dw):
    b, t, _ = up.shape
    tt = min(t, 512)
    tc = 512 if t >= 512 else D_FF
    ncb = D_FF // tc
    hb = tt // FFN_HALO
    return pl.pallas_call(
        functools.partial(_ffn_act_kernel, tt=tt),
        grid=(b, t // tt, ncb),
        in_specs=[pl.BlockSpec((None, tt, tc), lambda i, j, c: (i, j, c)),
                  pl.BlockSpec((None, tt, tc), lambda i, j, c: (i, j, ncb + c)),
                  pl.BlockSpec((None, FFN_HALO, tc), lambda i, j, c: (i, jnp.maximum(j * hb - 1, 0), c)),
                  pl.BlockSpec((None, FFN_HALO, tc), lambda i, j, c: (i, 0, c)),
                  pl.BlockSpec((FFN_K, tc), lambda i, j, c: (0, c)),
                  pl.BlockSpec((1, tc), lambda i, j, c: (0, c))],
        out_specs=pl.BlockSpec((None, tt, tc), lambda i, j, c: (i, j, c)),
        out_shape=jax.ShapeDtypeStruct((b, t, D_FF), BF16),
        scratch_shapes=[pltpu.VMEM((tt + FFN_HALO, tc), F32)],
        compiler_params=_params("parallel", "parallel", "parallel"),
        name="ffn_act",
    )(up, up, up, hist, w_dw, b_dw.reshape(1, D_FF))


def _rope_tables(pos0, t):
    half = ROPE_DIM // 2
    inv = ROPE_THETA ** (-jnp.arange(half, dtype=F32) / half)
    pos = (pos0 + jnp.arange(t, dtype=jnp.int32)).astype(F32)
    ang = pos[:, None] * inv[None, :]
    cos, sin = jnp.cos(ang), jnp.sin(ang)
    return (jnp.concatenate([cos, cos, cos, cos], axis=-1),
            jnp.concatenate([-sin, sin, -sin, sin], axis=-1))


def _front_pad(a, rows):
    return jnp.pad(a, ((0, 0), (rows - a.shape[1], 0), (0, 0)))


def _tail(hist, new, n):
    keep = min(new.shape[1], n)
    return jnp.concatenate([hist, new[:, new.shape[1] - keep:].astype(F32)], axis=1)[:, -n:]


def _layer(x, mod, past_ckv, past_krope, hist_conv, hist_pool, hist_ffn, pos0, p):
    b, t, d = x.shape
    r = b * t
    shift1, scale1, gate1, shift2, scale2, gate2 = (mod[:, i] for i in range(6))
    cos, sin = _rope_tables(pos0, t)

    h = _norm_mod(x, p["g_pre_mix"], scale1, shift1).reshape(r, d)
    u_a = _mm(h, p["w_a1"], F32, tn=512, epi="glu", w2=p["w_a2"], name="mm_glu").reshape(b, t, D_CONV)
    zmid = _mm(h, p["w_mid"], F32, tn=MID_TN, name="mm_mid").reshape(b, t, MID_W)
    gates = _mm(h, p["w_g"], BF16, tn=512, epi="sigmoid", name="mm_gates").reshape(b, t, 3 * d)

    a_act = _conv_a(u_a, _front_pad(hist_conv, CONV_HALO), p["w_dwa"], p["b_dwa"], p["ln_a_g"], p["ln_a_b"])
    out_a = _mm(a_act.reshape(r, D_CONV), p["w_pa"], BF16, tn=512, name="mm_pa").reshape(b, t, d)

    m = _pool(zmid, _front_pad(hist_pool, POOL_HALO), pos0)
    out_b = _pool_mm(m.reshape(r, D_POOL), p["w_pool"], p["pool_scale"]).reshape(b, t, d)

    q_lat, ckv, ckv_b, krope, krope_b = _lat(zmid, p["g_q_lat"], p["g_kv_lat"], cos, sin)
    q = _mm(q_lat.reshape(r, Q_RANK), p["w_uq"], F32, tn=512, name="mm_uq").reshape(b, t, -1)
    if past_ckv.shape[1] == 0 and t % 512 == 0:
        kc, vc = _kvcat(ckv_b, krope_b, p["w_uk2"], p["w_uv2"])
        o = _mha(_qcat(q, cos, sin), kc, vc)
    else:
        q_abs, q_rope = _qprep(q, p["w_ukt"], cos, sin)
        s_valid = past_ckv.shape[1] + t
        s_pad = -(-s_valid // 512) * 512
        kv_pad = ((0, 0), (0, s_pad - s_valid), (0, 0))
        ckv_all = jnp.pad(jnp.concatenate([past_ckv.astype(BF16), ckv_b], axis=1), kv_pad)
        krope_all = jnp.pad(jnp.concatenate([past_krope.astype(BF16), krope_b[:, :, :ROPE_DIM]], axis=1), kv_pad)
        o = _attn(q_abs, q_rope, ckv_all, krope_all, p["w_uv"], pos0=pos0, s_valid=s_valid)
    out_c = _mm(o.reshape(r, d), p["w_oc"], BF16, tn=512, name="mm_oc").reshape(b, t, d)

    x = _mm_res((gates, out_a, out_b, out_c), p["w_out"], x, gate1, p["g_post_mix"], merge=True, name="mm_out")

    h2 = _norm_mod(x, p["g_pre_ffn"], scale2, shift2).reshape(r, d)
    up = _mm(h2, p["w_up"], BF16, tn=512, name="mm_up").reshape(b, t, 2 * D_FF)
    act = _ffn_act(up, _front_pad(hist_ffn, FFN_HALO), p["w_dwf"], p["b_dwf"])
    x = _mm_res(act, p["w_down"], x, gate2, p["g_post_ffn"], merge=False, name="mm_down")

    state = (ckv, krope,
             _tail(hist_conv, u_a, CONV_K - 1),
             _tail(hist_pool, zmid[:, :, :D_POOL], POOL_MAX - 1),
             _tail(hist_ffn, up[:, :, :D_FF], FFN_K - 1))
    return x, state


def _prep_weights(l, w):
    w_in = w["w_in"][l]
    w_mid = jnp.pad(w_in[:, OFF_B:OFF_G], ((0, 0), (0, MID_W - (OFF_G - OFF_B))))
    w_uq = w["w_uq"][l].reshape(Q_RANK, N_HEADS, NOPE_DIM + ROPE_DIM)
    w_uq = jnp.concatenate([w_uq[:, :, :NOPE_DIM].reshape(Q_RANK, -1), w_uq[:, :, NOPE_DIM:].reshape(Q_RANK, -1)], axis=1)
    p = dict(
        w_a1=w_in[:, :D_CONV].astype(BF16), w_a2=w_in[:, D_CONV:OFF_B].astype(BF16),
        w_mid=w_mid.astype(BF16), w_g=w_in[:, OFF_G:].astype(BF16),
        w_pa=w["w_pa"][l].astype(BF16), w_pool=w["w_pool"][l].astype(BF16),
        w_uq=w_uq.astype(BF16),
        w_ukt=jnp.transpose(w["w_uk"][l], (1, 2, 0)).astype(BF16),
        w_uv=jnp.transpose(w["w_uv"][l], (1, 0, 2)).astype(BF16),
        w_uk2=w["w_uk"][l].reshape(KV_RANK, N_HEADS * NOPE_DIM).astype(BF16),
        w_uv2=w["w_uv"][l].reshape(KV_RANK, N_HEADS * V_DIM).astype(BF16),
        w_oc=w["w_oc"][l].astype(BF16), w_out=w["w_out"][l].astype(BF16),
        w_up=w["w_up"][l].astype(BF16), w_down=w["w_down"][l].astype(BF16),
    )
    for name in ("g_pre_mix", "g_post_mix", "w_dwa", "b_dwa", "ln_a_g", "ln_a_b", "pool_scale", "g_q_lat",
                 "g_kv_lat", "g_pre_ffn", "g_post_ffn", "w_dwf", "b_dwf"):
        p[name] = w[name][l]
    return p


def kernel(x_prompt, x_sample, cache_ckv, cache_krope, state_conv, state_pool, state_ffn, c_prompt, c_sample, w_mod, b_mod, g_pre_mix, g_post_mix, w_in, w_dwa, b_dwa, ln_a_g, ln_a_b, w_pa, w_pool, pool_scale, g_q_lat, g_kv_lat, w_uq, w_uk, w_uv, w_oc, w_out, g_pre_ffn, g_post_ffn, w_up, w_dwf, b_dwf, w_down):
    weights = dict(g_pre_mix=g_pre_mix, g_post_mix=g_post_mix, w_in=w_in, w_dwa=w_dwa, b_dwa=b_dwa,
                   ln_a_g=ln_a_g, ln_a_b=ln_a_b, w_pa=w_pa, w_pool=w_pool, pool_scale=pool_scale,
                   g_q_lat=g_q_lat, g_kv_lat=g_kv_lat, w_uq=w_uq, w_uk=w_uk, w_uv=w_uv, w_oc=w_oc,
                   w_out=w_out, g_pre_ffn=g_pre_ffn, g_post_ffn=g_post_ffn, w_up=w_up, w_dwf=w_dwf,
                   b_dwf=b_dwf, w_down=w_down)
    depth = w_mod.shape[0]
    bp, bs = x_prompt.shape[0], x_sample.shape[0]
    past_len = cache_ckv.shape[2]
    d = x_prompt.shape[-1]

    rows = -(-(bp + bs) // SUBLANES) * SUBLANES
    c_all = jnp.pad(jnp.concatenate([c_prompt, c_sample], axis=0), ((0, rows - bp - bs), (0, 0)))
    mod_all = _mod(c_all, w_mod, b_mod)

    xp, xs = x_prompt, x_sample
    st_p = [[] for _ in range(5)]
    st_s = [[] for _ in range(5)]
    for l in range(depth):
        p = _prep_weights(l, weights)
        mod_p = mod_all[l, :bp].reshape(bp, 6, 1, d)
        mod_s = mod_all[l, bp:bp + bs].reshape(bs, 6, 1, d)
        xp, sp = _layer(xp, mod_p,
                        jnp.zeros((bp, 0, KV_RANK), F32), jnp.zeros((bp, 0, ROPE_DIM), F32),
                        jnp.zeros((bp, CONV_K - 1, D_CONV), F32), jnp.zeros((bp, POOL_MAX - 1, D_POOL), F32),
                        jnp.zeros((bp, FFN_K - 1, D_FF), F32), 0, p)
        xs, ss = _layer(xs, mod_s, cache_ckv[l], cache_krope[l], state_conv[l], state_pool[l], state_ffn[l],
                        past_len, p)
        for i in range(5):
            st_p[i].append(sp[i])
            st_s[i].append(ss[i])
    return (xp, xs) + tuple(jnp.stack(s) for s in st_p) + tuple(jnp.stack(s) for s in st_s)
```

```python
import functools

import jax
import jax.numpy as jnp
from jax import lax
from jax.experimental import pallas as pl
from jax.experimental.pallas import tpu as pltpu

F32 = jnp.float32
BF16 = jnp.bfloat16

D_MODEL = 2048
CHUNK = 64
CHUNK_SHIFT = 6
assert 1 << CHUNK_SHIFT == CHUNK
D_CONV = D_MODEL // 2
CONV_K = 31
D_POOL = D_MODEL // 2
POOL_WINDOWS = (2, 4, 8, 16)
POOL_MAX = 16
N_POOL_GROUPS = 4
POOL_GROUP = D_POOL // N_POOL_GROUPS
POOL_OUT = D_MODEL // N_POOL_GROUPS
N_HEADS = D_MODEL // 128
NOPE_DIM = 128
ROPE_DIM = 64
V_DIM = 128
Q_RANK = D_MODEL // 4
KV_RANK = D_MODEL // 4
ROPE_THETA = 10000.0
ATTN_SCALE = (NOPE_DIM + ROPE_DIM) ** -0.5
LOG2E = 1.4426950408889634
QK_DIM = 256
V_AUG = 256
D_FF = 256 * ((8 * D_MODEL // 3 + 255) // 256)
FFN_K = 3
EPS = 1e-6
NEG = -1e30
OFF_B = 2 * D_CONV
OFF_R = OFF_B + D_POOL + Q_RANK + KV_RANK
OFF_G = OFF_R + ROPE_DIM

LANES = 128
SUBLANES = 8
VMEM_LIMIT_BYTES = 56 * 1024 * 1024

MID_Q = D_POOL
MID_KV = MID_Q + Q_RANK
MID_R = MID_KV + KV_RANK
MID_W = MID_R + 2 * LANES
MID_TN = 256
assert OFF_B % MID_TN == 0 and MID_W % MID_TN == 0
CONV_HALO = 32
POOL_HALO = 16
FFN_HALO = 8
FFN_SUB = 512


def _params(*sem):
    return pltpu.CompilerParams(dimension_semantics=sem, vmem_limit_bytes=VMEM_LIMIT_BYTES)


def _sigmoid(x):
    return 0.5 + 0.5 * jnp.tanh(0.5 * x)


def _silu(x):
    h = 0.5 * x
    return h + h * jnp.tanh(h)


def _row_tiles(b, t):
    if t >= 512:
        return 1, 512
    return b, t


def _mod_kernel(c_ref, w_ref, b_ref, o_ref):
    c = c_ref[...]
    a = _silu(c).astype(BF16)
    o_ref[...] = jnp.dot(a, w_ref[...].astype(BF16), preferred_element_type=F32) + b_ref[...]


def _mod(c_all, w_mod, b_mod):
    nl, d, n = w_mod.shape
    bp = c_all.shape[0]
    tn = 1024
    return pl.pallas_call(
        _mod_kernel,
        grid=(nl, n // tn),
        in_specs=[pl.BlockSpec((bp, d), lambda l, j: (0, 0)),
                  pl.BlockSpec((None, d, tn), lambda l, j: (l, 0, j)),
                  pl.BlockSpec((None, 1, tn), lambda l, j: (l, 0, j))],
        out_specs=pl.BlockSpec((None, bp, tn), lambda l, j: (l, 0, j)),
        out_shape=jax.ShapeDtypeStruct((nl, bp, n), F32),
        compiler_params=_params("parallel", "parallel"),
        name="mod",
    )(c_all, w_mod, b_mod.reshape(nl, 1, n))


def _norm_mod_kernel(x_ref, g_ref, sc_ref, sh_ref, o_ref):
    x = x_ref[...]
    y = x * lax.rsqrt(jnp.mean(x * x, axis=-1, keepdims=True) + EPS) * g_ref[...]
    o_ref[...] = (y * (1.0 + sc_ref[...]) + sh_ref[...]).astype(o_ref.dtype)


def _norm_mod(x, g, scale, shift):
    b, t, d = x.shape
    nb, tt = _row_tiles(b, t)
    return pl.pallas_call(
        _norm_mod_kernel,
        grid=(b // nb, t // tt),
        in_specs=[pl.BlockSpec((nb, tt, d), lambda i, j: (i, j, 0)),
                  pl.BlockSpec((1, d), lambda i, j: (0, 0)),
                  pl.BlockSpec((nb, 1, d), lambda i, j: (i, 0, 0)),
                  pl.BlockSpec((nb, 1, d), lambda i, j: (i, 0, 0))],
        out_specs=pl.BlockSpec((nb, tt, d), lambda i, j: (i, j, 0)),
        out_shape=jax.ShapeDtypeStruct((b, t, d), BF16),
        compiler_params=_params("parallel", "parallel"),
        name="norm_mod",
    )(x, g.reshape(1, d), scale, shift)


def _mm_kernel(*refs, epi, n_w, n_prompt_tiles):
    ap_ref, as_ref = refs[0], refs[1]
    w_refs = refs[2:2 + n_w]
    op_ref, os_ref = refs[2 + n_w], refs[3 + n_w]
    wb_refs = refs[4 + n_w:]
    m = pl.program_id(1)

    @pl.when(m == 0)
    def _():
        for w_ref, wb_ref in zip(w_refs, wb_refs):
            wb_ref[...] = w_ref[...].astype(BF16)

    def compute(a):
        z = jnp.dot(a, wb_refs[0][...], preferred_element_type=F32)
        if epi == "sigmoid":
            z = _sigmoid(z)
        elif epi == "glu":
            z = z * _sigmoid(jnp.dot(a, wb_refs[1][...], preferred_element_type=F32))
        return z

    @pl.when(m < n_prompt_tiles)
    def _():
        op_ref[...] = compute(ap_ref[...]).astype(op_ref.dtype)

    @pl.when(m == n_prompt_tiles)
    def _():
        os_ref[...] = compute(as_ref[...]).astype(os_ref.dtype)


def _mm(a_p, a_s, ws, n, out_dtype, *, tn, epi="none", name="mm"):
    rp, k = a_p.shape
    rs = a_s.shape[0]
    tm = min(rp, 1024)
    npt = rp // tm

    def w_spec(arr, lead, first):
        if lead is None:
            return pl.BlockSpec((k, tn), lambda j, m: (0, first + j))
        return pl.BlockSpec((None, k, tn), lambda j, m: (lead, 0, first + j))

    p_row = lambda j, m: (jnp.minimum(m, npt - 1), 0)
    p_out = lambda j, m: (jnp.minimum(m, npt - 1), j)
    return pl.pallas_call(
        functools.partial(_mm_kernel, epi=epi, n_w=len(ws), n_prompt_tiles=npt),
        grid=(n // tn, npt + 1),
        in_specs=[pl.BlockSpec((tm, k), p_row), pl.BlockSpec((rs, k), lambda j, m: (0, 0))]
        + [w_spec(*w) for w in ws],
        out_specs=(pl.BlockSpec((tm, tn), p_out), pl.BlockSpec((rs, tn), lambda j, m: (0, j))),
        out_shape=(jax.ShapeDtypeStruct((rp, n), out_dtype), jax.ShapeDtypeStruct((rs, n), out_dtype)),
        scratch_shapes=[pltpu.VMEM((k, tn), BF16) for _ in ws],
        compiler_params=_params("parallel", "arbitrary"),
        name=name,
    )(a_p, a_s, *[w[0] for w in ws])


def _conv_a_kernel(u_ref, prev_ref, hist_ref, w_ref, b_ref, g_ref, be_ref, o_ref, ext_ref, sh_ref, a_ref, *, tt):
    j = pl.program_id(1)
    ext_ref[0:CONV_HALO, :] = jnp.where(j == 0, hist_ref[...], prev_ref[...])
    ext_ref[CONV_HALO:, :] = u_ref[...]
    sh_rows = sh_ref.shape[1]
    for s in range(1, SUBLANES):
        sh_ref[s - 1] = ext_ref[s:s + sh_rows, :]
    rc = 32
    cc = 512
    lead = CONV_HALO - (CONV_K - 1)

    def body(r, carry):
        r0 = pl.multiple_of(r * rc, rc)
        for c0 in range(0, D_CONV, cc):
            acc = jnp.zeros((rc, cc), F32)
            for k in range(CONV_K):
                s = (lead + k) % SUBLANES
                row = pl.multiple_of(r0 + (lead + k - s), SUBLANES)
                if s == 0:
                    x = ext_ref[pl.ds(row, rc), c0:c0 + cc]
                else:
                    x = sh_ref[s - 1, pl.ds(row, rc), c0:c0 + cc]
                acc = acc + w_ref[k:k + 1, c0:c0 + cc] * x
            a_ref[pl.ds(r0, rc), c0:c0 + cc] = acc
        return carry

    lax.fori_loop(0, tt // rc, body, 0)
    a = a_ref[...] + b_ref[...]
    mu = jnp.mean(a, axis=-1, keepdims=True)
    ac = a - mu
    var = jnp.mean(ac * ac, axis=-1, keepdims=True)
    y = ac * lax.rsqrt(var + EPS) * g_ref[...] + be_ref[...]
    o_ref[...] = _silu(y).astype(o_ref.dtype)


def _conv_a(u, hist, w_dw, b_dw, ln_g, ln_b):
    b, t, c = u.shape
    tt = min(t, 256)
    hb = tt // CONV_HALO
    vec = lambda i, j: (0, 0)
    return pl.pallas_call(
        functools.partial(_conv_a_kernel, tt=tt),
        grid=(b, t // tt),
        in_specs=[pl.BlockSpec((None, tt, c), lambda i, j: (i, j, 0)),
                  pl.BlockSpec((None, CONV_HALO, c), lambda i, j: (i, jnp.maximum(j * hb - 1, 0), 0)),
                  pl.BlockSpec((None, CONV_HALO, c), lambda i, j: (i, 0, 0)),
                  pl.BlockSpec((CONV_K, c), vec),
                  pl.BlockSpec((1, c), vec), pl.BlockSpec((1, c), vec), pl.BlockSpec((1, c), vec)],
        out_specs=pl.BlockSpec((None, tt, c), lambda i, j: (i, j, 0)),
        out_shape=jax.ShapeDtypeStruct((b, t, c), BF16),
        scratch_shapes=[pltpu.VMEM((tt + CONV_HALO, c), F32),
                        pltpu.VMEM((SUBLANES - 1, tt + CONV_HALO - SUBLANES, c), F32),
                        pltpu.VMEM((tt, c), F32)],
        compiler_params=_params("parallel", "parallel"),
        name="conv_a",
    )(u, u, hist, w_dw, b_dw.reshape(1, c), ln_g.reshape(1, c), ln_b.reshape(1, c))


def _pool_kernel(z_ref, prev_ref, hist_ref, o_ref, ext_ref, *, tt, pos0):
    j = pl.program_id(1)
    ext_ref[0:POOL_HALO, :] = jnp.where(j == 0, hist_ref[...], prev_ref[...])
    ext_ref[POOL_HALO:, :] = z_ref[...]
    pos = pos0 + j * tt + lax.broadcasted_iota(jnp.int32, (tt, 1), 0)
    for g, w in enumerate(POOL_WINDOWS):
        c0 = g * POOL_GROUP
        cur = ext_ref[:, c0:c0 + POOL_GROUP]
        sh = 1
        while sh < w:
            cur = cur + pltpu.roll(cur, sh, 0)
            sh *= 2
        win = cur[POOL_HALO:, :]
        cnt = jnp.minimum(w, pos + 1).astype(F32)
        m = win / cnt - ext_ref[POOL_HALO:, c0:c0 + POOL_GROUP]
        o_ref[:, c0:c0 + POOL_GROUP] = m.astype(o_ref.dtype)


def _pool(zmid, hist, pos0):
    b, t, _ = zmid.shape
    c = D_POOL
    tt = min(t, 256)
    hb = tt // POOL_HALO
    return pl.pallas_call(
        functools.partial(_pool_kernel, tt=tt, pos0=pos0),
        grid=(b, t // tt),
        in_specs=[pl.BlockSpec((None, tt, c), lambda i, j: (i, j, 0)),
                  pl.BlockSpec((None, POOL_HALO, c), lambda i, j: (i, jnp.maximum(j * hb - 1, 0), 0)),
                  pl.BlockSpec((None, POOL_HALO, c), lambda i, j: (i, 0, 0))],
        out_specs=pl.BlockSpec((None, tt, c), lambda i, j: (i, j, 0)),
        out_shape=jax.ShapeDtypeStruct((b, t, c), BF16),
        scratch_shapes=[pltpu.VMEM((tt + POOL_HALO, c), F32)],
        compiler_params=_params("parallel", "parallel"),
        name="pool",
    )(zmid, zmid, hist)


def _pool_mm_kernel(a_ref, w_ref, s_ref, o_ref):
    z = jnp.dot(a_ref[...], w_ref[...], preferred_element_type=F32)
    o_ref[...] = (z * s_ref[...]).astype(o_ref.dtype)


def _pool_mm(m, w_pool, pool_scale):
    r = m.shape[0]
    tm = min(r, 1024)
    return pl.pallas_call(
        _pool_mm_kernel,
        grid=(r // tm, N_POOL_GROUPS),
        in_specs=[pl.BlockSpec((tm, POOL_GROUP), lambda i, g: (i, g)),
                  pl.BlockSpec((None, POOL_GROUP, POOL_OUT), lambda i, g: (g, 0, 0)),
                  pl.BlockSpec((1, POOL_OUT), lambda i, g: (0, g))],
        out_specs=pl.BlockSpec((tm, POOL_OUT), lambda i, g: (i, g)),
        out_shape=jax.ShapeDtypeStruct((r, D_MODEL), BF16),
        compiler_params=_params("parallel", "parallel"),
        name="pool_mm",
    )(m, w_pool, pool_scale.reshape(1, D_MODEL))


def _rot_half(x):
    lane = lax.broadcasted_iota(jnp.int32, x.shape, 1)
    first = (lane % ROPE_DIM) < (ROPE_DIM // 2)
    return jnp.where(first, pltpu.roll(x, LANES - ROPE_DIM // 2, 1), pltpu.roll(x, ROPE_DIM // 2, 1))


def _lat_kernel(z_ref, gq_ref, gkv_ref, cos_ref, sin_ref, ql_ref, ckv_ref, ckvb_ref, kr_ref, krb_ref):
    nb, tt, _ = z_ref.shape
    zq = z_ref[:, :, MID_Q:MID_KV]
    ql = zq * lax.rsqrt(jnp.mean(zq * zq, axis=-1, keepdims=True) + EPS) * gq_ref[...]
    ql_ref[...] = ql.astype(ql_ref.dtype)
    zkv = z_ref[:, :, MID_KV:MID_R]
    ckv = zkv * lax.rsqrt(jnp.mean(zkv * zkv, axis=-1, keepdims=True) + EPS) * gkv_ref[...]
    ckv_ref[...] = ckv
    ckvb_ref[...] = ckv.astype(BF16)
    zr = z_ref[:, :, MID_R:MID_R + LANES]
    rot = _rot_half(zr.reshape(nb * tt, LANES)).reshape(nb, tt, LANES)
    kr = zr * cos_ref[...] + rot * sin_ref[...]
    kr_ref[...] = kr[:, :, :ROPE_DIM]
    krb_ref[...] = kr.astype(BF16)


def _lat(zmid, g_q, g_kv, cos, sin):
    b, t, _ = zmid.shape
    nb, tt = _row_tiles(b, t)
    row = lambda i, j: (i, j, 0)
    vec = lambda i, j: (0, 0)
    shapes = (jax.ShapeDtypeStruct((b, t, Q_RANK), BF16),
              jax.ShapeDtypeStruct((b, t, KV_RANK), F32),
              jax.ShapeDtypeStruct((b, t, KV_RANK), BF16),
              jax.ShapeDtypeStruct((b, t, ROPE_DIM), F32),
              jax.ShapeDtypeStruct((b, t, LANES), BF16))
    return pl.pallas_call(
        _lat_kernel,
        grid=(b // nb, t // tt),
        in_specs=[pl.BlockSpec((nb, tt, MID_W), row),
                  pl.BlockSpec((1, Q_RANK), vec), pl.BlockSpec((1, KV_RANK), vec),
                  pl.BlockSpec((tt, LANES), lambda i, j: (j, 0)),
                  pl.BlockSpec((tt, LANES), lambda i, j: (j, 0))],
        out_specs=(pl.BlockSpec((nb, tt, Q_RANK), row), pl.BlockSpec((nb, tt, KV_RANK), row),
                   pl.BlockSpec((nb, tt, KV_RANK), row), pl.BlockSpec((nb, tt, ROPE_DIM), row),
                   pl.BlockSpec((nb, tt, LANES), row)),
        out_shape=shapes,
        compiler_params=_params("parallel", "parallel"),
        name="lat",
    )(zmid, g_q.reshape(1, Q_RANK), g_kv.reshape(1, KV_RANK), cos, sin)


def _qprep_kernel(q_ref, wuk_ref, cos_ref, sin_ref, qa_ref, qr_ref):
    nb, tt, _ = q_ref.shape
    rows = nb * tt
    nope = N_HEADS * NOPE_DIM
    for h in range(N_HEADS):
        qn = q_ref[:, :, h * NOPE_DIM:(h + 1) * NOPE_DIM].reshape(rows, NOPE_DIM).astype(BF16)
        qa = jnp.dot(qn, wuk_ref[h], preferred_element_type=F32) * ATTN_SCALE
        qa_ref[h] = qa.reshape(nb, tt, KV_RANK).astype(qa_ref.dtype)
    cos = cos_ref[...]
    sin = sin_ref[...]
    for c in range(N_HEADS * ROPE_DIM // LANES):
        x = q_ref[:, :, nope + c * LANES:nope + (c + 1) * LANES]
        rot = _rot_half(x.reshape(rows, LANES)).reshape(nb, tt, LANES)
        r = ((x * cos + rot * sin) * ATTN_SCALE).astype(qr_ref.dtype)
        qr_ref[2 * c] = r[:, :, :ROPE_DIM]
        qr_ref[2 * c + 1] = r[:, :, ROPE_DIM:]


def _qprep(q, w_ukt, cos, sin):
    b, t, qw = q.shape
    nb, tt = (1, 256) if t >= 256 else (b, t)
    return pl.pallas_call(
        _qprep_kernel,
        grid=(b // nb, t // tt),
        in_specs=[pl.BlockSpec((nb, tt, qw), lambda i, j: (i, j, 0)),
                  pl.BlockSpec((N_HEADS, NOPE_DIM, KV_RANK), lambda i, j: (0, 0, 0)),
                  pl.BlockSpec((tt, LANES), lambda i, j: (j, 0)),
                  pl.BlockSpec((tt, LANES), lambda i, j: (j, 0))],
        out_specs=(pl.BlockSpec((N_HEADS, nb, tt, KV_RANK), lambda i, j: (0, i, j, 0)),
                   pl.BlockSpec((N_HEADS, nb, tt, ROPE_DIM), lambda i, j: (0, i, j, 0))),
        out_shape=(jax.ShapeDtypeStruct((N_HEADS, b, t, KV_RANK), BF16),
                   jax.ShapeDtypeStruct((N_HEADS, b, t, ROPE_DIM), BF16)),
        compiler_params=_params("parallel", "parallel"),
        name="qprep",
    )(q, w_ukt, cos, sin)


def _qcat_kernel(q_ref, cos_ref, sin_ref, o_ref):
    tt = q_ref.shape[0]
    scale = ATTN_SCALE * LOG2E
    nope = N_HEADS * NOPE_DIM
    cos = cos_ref[...]
    sin = sin_ref[...]
    low = lax.broadcasted_iota(jnp.int32, (tt, LANES), 1) < ROPE_DIM
    for c in range(N_HEADS * ROPE_DIM // LANES):
        x = q_ref[:, nope + c * LANES:nope + (c + 1) * LANES]
        r = (x * cos + _rot_half(x) * sin) * scale
        o_ref[2 * c, :, NOPE_DIM:] = jnp.where(low, r, 0.0).astype(o_ref.dtype)
        o_ref[2 * c + 1, :, NOPE_DIM:] = jnp.where(low, pltpu.roll(r, ROPE_DIM, 1), 0.0).astype(o_ref.dtype)
    for h in range(N_HEADS):
        o_ref[h, :, :NOPE_DIM] = (q_ref[:, h * NOPE_DIM:(h + 1) * NOPE_DIM] * scale).astype(o_ref.dtype)


def _qcat(q, cos, sin):
    b, t, qw = q.shape
    tt = 256
    return pl.pallas_call(
        _qcat_kernel,
        grid=(b, t // tt),
        in_specs=[pl.BlockSpec((None, tt, qw), lambda i, j: (i, j, 0)),
                  pl.BlockSpec((tt, LANES), lambda i, j: (j, 0)),
                  pl.BlockSpec((tt, LANES), lambda i, j: (j, 0))],
        out_specs=pl.BlockSpec((None, N_HEADS, tt, QK_DIM), lambda i, j: (i, 0, j, 0)),
        out_shape=jax.ShapeDtypeStruct((b, N_HEADS, t, QK_DIM), BF16),
        compiler_params=_params("parallel", "parallel"),
        name="qcat",
    )(q, cos, sin)


def _kvcat_kernel(ckv_ref, kr_ref, wuk_ref, wuv_ref, k_ref, v_ref):
    c = ckv_ref[...]
    kn = jnp.dot(c, wuk_ref[...], preferred_element_type=F32)
    vv = jnp.dot(c, wuv_ref[...], preferred_element_type=F32)
    kr = kr_ref[...]
    one_col = (lax.broadcasted_iota(jnp.int32, kr.shape, 1) == 0).astype(v_ref.dtype)
    for h in range(N_HEADS):
        k_ref[h, :, :NOPE_DIM] = kn[:, h * NOPE_DIM:(h + 1) * NOPE_DIM].astype(k_ref.dtype)
        k_ref[h, :, NOPE_DIM:] = kr
        v_ref[h, :, :V_DIM] = vv[:, h * V_DIM:(h + 1) * V_DIM].astype(v_ref.dtype)
        v_ref[h, :, V_DIM:] = one_col


def _kvcat(ckv_b, krope_b, w_uk2, w_uv2):
    b, t, _ = ckv_b.shape
    tt = 512
    return pl.pallas_call(
        _kvcat_kernel,
        grid=(b, t // tt),
        in_specs=[pl.BlockSpec((None, tt, KV_RANK), lambda i, j: (i, j, 0)),
                  pl.BlockSpec((None, tt, LANES), lambda i, j: (i, j, 0)),
                  pl.BlockSpec((KV_RANK, N_HEADS * NOPE_DIM), lambda i, j: (0, 0)),
                  pl.BlockSpec((KV_RANK, N_HEADS * V_DIM), lambda i, j: (0, 0))],
        out_specs=(pl.BlockSpec((None, N_HEADS, tt, QK_DIM), lambda i, j: (i, 0, j, 0)),
                   pl.BlockSpec((None, N_HEADS, tt, V_AUG), lambda i, j: (i, 0, j, 0))),
        out_shape=(jax.ShapeDtypeStruct((b, N_HEADS, t, QK_DIM), BF16),
                   jax.ShapeDtypeStruct((b, N_HEADS, t, V_AUG), BF16)),
        compiler_params=_params("parallel", "parallel"),
        name="kvcat",
    )(ckv_b, krope_b, w_uk2, w_uv2)


def _mha_kernel(q_ref, k_ref, v_ref, o_ref, m_ref, acc_ref, *, tb, gh):
    i = pl.program_id(2)
    dn = (((1,), (1,)), ((), ()))
    m_ref[...] = jnp.full(m_ref.shape, NEG, F32)
    acc_ref[...] = jnp.zeros(acc_ref.shape, F32)

    def block(j, masked):
        k0 = pl.multiple_of(j * tb, tb)
        for g in range(gh):
            s = lax.dot_general(q_ref[g], k_ref[g, pl.ds(k0, tb), :], dn, preferred_element_type=F32)
            if masked:
                qc = lax.broadcasted_iota(jnp.int32, (tb, tb), 0) >> CHUNK_SHIFT
                kc = lax.broadcasted_iota(jnp.int32, (tb, tb), 1) >> CHUNK_SHIFT
                s = jnp.where(kc <= qc, s, NEG)
            m_old = m_ref[g]
            m_new = jnp.maximum(m_old, jnp.max(s, axis=-1, keepdims=True))
            alpha = jnp.exp2(m_old - m_new)
            p = jnp.exp2(s - jnp.tile(m_new, (1, tb // LANES)))
            pv = jnp.dot(p.astype(BF16), v_ref[g, pl.ds(k0, tb), :], preferred_element_type=F32)
            acc_ref[g] = jnp.tile(alpha, (1, V_AUG // LANES)) * acc_ref[g] + pv
            m_ref[g] = m_new

    def full_block(j, carry):
        block(j, False)
        return carry

    lax.fori_loop(0, i, full_block, 0)
    block(i, True)
    for g in range(gh):
        acc = acc_ref[g]
        o_ref[:, g * V_DIM:(g + 1) * V_DIM] = (acc[:, :V_DIM] / acc[:, V_DIM:V_DIM + 1]).astype(o_ref.dtype)


def _mha(qc, kc, vc):
    b, nh, t, _ = qc.shape
    tb = 512
    gh = 4
    return pl.pallas_call(
        functools.partial(_mha_kernel, tb=tb, gh=gh),
        grid=(b, nh // gh, t // tb),
        in_specs=[pl.BlockSpec((None, gh, tb, QK_DIM), lambda bi, hg, i: (bi, hg, i, 0)),
                  pl.BlockSpec((None, gh, t, QK_DIM), lambda bi, hg, i: (bi, hg, 0, 0)),
                  pl.BlockSpec((None, gh, t, V_AUG), lambda bi, hg, i: (bi, hg, 0, 0))],
        out_specs=pl.BlockSpec((None, tb, gh * V_DIM), lambda bi, hg, i: (bi, i, hg)),
        out_shape=jax.ShapeDtypeStruct((b, t, nh * V_DIM), BF16),
        scratch_shapes=[pltpu.VMEM((gh, tb, LANES), F32), pltpu.VMEM((gh, tb, V_AUG), F32)],
        compiler_params=_params("parallel", "parallel", "parallel"),
        name="mha",
    )(qc, kc, vc)


def _last_kv_block(i, *, tq, tk, pos0, nk):
    last_pos = (pos0 + (i + 1) * tq - 1) // CHUNK * CHUNK + CHUNK - 1
    return jnp.minimum(last_pos // tk, nk - 1)


def _attn_kernel(qa_ref, qr_ref, k_ref, kr_ref, wuv_ref, o_ref, m_ref, l_ref, acc_ref,
                 *, tq, tk, pos0, s_valid, nk):
    i = pl.program_id(1)
    kk = pl.program_id(2)
    rows = N_HEADS * tq

    @pl.when(kk == 0)
    def _():
        m_ref[...] = jnp.full(m_ref.shape, NEG, F32)
        l_ref[...] = jnp.zeros(l_ref.shape, F32)
        acc_ref[...] = jnp.zeros(acc_ref.shape, F32)

    @pl.when(kk <= _last_kv_block(i, tq=tq, tk=tk, pos0=pos0, nk=nk))
    def _():
        qa = qa_ref[...].reshape(rows, KV_RANK)
        qr = qr_ref[...].reshape(rows, ROPE_DIM)
        k = k_ref[...]
        dn = (((1,), (1,)), ((), ()))
        s = lax.dot_general(qa, k, dn, preferred_element_type=F32)
        s = s + lax.dot_general(qr, kr_ref[...], dn, preferred_element_type=F32)
        qpos = pos0 + i * tq + lax.broadcasted_iota(jnp.int32, (tq, tk), 0)
        kpos = kk * tk + lax.broadcasted_iota(jnp.int32, (tq, tk), 1)
        ok = jnp.logical_and((kpos >> CHUNK_SHIFT) <= (qpos >> CHUNK_SHIFT), kpos < s_valid)
        s = jnp.where(ok[None], s.reshape(N_HEADS, tq, tk), NEG).reshape(rows, tk)
        m_old = m_ref[...]
        m_new = jnp.maximum(m_old, jnp.max(s, axis=-1, keepdims=True))
        alpha = jnp.exp(m_old - m_new)
        p = jnp.exp(s - m_new)
        l_ref[...] = alpha * l_ref[...] + jnp.sum(p, axis=-1, keepdims=True)
        acc_ref[...] = alpha * acc_ref[...] + jnp.dot(p.astype(BF16), k, preferred_element_type=F32)
        m_ref[...] = m_new

    @pl.when(kk == nk - 1)
    def _():
        o_lat = (acc_ref[...] / l_ref[...]).astype(BF16).reshape(N_HEADS, tq, KV_RANK)
        for h in range(N_HEADS):
            o_h = jnp.dot(o_lat[h], wuv_ref[h], preferred_element_type=F32)
            o_ref[:, h * V_DIM:(h + 1) * V_DIM] = o_h.astype(o_ref.dtype)


def _attn(q_abs, q_rope, ckv_all, krope_all, w_uv, *, pos0, s_valid):
    _, b, t, _ = q_abs.shape
    s_len = ckv_all.shape[1]
    tq = min(t, 128)
    tk = next(c for c in (1536, 1024, 512) if s_len % c == 0) if t < 128 else 512
    nk = s_len // tk
    last = functools.partial(_last_kv_block, tq=tq, tk=tk, pos0=pos0, nk=nk)
    kv_idx = lambda bi, i, kk: (bi, jnp.minimum(kk, last(i)), 0)
    return pl.pallas_call(
        functools.partial(_attn_kernel, tq=tq, tk=tk, pos0=pos0, s_valid=s_valid, nk=nk),
        grid=(b, t // tq, nk),
        in_specs=[pl.BlockSpec((N_HEADS, None, tq, KV_RANK), lambda bi, i, kk: (0, bi, i, 0)),
                  pl.BlockSpec((N_HEADS, None, tq, ROPE_DIM), lambda bi, i, kk: (0, bi, i, 0)),
                  pl.BlockSpec((None, tk, KV_RANK), kv_idx),
                  pl.BlockSpec((None, tk, ROPE_DIM), kv_idx),
                  pl.BlockSpec((N_HEADS, KV_RANK, V_DIM), lambda bi, i, kk: (0, 0, 0))],
        out_specs=pl.BlockSpec((None, tq, N_HEADS * V_DIM), lambda bi, i, kk: (bi, i, 0)),
        out_shape=jax.ShapeDtypeStruct((b, t, N_HEADS * V_DIM), BF16),
        scratch_shapes=[pltpu.VMEM((N_HEADS * tq, 1), F32), pltpu.VMEM((N_HEADS * tq, 1), F32),
                        pltpu.VMEM((N_HEADS * tq, KV_RANK), F32)],
        compiler_params=_params("parallel", "parallel", "arbitrary"),
        name="attn",
    )(q_abs, q_rope, ckv_all, krope_all, w_uv)


def _accumulate(acc_ref, kk, a, w_ref):
    part = jnp.dot(a, w_ref[...], preferred_element_type=F32)

    @pl.when(kk == 0)
    def _():
        acc_ref[...] = part

    @pl.when(kk > 0)
    def _():
        acc_ref[...] += part


def _norm_residual(acc_ref, x_ref, gate_ref, gn_ref, o_ref):
    nb, tt, d = x_ref.shape
    y = acc_ref[...].reshape(nb, tt, d)
    yn = y * lax.rsqrt(jnp.mean(y * y, axis=-1, keepdims=True) + EPS) * gn_ref[...]
    o_ref[...] = x_ref[...] + gate_ref[...] * yn


def _merge_out_kernel(g0, g1, g2, oa, ob, oc, w_ref, x_ref, gate_ref, gn_ref, o_ref, acc_ref):
    kk = pl.program_id(2)
    nb, tt, _ = x_ref.shape
    a = (g0[...].astype(F32) * oa[...].astype(F32) + g1[...].astype(F32) * ob[...].astype(F32)
         + g2[...].astype(F32) * oc[...].astype(F32)).astype(BF16)
    _accumulate(acc_ref, kk, a.reshape(nb * tt, a.shape[-1]), w_ref)

    @pl.when(kk == pl.num_programs(2) - 1)
    def _():
        _norm_residual(acc_ref, x_ref, gate_ref, gn_ref, o_ref)


def _merge_out(gates, oa, ob, oc, w, x, gate, g_norm):
    b, t, d = x.shape
    k = w.shape[0]
    nb, tt = _row_tiles(b, t)
    tk = 1024
    nkb = k // tk
    row = lambda i, j, kk: (i, j, kk)
    branch = lambda o, i, j, kk: (i, j, o * nkb + kk)
    full = lambda i, j, kk: (i, j, 0)
    return pl.pallas_call(
        _merge_out_kernel,
        grid=(b // nb, t // tt, nkb),
        in_specs=[pl.BlockSpec((nb, tt, tk), functools.partial(branch, o)) for o in range(3)]
        + [pl.BlockSpec((nb, tt, tk), row)] * 3
        + [pl.BlockSpec((tk, d), lambda i, j, kk: (kk, 0)),
           pl.BlockSpec((nb, tt, d), full),
           pl.BlockSpec((nb, 1, d), lambda i, j, kk: (i, 0, 0)),
           pl.BlockSpec((1, d), lambda i, j, kk: (0, 0))],
        out_specs=pl.BlockSpec((nb, tt, d), full),
        out_shape=jax.ShapeDtypeStruct((b, t, d), F32),
        scratch_shapes=[pltpu.VMEM((nb * tt, d), F32)],
        compiler_params=_params("parallel", "parallel", "arbitrary"),
        name="mm_out",
    )(gates, gates, gates, oa, ob, oc, w, x, gate, g_norm.reshape(1, d))


def _ffn_down_kernel(ug_ref, uv_ref, prev_ref, hist_ref, wd_ref, bd_ref, w_ref, x_ref, gate_ref, gn_ref, o_ref,
                     ext_ref, acc_ref):
    j = pl.program_id(1)
    kk = pl.program_id(2)
    nb, tt, _ = x_ref.shape
    ext_ref[:, 0:FFN_HALO, :] = jnp.where(j == 0, hist_ref[...], prev_ref[...].astype(F32))
    ext_ref[:, FFN_HALO:, :] = ug_ref[...].astype(F32)
    lead = FFN_HALO - (FFN_K - 1)
    tk = ext_ref.shape[-1]
    part = None
    for c0 in range(0, tk, FFN_SUB):
        c1 = min(c0 + FFN_SUB, tk)
        conv = bd_ref[:, c0:c1]
        for k in range(FFN_K):
            conv = conv + wd_ref[k:k + 1, c0:c1] * ext_ref[:, lead + k:lead + k + tt, c0:c1]
        act = (_silu(conv) * uv_ref[:, :, c0:c1].astype(F32)).astype(BF16)
        dot = jnp.dot(act.reshape(nb * tt, c1 - c0), w_ref[c0:c1, :], preferred_element_type=F32)
        part = dot if part is None else part + dot

    @pl.when(kk == 0)
    def _():
        acc_ref[...] = part

    @pl.when(kk > 0)
    def _():
        acc_ref[...] += part

    @pl.when(kk == pl.num_programs(2) - 1)
    def _():
        _norm_residual(acc_ref, x_ref, gate_ref, gn_ref, o_ref)


def _ffn_down(up, hist, w_dw, b_dw, w, x, gate, g_norm):
    b, t, d = x.shape
    nb, tt = _row_tiles(b, t)
    tk = D_FF // 4
    nkb = D_FF // tk
    hb = tt // FFN_HALO
    full = lambda i, j, kk: (i, j, 0)
    return pl.pallas_call(
        _ffn_down_kernel,
        grid=(b // nb, t // tt, nkb),
        in_specs=[pl.BlockSpec((nb, tt, tk), lambda i, j, kk: (i, j, kk)),
                  pl.BlockSpec((nb, tt, tk), lambda i, j, kk: (i, j, nkb + kk)),
                  pl.BlockSpec((nb, FFN_HALO, tk), lambda i, j, kk: (i, jnp.maximum(j * hb - 1, 0), kk)),
                  pl.BlockSpec((nb, FFN_HALO, tk), lambda i, j, kk: (i, 0, kk)),
                  pl.BlockSpec((FFN_K, tk), lambda i, j, kk: (0, kk)),
                  pl.BlockSpec((1, tk), lambda i, j, kk: (0, kk)),
                  pl.BlockSpec((tk, d), lambda i, j, kk: (kk, 0)),
                  pl.BlockSpec((nb, tt, d), full),
                  pl.BlockSpec((nb, 1, d), lambda i, j, kk: (i, 0, 0)),
                  pl.BlockSpec((1, d), lambda i, j, kk: (0, 0))],
        out_specs=pl.BlockSpec((nb, tt, d), full),
        out_shape=jax.ShapeDtypeStruct((b, t, d), F32),
        scratch_shapes=[pltpu.VMEM((nb, tt + FFN_HALO, tk), F32), pltpu.VMEM((nb * tt, d), F32)],
        compiler_params=_params("parallel", "parallel", "arbitrary"),
        name="ffn_down",
    )(up, up, up, hist, w_dw, b_dw.reshape(1, D_FF), w, x, gate, g_norm.reshape(1, d))


def _rope_tables(pos0, t):
    half = ROPE_DIM // 2
    inv = ROPE_THETA ** (-jnp.arange(half, dtype=F32) / half)
    pos = (pos0 + jnp.arange(t, dtype=jnp.int32)).astype(F32)
    ang = pos[:, None] * inv[None, :]
    cos, sin = jnp.cos(ang), jnp.sin(ang)
    return (jnp.concatenate([cos, cos, cos, cos], axis=-1),
            jnp.concatenate([-sin, sin, -sin, sin], axis=-1))


def _front_pad(a, rows):
    return jnp.pad(a, ((0, 0), (rows - a.shape[1], 0), (0, 0)))


def _tail(hist, new, n, cols):
    keep = min(new.shape[1], n)
    return jnp.concatenate([hist, new[:, new.shape[1] - keep:, :cols].astype(F32)], axis=1)[:, -n:]


def _attend(g, q, ckv_b, krope_b, cos, sin, p):
    t = q.shape[1]
    if g["past_ckv"].shape[1] == 0 and t % 512 == 0:
        kc, vc = _kvcat(ckv_b, krope_b, p["w_uk2"], p["w_uv2"])
        return _mha(_qcat(q, cos, sin), kc, vc)
    q_abs, q_rope = _qprep(q, p["w_ukt"], cos, sin)
    s_valid = g["past_ckv"].shape[1] + t
    s_pad = -(-s_valid // 512) * 512
    kv_pad = ((0, 0), (0, s_pad - s_valid), (0, 0))
    ckv_all = jnp.pad(jnp.concatenate([g["past_ckv"].astype(BF16), ckv_b], axis=1), kv_pad)
    krope_all = jnp.pad(jnp.concatenate([g["past_krope"].astype(BF16), krope_b[:, :, :ROPE_DIM]], axis=1), kv_pad)
    return _attn(q_abs, q_rope, ckv_all, krope_all, p["w_uv"], pos0=g["pos0"], s_valid=s_valid)


def _layer(groups, l, w, p):
    d = D_MODEL
    dims = [g["x"].shape[:2] for g in groups]
    flat = lambda arrs: [a.reshape(-1, a.shape[-1]) for a in arrs]
    unflat = lambda outs: [o.reshape(b, t, o.shape[-1]) for o, (b, t) in zip(outs, dims)]
    mods = [[g["mod"][:, i] for i in range(6)] for g in groups]
    tables = [_rope_tables(g["pos0"], t) for g, (_, t) in zip(groups, dims)]

    h = flat([_norm_mod(g["x"], p["g_pre_mix"], m[1], m[0]) for g, m in zip(groups, mods)])
    glu_tn = 512
    u_a = unflat(_mm(*h, [(w["w_in"], l, 0), (w["w_in"], l, D_CONV // glu_tn)], D_CONV, F32,
                     tn=glu_tn, epi="glu", name="mm_glu"))
    zmid = unflat(_mm(*h, [(w["w_in"], l, OFF_B // MID_TN)], MID_W, F32, tn=MID_TN, name="mm_mid"))
    gates = unflat(_mm(*h, [(p["w_g"], None, 0)], 3 * d, BF16, tn=1024, epi="sigmoid", name="mm_gates"))

    a_act = [_conv_a(u, _front_pad(g["hist_conv"], CONV_HALO), p["w_dwa"], p["b_dwa"], p["ln_a_g"], p["ln_a_b"])
             for u, g in zip(u_a, groups)]
    out_a = unflat(_mm(*flat(a_act), [(w["w_pa"], l, 0)], d, BF16, tn=512, name="mm_pa"))

    pooled = [_pool(z, _front_pad(g["hist_pool"], POOL_HALO), g["pos0"]) for z, g in zip(zmid, groups)]
    out_b = unflat([_pool_mm(m, p["w_pool"], p["pool_scale"]) for m in flat(pooled)])

    lat = [_lat(z, p["g_q_lat"], p["g_kv_lat"], cos, sin) for z, (cos, sin) in zip(zmid, tables)]
    q = unflat(_mm(*flat([o[0] for o in lat]), [(p["w_uq"], None, 0)], p["w_uq"].shape[1], F32, tn=512, name="mm_uq"))
    o = [_attend(g, qg, lg[2], lg[4], cos, sin, p) for g, qg, lg, (cos, sin) in zip(groups, q, lat, tables)]
    out_c = unflat(_mm(*flat(o), [(w["w_oc"], l, 0)], d, BF16, tn=512, name="mm_oc"))

    x = [_merge_out(gt, oa, ob, oc, p["w_out"], g["x"], m[2], p["g_post_mix"])
         for gt, oa, ob, oc, g, m in zip(gates, out_a, out_b, out_c, groups, mods)]

    h2 = flat([_norm_mod(xg, p["g_pre_ffn"], m[4], m[3]) for xg, m in zip(x, mods)])
    up = unflat(_mm(*h2, [(w["w_up"], l, 0)], 2 * D_FF, BF16, tn=1024, name="mm_up"))
    x = [_ffn_down(u, _front_pad(g["hist_ffn"], FFN_HALO), p["w_dwf"], p["b_dwf"], p["w_down"], xg, m[5],
                   p["g_post_ffn"]) for u, g, xg, m in zip(up, groups, x, mods)]

    states = [(lg[1], lg[3],
               _tail(g["hist_conv"], u, CONV_K - 1, D_CONV),
               _tail(g["hist_pool"], z, POOL_MAX - 1, D_POOL),
               _tail(g["hist_ffn"], uu, FFN_K - 1, D_FF))
              for lg, g, u, z, uu in zip(lat, groups, u_a, zmid, up)]
    return x, states


def _prep_weights(l, w):
    w_uq = w["w_uq"][l].reshape(Q_RANK, N_HEADS, NOPE_DIM + ROPE_DIM)
    w_uq = jnp.concatenate([w_uq[:, :, :NOPE_DIM].reshape(Q_RANK, -1), w_uq[:, :, NOPE_DIM:].reshape(Q_RANK, -1)], axis=1)
    p = dict(
        w_g=w["w_in"][l, :, OFF_G:].astype(BF16),
        w_pool=w["w_pool"][l].astype(BF16),
        w_uq=w_uq.astype(BF16),
        w_ukt=jnp.transpose(w["w_uk"][l], (1, 2, 0)).astype(BF16),
        w_uv=jnp.transpose(w["w_uv"][l], (1, 0, 2)).astype(BF16),
        w_uk2=w["w_uk"][l].reshape(KV_RANK, N_HEADS * NOPE_DIM).astype(BF16),
        w_uv2=w["w_uv"][l].reshape(KV_RANK, N_HEADS * V_DIM).astype(BF16),
        w_out=w["w_out"][l].astype(BF16), w_down=w["w_down"][l].astype(BF16),
    )
    for name in ("g_pre_mix", "g_post_mix", "w_dwa", "b_dwa", "ln_a_g", "ln_a_b", "pool_scale", "g_q_lat",
                 "g_kv_lat", "g_pre_ffn", "g_post_ffn", "w_dwf", "b_dwf"):
        p[name] = w[name][l]
    return p


def kernel(x_prompt, x_sample, cache_ckv, cache_krope, state_conv, state_pool, state_ffn, c_prompt, c_sample, w_mod, b_mod, g_pre_mix, g_post_mix, w_in, w_dwa, b_dwa, ln_a_g, ln_a_b, w_pa, w_pool, pool_scale, g_q_lat, g_kv_lat, w_uq, w_uk, w_uv, w_oc, w_out, g_pre_ffn, g_post_ffn, w_up, w_dwf, b_dwf, w_down):
    weights = dict(g_pre_mix=g_pre_mix, g_post_mix=g_post_mix, w_in=w_in, w_dwa=w_dwa, b_dwa=b_dwa,
                   ln_a_g=ln_a_g, ln_a_b=ln_a_b, w_pa=w_pa, w_pool=w_pool, pool_scale=pool_scale,
                   g_q_lat=g_q_lat, g_kv_lat=g_kv_lat, w_uq=w_uq, w_uk=w_uk, w_uv=w_uv, w_oc=w_oc,
                   w_out=w_out, g_pre_ffn=g_pre_ffn, g_post_ffn=g_post_ffn, w_up=w_up, w_dwf=w_dwf,
                   b_dwf=b_dwf, w_down=w_down)
    depth = w_mod.shape[0]
    bp, bs = x_prompt.shape[0], x_sample.shape[0]
    past_len = cache_ckv.shape[2]
    d = x_prompt.shape[-1]

    rows = -(-(bp + bs) // SUBLANES) * SUBLANES
    c_all = jnp.pad(jnp.concatenate([c_prompt, c_sample], axis=0), ((0, rows - bp - bs), (0, 0)))
    mod_all = _mod(c_all, w_mod, b_mod)

    xp, xs = x_prompt, x_sample
    st_p = [[] for _ in range(5)]
    st_s = [[] for _ in range(5)]
    for l in range(depth):
        p = _prep_weights(l, weights)
        prompt = dict(x=xp, mod=mod_all[l, :bp].reshape(bp, 6, 1, d), pos0=0,
                      past_ckv=jnp.zeros((bp, 0, KV_RANK), F32), past_krope=jnp.zeros((bp, 0, ROPE_DIM), F32),
                      hist_conv=jnp.zeros((bp, CONV_K - 1, D_CONV), F32),
                      hist_pool=jnp.zeros((bp, POOL_MAX - 1, D_POOL), F32),
                      hist_ffn=jnp.zeros((bp, FFN_K - 1, D_FF), F32))
        sample = dict(x=xs, mod=mod_all[l, bp:bp + bs].reshape(bs, 6, 1, d), pos0=past_len,
                      past_ckv=cache_ckv[l], past_krope=cache_krope[l], hist_conv=state_conv[l],
                      hist_pool=state_pool[l], hist_ffn=state_ffn[l])
        (xp, xs), (sp, ss) = _layer((prompt, sample), l, weights, p)
        for i in range(5):
            st_p[i].append(sp[i])
            st_s[i].append(ss[i])
    return (xp, xs) + tuple(jnp.stack(s) for s in st_p) + tuple(jnp.stack(s) for s in st_s)
```

```python
import functools

import jax
import jax.numpy as jnp
from jax import lax
from jax.experimental import pallas as pl
from jax.experimental.pallas import tpu as pltpu

F32 = jnp.float32
BF16 = jnp.bfloat16

D_MODEL = 2048
CHUNK = 64
CHUNK_SHIFT = 6
assert 1 << CHUNK_SHIFT == CHUNK
D_CONV = D_MODEL // 2
CONV_K = 31
D_POOL = D_MODEL // 2
POOL_WINDOWS = (2, 4, 8, 16)
POOL_MAX = 16
N_POOL_GROUPS = 4
POOL_GROUP = D_POOL // N_POOL_GROUPS
POOL_OUT = D_MODEL // N_POOL_GROUPS
N_HEADS = D_MODEL // 128
NOPE_DIM = 128
ROPE_DIM = 64
V_DIM = 128
Q_RANK = D_MODEL // 4
KV_RANK = D_MODEL // 4
ROPE_THETA = 10000.0
ATTN_SCALE = (NOPE_DIM + ROPE_DIM) ** -0.5
LOG2E = 1.4426950408889634
QK_DIM = 256
V_AUG = 256
D_FF = 256 * ((8 * D_MODEL // 3 + 255) // 256)
FFN_K = 3
EPS = 1e-6
NEG = -1e30
OFF_B = 2 * D_CONV
OFF_R = OFF_B + D_POOL + Q_RANK + KV_RANK
OFF_G = OFF_R + ROPE_DIM

LANES = 128
SUBLANES = 8
VMEM_LIMIT_BYTES = 56 * 1024 * 1024

MID_Q = D_POOL
MID_KV = MID_Q + Q_RANK
MID_R = MID_KV + KV_RANK
MID_W = MID_R + 2 * LANES
MID_TN = MID_W // 3
CONV_HALO = 32
POOL_HALO = 16
FFN_HALO = 8
FFN_SUB = 512


def _params(*sem):
    return pltpu.CompilerParams(dimension_semantics=sem, vmem_limit_bytes=VMEM_LIMIT_BYTES)


def _sigmoid(x):
    return 0.5 + 0.5 * jnp.tanh(0.5 * x)


def _silu(x):
    h = 0.5 * x
    return h + h * jnp.tanh(h)


def _row_tiles(b, t):
    if t >= 512:
        return 1, 512
    return b, t


def _mod_kernel(c_ref, w_ref, b_ref, o_ref):
    c = c_ref[...]
    a = _silu(c).astype(BF16)
    o_ref[...] = jnp.dot(a, w_ref[...].astype(BF16), preferred_element_type=F32) + b_ref[...]


def _mod(c_all, w_mod, b_mod):
    nl, d, n = w_mod.shape
    bp = c_all.shape[0]
    tn = 1024
    return pl.pallas_call(
        _mod_kernel,
        grid=(nl, n // tn),
        in_specs=[pl.BlockSpec((bp, d), lambda l, j: (0, 0)),
                  pl.BlockSpec((None, d, tn), lambda l, j: (l, 0, j)),
                  pl.BlockSpec((None, 1, tn), lambda l, j: (l, 0, j))],
        out_specs=pl.BlockSpec((None, bp, tn), lambda l, j: (l, 0, j)),
        out_shape=jax.ShapeDtypeStruct((nl, bp, n), F32),
        compiler_params=_params("parallel", "parallel"),
        name="mod",
    )(c_all, w_mod, b_mod.reshape(nl, 1, n))


def _norm_mod_kernel(x_ref, g_ref, sc_ref, sh_ref, o_ref):
    x = x_ref[...]
    y = x * lax.rsqrt(jnp.mean(x * x, axis=-1, keepdims=True) + EPS) * g_ref[...]
    o_ref[...] = (y * (1.0 + sc_ref[...]) + sh_ref[...]).astype(o_ref.dtype)


def _norm_mod(x, g, scale, shift):
    b, t, d = x.shape
    nb, tt = _row_tiles(b, t)
    return pl.pallas_call(
        _norm_mod_kernel,
        grid=(b // nb, t // tt),
        in_specs=[pl.BlockSpec((nb, tt, d), lambda i, j: (i, j, 0)),
                  pl.BlockSpec((1, d), lambda i, j: (0, 0)),
                  pl.BlockSpec((nb, 1, d), lambda i, j: (i, 0, 0)),
                  pl.BlockSpec((nb, 1, d), lambda i, j: (i, 0, 0))],
        out_specs=pl.BlockSpec((nb, tt, d), lambda i, j: (i, j, 0)),
        out_shape=jax.ShapeDtypeStruct((b, t, d), BF16),
        compiler_params=_params("parallel", "parallel"),
        name="norm_mod",
    )(x, g.reshape(1, d), scale, shift)


def _mm_kernel(*refs, epi, n_w, n_prompt_tiles):
    ap_ref, as_ref = refs[0], refs[1]
    w_refs = refs[2:2 + n_w]
    op_ref, os_ref = refs[2 + n_w], refs[3 + n_w]
    wb_refs = refs[4 + n_w:]
    m = pl.program_id(1)

    @pl.when(m == 0)
    def _():
        for w_ref, wb_ref in zip(w_refs, wb_refs):
            wb_ref[...] = w_ref[...].astype(BF16)

    def compute(a):
        z = jnp.dot(a, wb_refs[0][...], preferred_element_type=F32)
        if epi == "sigmoid":
            z = _sigmoid(z)
        elif epi == "glu":
            z = z * _sigmoid(jnp.dot(a, wb_refs[1][...], preferred_element_type=F32))
        return z

    @pl.when(m < n_prompt_tiles)
    def _():
        op_ref[...] = compute(ap_ref[...]).astype(op_ref.dtype)

    @pl.when(m == n_prompt_tiles)
    def _():
        os_ref[...] = compute(as_ref[...]).astype(os_ref.dtype)


def _mm(a_p, a_s, ws, n, out_dtype, *, tn, epi="none", name="mm"):
    rp, k = a_p.shape
    rs = a_s.shape[0]
    tm = min(rp, 1024)
    npt = rp // tm

    def w_spec(arr, lead, first):
        if lead is None:
            return pl.BlockSpec((k, tn), lambda j, m: (0, first + j))
        return pl.BlockSpec((None, k, tn), lambda j, m: (lead, 0, first + j))

    p_row = lambda j, m: (jnp.minimum(m, npt - 1), 0)
    p_out = lambda j, m: (jnp.minimum(m, npt - 1), j)
    return pl.pallas_call(
        functools.partial(_mm_kernel, epi=epi, n_w=len(ws), n_prompt_tiles=npt),
        grid=(n // tn, npt + 1),
        in_specs=[pl.BlockSpec((tm, k), p_row), pl.BlockSpec((rs, k), lambda j, m: (0, 0))]
        + [w_spec(*w) for w in ws],
        out_specs=(pl.BlockSpec((tm, tn), p_out), pl.BlockSpec((rs, tn), lambda j, m: (0, j))),
        out_shape=(jax.ShapeDtypeStruct((rp, n), out_dtype), jax.ShapeDtypeStruct((rs, n), out_dtype)),
        scratch_shapes=[pltpu.VMEM((k, tn), BF16) for _ in ws],
        compiler_params=_params("parallel", "arbitrary"),
        name=name,
    )(a_p, a_s, *[w[0] for w in ws])


def _conv_a_kernel(u_ref, prev_ref, hist_ref, w_ref, b_ref, g_ref, be_ref, o_ref, ext_ref, sh_ref, a_ref, *, tt):
    j = pl.program_id(1)
    ext_ref[0:CONV_HALO, :] = jnp.where(j == 0, hist_ref[...], prev_ref[...])
    ext_ref[CONV_HALO:, :] = u_ref[...]
    sh_rows = sh_ref.shape[1]
    for s in range(1, SUBLANES):
        sh_ref[s - 1] = ext_ref[s:s + sh_rows, :]
    rc = 32
    cc = 512
    lead = CONV_HALO - (CONV_K - 1)

    def body(r, carry):
        r0 = pl.multiple_of(r * rc, rc)
        for c0 in range(0, D_CONV, cc):
            acc = jnp.zeros((rc, cc), F32)
            for k in range(CONV_K):
                s = (lead + k) % SUBLANES
                row = pl.multiple_of(r0 + (lead + k - s), SUBLANES)
                if s == 0:
                    x = ext_ref[pl.ds(row, rc), c0:c0 + cc]
                else:
                    x = sh_ref[s - 1, pl.ds(row, rc), c0:c0 + cc]
                acc = acc + w_ref[k:k + 1, c0:c0 + cc] * x
            a_ref[pl.ds(r0, rc), c0:c0 + cc] = acc
        return carry

    lax.fori_loop(0, tt // rc, body, 0)
    a = a_ref[...] + b_ref[...]
    mu = jnp.mean(a, axis=-1, keepdims=True)
    ac = a - mu
    var = jnp.mean(ac * ac, axis=-1, keepdims=True)
    y = ac * lax.rsqrt(var + EPS) * g_ref[...] + be_ref[...]
    o_ref[...] = _silu(y).astype(o_ref.dtype)


def _conv_a(u, hist, w_dw, b_dw, ln_g, ln_b):
    b, t, c = u.shape
    tt = min(t, 256)
    hb = tt // CONV_HALO
    vec = lambda i, j: (0, 0)
    return pl.pallas_call(
        functools.partial(_conv_a_kernel, tt=tt),
        grid=(b, t // tt),
        in_specs=[pl.BlockSpec((None, tt, c), lambda i, j: (i, j, 0)),
                  pl.BlockSpec((None, CONV_HALO, c), lambda i, j: (i, jnp.maximum(j * hb - 1, 0), 0)),
                  pl.BlockSpec((None, CONV_HALO, c), lambda i, j: (i, 0, 0)),
                  pl.BlockSpec((CONV_K, c), vec),
                  pl.BlockSpec((1, c), vec), pl.BlockSpec((1, c), vec), pl.BlockSpec((1, c), vec)],
        out_specs=pl.BlockSpec((None, tt, c), lambda i, j: (i, j, 0)),
        out_shape=jax.ShapeDtypeStruct((b, t, c), BF16),
        scratch_shapes=[pltpu.VMEM((tt + CONV_HALO, c), F32),
                        pltpu.VMEM((SUBLANES - 1, tt + CONV_HALO - SUBLANES, c), F32),
                        pltpu.VMEM((tt, c), F32)],
        compiler_params=_params("parallel", "parallel"),
        name="conv_a",
    )(u, u, hist, w_dw, b_dw.reshape(1, c), ln_g.reshape(1, c), ln_b.reshape(1, c))


def _pool_kernel(z_ref, prev_ref, hist_ref, w_ref, s_ref, o_ref, ext_ref, *, tt, pos0):
    j = pl.program_id(1)
    ext_ref[0:POOL_HALO, :] = jnp.where(j == 0, hist_ref[...], prev_ref[...])
    ext_ref[POOL_HALO:, :] = z_ref[...]
    pos = pos0 + j * tt + lax.broadcasted_iota(jnp.int32, (tt, 1), 0)
    for g, w in enumerate(POOL_WINDOWS):
        c0 = g * POOL_GROUP
        cur = ext_ref[:, c0:c0 + POOL_GROUP]
        sh = 1
        while sh < w:
            cur = cur + pltpu.roll(cur, sh, 0)
            sh *= 2
        win = cur[POOL_HALO:, :]
        cnt = jnp.minimum(w, pos + 1).astype(F32)
        m = (win / cnt - ext_ref[POOL_HALO:, c0:c0 + POOL_GROUP]).astype(BF16)
        out = jnp.dot(m, w_ref[g], preferred_element_type=F32) * s_ref[:, g * POOL_OUT:(g + 1) * POOL_OUT]
        o_ref[:, g * POOL_OUT:(g + 1) * POOL_OUT] = out.astype(o_ref.dtype)


def _pool(zmid, hist, pos0, w_pool, pool_scale):
    b, t, _ = zmid.shape
    c = D_POOL
    tt = min(t, 512)
    hb = tt // POOL_HALO
    return pl.pallas_call(
        functools.partial(_pool_kernel, tt=tt, pos0=pos0),
        grid=(b, t // tt),
        in_specs=[pl.BlockSpec((None, tt, c), lambda i, j: (i, j, 0)),
                  pl.BlockSpec((None, POOL_HALO, c), lambda i, j: (i, jnp.maximum(j * hb - 1, 0), 0)),
                  pl.BlockSpec((None, POOL_HALO, c), lambda i, j: (i, 0, 0)),
                  pl.BlockSpec((N_POOL_GROUPS, POOL_GROUP, POOL_OUT), lambda i, j: (0, 0, 0)),
                  pl.BlockSpec((1, D_MODEL), lambda i, j: (0, 0))],
        out_specs=pl.BlockSpec((None, tt, D_MODEL), lambda i, j: (i, j, 0)),
        out_shape=jax.ShapeDtypeStruct((b, t, D_MODEL), BF16),
        scratch_shapes=[pltpu.VMEM((tt + POOL_HALO, c), F32)],
        compiler_params=_params("parallel", "parallel"),
        name="pool",
    )(zmid, zmid, hist, w_pool, pool_scale.reshape(1, D_MODEL))


def _rot_half(x):
    lane = lax.broadcasted_iota(jnp.int32, x.shape, 1)
    first = (lane % ROPE_DIM) < (ROPE_DIM // 2)
    return jnp.where(first, pltpu.roll(x, LANES - ROPE_DIM // 2, 1), pltpu.roll(x, ROPE_DIM // 2, 1))


def _lat_kernel(z_ref, gq_ref, gkv_ref, cos_ref, sin_ref, ql_ref, ckv_ref, ckvb_ref, kr_ref, krb_ref):
    nb, tt, _ = z_ref.shape
    zq = z_ref[:, :, MID_Q:MID_KV]
    ql = zq * lax.rsqrt(jnp.mean(zq * zq, axis=-1, keepdims=True) + EPS) * gq_ref[...]
    ql_ref[...] = ql.astype(ql_ref.dtype)
    zkv = z_ref[:, :, MID_KV:MID_R]
    ckv = zkv * lax.rsqrt(jnp.mean(zkv * zkv, axis=-1, keepdims=True) + EPS) * gkv_ref[...]
    ckv_ref[...] = ckv
    ckvb_ref[...] = ckv.astype(BF16)
    zr = z_ref[:, :, MID_R:MID_R + LANES]
    rot = _rot_half(zr.reshape(nb * tt, LANES)).reshape(nb, tt, LANES)
    kr = zr * cos_ref[...] + rot * sin_ref[...]
    kr_ref[...] = kr[:, :, :ROPE_DIM]
    krb_ref[...] = kr.astype(BF16)


def _lat(zmid, g_q, g_kv, cos, sin):
    b, t, _ = zmid.shape
    nb, tt = _row_tiles(b, t)
    row = lambda i, j: (i, j, 0)
    vec = lambda i, j: (0, 0)
    shapes = (jax.ShapeDtypeStruct((b, t, Q_RANK), BF16),
              jax.ShapeDtypeStruct((b, t, KV_RANK), F32),
              jax.ShapeDtypeStruct((b, t, KV_RANK), BF16),
              jax.ShapeDtypeStruct((b, t, ROPE_DIM), F32),
              jax.ShapeDtypeStruct((b, t, LANES), BF16))
    return pl.pallas_call(
        _lat_kernel,
        grid=(b // nb, t // tt),
        in_specs=[pl.BlockSpec((nb, tt, MID_W), row),
                  pl.BlockSpec((1, Q_RANK), vec), pl.BlockSpec((1, KV_RANK), vec),
                  pl.BlockSpec((tt, LANES), lambda i, j: (j, 0)),
                  pl.BlockSpec((tt, LANES), lambda i, j: (j, 0))],
        out_specs=(pl.BlockSpec((nb, tt, Q_RANK), row), pl.BlockSpec((nb, tt, KV_RANK), row),
                   pl.BlockSpec((nb, tt, KV_RANK), row), pl.BlockSpec((nb, tt, ROPE_DIM), row),
                   pl.BlockSpec((nb, tt, LANES), row)),
        out_shape=shapes,
        compiler_params=_params("parallel", "parallel"),
        name="lat",
    )(zmid, g_q.reshape(1, Q_RANK), g_kv.reshape(1, KV_RANK), cos, sin)


def _qprep_kernel(q_ref, wuk_ref, cos_ref, sin_ref, qa_ref, qr_ref):
    nb, tt, _ = q_ref.shape
    rows = nb * tt
    nope = N_HEADS * NOPE_DIM
    for h in range(N_HEADS):
        qn = q_ref[:, :, h * NOPE_DIM:(h + 1) * NOPE_DIM].reshape(rows, NOPE_DIM).astype(BF16)
        qa = jnp.dot(qn, wuk_ref[h], preferred_element_type=F32) * ATTN_SCALE
        qa_ref[h] = qa.reshape(nb, tt, KV_RANK).astype(qa_ref.dtype)
    cos = cos_ref[...]
    sin = sin_ref[...]
    for c in range(N_HEADS * ROPE_DIM // LANES):
        x = q_ref[:, :, nope + c * LANES:nope + (c + 1) * LANES]
        rot = _rot_half(x.reshape(rows, LANES)).reshape(nb, tt, LANES)
        r = ((x * cos + rot * sin) * ATTN_SCALE).astype(qr_ref.dtype)
        qr_ref[2 * c] = r[:, :, :ROPE_DIM]
        qr_ref[2 * c + 1] = r[:, :, ROPE_DIM:]


def _qprep(q, w_ukt, cos, sin):
    b, t, qw = q.shape
    nb, tt = (1, 256) if t >= 256 else (b, t)
    return pl.pallas_call(
        _qprep_kernel,
        grid=(b // nb, t // tt),
        in_specs=[pl.BlockSpec((nb, tt, qw), lambda i, j: (i, j, 0)),
                  pl.BlockSpec((N_HEADS, NOPE_DIM, KV_RANK), lambda i, j: (0, 0, 0)),
                  pl.BlockSpec((tt, LANES), lambda i, j: (j, 0)),
                  pl.BlockSpec((tt, LANES), lambda i, j: (j, 0))],
        out_specs=(pl.BlockSpec((N_HEADS, nb, tt, KV_RANK), lambda i, j: (0, i, j, 0)),
                   pl.BlockSpec((N_HEADS, nb, tt, ROPE_DIM), lambda i, j: (0, i, j, 0))),
        out_shape=(jax.ShapeDtypeStruct((N_HEADS, b, t, KV_RANK), BF16),
                   jax.ShapeDtypeStruct((N_HEADS, b, t, ROPE_DIM), BF16)),
        compiler_params=_params("parallel", "parallel"),
        name="qprep",
    )(q, w_ukt, cos, sin)


def _qcat_kernel(q_ref, cos_ref, sin_ref, o_ref):
    tt = q_ref.shape[0]
    scale = ATTN_SCALE * LOG2E
    nope = N_HEADS * NOPE_DIM
    cos = cos_ref[...]
    sin = sin_ref[...]
    low = lax.broadcasted_iota(jnp.int32, (tt, LANES), 1) < ROPE_DIM
    for c in range(N_HEADS * ROPE_DIM // LANES):
        x = q_ref[:, nope + c * LANES:nope + (c + 1) * LANES]
        r = (x * cos + _rot_half(x) * sin) * scale
        o_ref[2 * c, :, NOPE_DIM:] = jnp.where(low, r, 0.0).astype(o_ref.dtype)
        o_ref[2 * c + 1, :, NOPE_DIM:] = jnp.where(low, pltpu.roll(r, ROPE_DIM, 1), 0.0).astype(o_ref.dtype)
    for h in range(N_HEADS):
        o_ref[h, :, :NOPE_DIM] = (q_ref[:, h * NOPE_DIM:(h + 1) * NOPE_DIM] * scale).astype(o_ref.dtype)


def _qcat(q, cos, sin):
    b, t, qw = q.shape
    tt = 256
    return pl.pallas_call(
        _qcat_kernel,
        grid=(b, t // tt),
        in_specs=[pl.BlockSpec((None, tt, qw), lambda i, j: (i, j, 0)),
                  pl.BlockSpec((tt, LANES), lambda i, j: (j, 0)),
                  pl.BlockSpec((tt, LANES), lambda i, j: (j, 0))],
        out_specs=pl.BlockSpec((None, N_HEADS, tt, QK_DIM), lambda i, j: (i, 0, j, 0)),
        out_shape=jax.ShapeDtypeStruct((b, N_HEADS, t, QK_DIM), BF16),
        compiler_params=_params("parallel", "parallel"),
        name="qcat",
    )(q, cos, sin)


def _kvcat_kernel(ckv_ref, kr_ref, wuk_ref, wuv_ref, k_ref, v_ref):
    c = ckv_ref[...]
    kn = jnp.dot(c, wuk_ref[...], preferred_element_type=F32)
    vv = jnp.dot(c, wuv_ref[...], preferred_element_type=F32)
    kr = kr_ref[...]
    one_col = (lax.broadcasted_iota(jnp.int32, kr.shape, 1) == 0).astype(v_ref.dtype)
    for h in range(N_HEADS):
        k_ref[h, :, :NOPE_DIM] = kn[:, h * NOPE_DIM:(h + 1) * NOPE_DIM].astype(k_ref.dtype)
        k_ref[h, :, NOPE_DIM:] = kr
        v_ref[h, :, :V_DIM] = vv[:, h * V_DIM:(h + 1) * V_DIM].astype(v_ref.dtype)
        v_ref[h, :, V_DIM:] = one_col


def _kvcat(ckv_b, krope_b, w_uk2, w_uv2):
    b, t, _ = ckv_b.shape
    tt = 512
    return pl.pallas_call(
        _kvcat_kernel,
        grid=(b, t // tt),
        in_specs=[pl.BlockSpec((None, tt, KV_RANK), lambda i, j: (i, j, 0)),
                  pl.BlockSpec((None, tt, LANES), lambda i, j: (i, j, 0)),
                  pl.BlockSpec((KV_RANK, N_HEADS * NOPE_DIM), lambda i, j: (0, 0)),
                  pl.BlockSpec((KV_RANK, N_HEADS * V_DIM), lambda i, j: (0, 0))],
        out_specs=(pl.BlockSpec((None, N_HEADS, tt, QK_DIM), lambda i, j: (i, 0, j, 0)),
                   pl.BlockSpec((None, N_HEADS, tt, V_AUG), lambda i, j: (i, 0, j, 0))),
        out_shape=(jax.ShapeDtypeStruct((b, N_HEADS, t, QK_DIM), BF16),
                   jax.ShapeDtypeStruct((b, N_HEADS, t, V_AUG), BF16)),
        compiler_params=_params("parallel", "parallel"),
        name="kvcat",
    )(ckv_b, krope_b, w_uk2, w_uv2)


def _mha_kernel(q_ref, k_ref, v_ref, o_ref, m_ref, acc_ref, *, tb, gh):
    i = pl.program_id(2)
    dn = (((1,), (1,)), ((), ()))
    m_ref[...] = jnp.full(m_ref.shape, NEG, F32)
    acc_ref[...] = jnp.zeros(acc_ref.shape, F32)

    def block(j, masked):
        k0 = pl.multiple_of(j * tb, tb)
        for g in range(gh):
            s = lax.dot_general(q_ref[g], k_ref[g, pl.ds(k0, tb), :], dn, preferred_element_type=F32)
            if masked:
                qc = lax.broadcasted_iota(jnp.int32, (tb, tb), 0) >> CHUNK_SHIFT
                kc = lax.broadcasted_iota(jnp.int32, (tb, tb), 1) >> CHUNK_SHIFT
                s = jnp.where(kc <= qc, s, NEG)
            m_old = m_ref[g]
            m_new = jnp.maximum(m_old, jnp.max(s, axis=-1, keepdims=True))
            alpha = jnp.exp2(m_old - m_new)
            p = jnp.exp2(s - jnp.tile(m_new, (1, tb // LANES)))
            pv = jnp.dot(p.astype(BF16), v_ref[g, pl.ds(k0, tb), :], preferred_element_type=F32)
            acc_ref[g] = jnp.tile(alpha, (1, V_AUG // LANES)) * acc_ref[g] + pv
            m_ref[g] = m_new

    def full_block(j, carry):
        block(j, False)
        return carry

    lax.fori_loop(0, i, full_block, 0)
    block(i, True)
    for g in range(gh):
        acc = acc_ref[g]
        o_ref[:, g * V_DIM:(g + 1) * V_DIM] = (acc[:, :V_DIM] / acc[:, V_DIM:V_DIM + 1]).astype(o_ref.dtype)


def _mha(qc, kc, vc):
    b, nh, t, _ = qc.shape
    tb = 512
    gh = 8
    resident = dict(pipeline_mode=pl.Buffered(1))
    return pl.pallas_call(
        functools.partial(_mha_kernel, tb=tb, gh=gh),
        grid=(b, nh // gh, t // tb),
        in_specs=[pl.BlockSpec((None, gh, tb, QK_DIM), lambda bi, hg, i: (bi, hg, i, 0)),
                  pl.BlockSpec((None, gh, t, QK_DIM), lambda bi, hg, i: (bi, hg, 0, 0), **resident),
                  pl.BlockSpec((None, gh, t, V_AUG), lambda bi, hg, i: (bi, hg, 0, 0), **resident)],
        out_specs=pl.BlockSpec((None, tb, gh * V_DIM), lambda bi, hg, i: (bi, i, hg)),
        out_shape=jax.ShapeDtypeStruct((b, t, nh * V_DIM), BF16),
        scratch_shapes=[pltpu.VMEM((gh, tb, LANES), F32), pltpu.VMEM((gh, tb, V_AUG), F32)],
        compiler_params=_params("parallel", "parallel", "parallel"),
        name="mha",
    )(qc, kc, vc)


def _attn_kernel(qa_ref, qr_ref, pk_ref, pkr_ref, nk_ref, nkr_ref, wuv_ref, o_ref, m_ref, l_ref, acc_ref,
                 *, tq, tk, n_past, past_len):
    i = pl.program_id(1)
    kk = pl.program_id(2)
    rows = N_HEADS * tq
    dn = (((1,), (1,)), ((), ()))

    @pl.when(kk == 0)
    def _():
        m_ref[...] = jnp.full(m_ref.shape, NEG, F32)
        l_ref[...] = jnp.zeros(l_ref.shape, F32)
        acc_ref[...] = jnp.zeros(acc_ref.shape, F32)

    def update(k, kr, ok):
        n = k.shape[0]
        s = lax.dot_general(qa_ref[...].reshape(rows, KV_RANK), k, dn, preferred_element_type=F32)
        s = s + lax.dot_general(qr_ref[...].reshape(rows, ROPE_DIM), kr, dn, preferred_element_type=F32)
        if ok is not None:
            s = jnp.where(ok[None], s.reshape(N_HEADS, tq, n), NEG).reshape(rows, n)
        m_old = m_ref[...]
        m_new = jnp.maximum(m_old, jnp.max(s, axis=-1, keepdims=True))
        alpha = jnp.exp(m_old - m_new)
        p = jnp.exp(s - m_new)
        l_ref[...] = alpha * l_ref[...] + jnp.sum(p, axis=-1, keepdims=True)
        acc_ref[...] = alpha * acc_ref[...] + jnp.dot(p.astype(BF16), k, preferred_element_type=F32)
        m_ref[...] = m_new

    @pl.when(kk < n_past)
    def _():
        ok = None
        if n_past * tk != past_len:
            ok = kk * tk + lax.broadcasted_iota(jnp.int32, (tq, tk), 1) < past_len
        update(pk_ref[...].astype(BF16), pkr_ref[...].astype(BF16), ok)

    @pl.when(kk == n_past)
    def _():
        n = nk_ref.shape[0]
        qpos = past_len + i * tq + lax.broadcasted_iota(jnp.int32, (tq, n), 0)
        kpos = past_len + lax.broadcasted_iota(jnp.int32, (tq, n), 1)
        update(nk_ref[...], nkr_ref[...], (kpos >> CHUNK_SHIFT) <= (qpos >> CHUNK_SHIFT))
        o_lat = (acc_ref[...] / l_ref[...]).astype(BF16).reshape(N_HEADS, tq, KV_RANK)
        for h in range(N_HEADS):
            o_h = jnp.dot(o_lat[h], wuv_ref[h], preferred_element_type=F32)
            o_ref[:, h * V_DIM:(h + 1) * V_DIM] = o_h.astype(o_ref.dtype)


def _attn(q_abs, q_rope, past_ckv, past_krope, new_ckv, new_krope, w_uv):
    _, b, t, _ = q_abs.shape
    past_len = past_ckv.shape[1]
    tq = min(t, 128)
    tk = next((c for c in (1024, 512) if past_len and past_len % c == 0), 512)
    n_past = -(-past_len // tk)
    pad = ((0, 0), (0, max(n_past, 1) * tk - past_len), (0, 0))
    past_ckv, past_krope = jnp.pad(past_ckv, pad), jnp.pad(past_krope, pad)
    past_idx = lambda bi, i, kk: (bi, jnp.minimum(kk, max(n_past - 1, 0)), 0)
    new_idx = lambda bi, i, kk: (bi, 0, 0)
    return pl.pallas_call(
        functools.partial(_attn_kernel, tq=tq, tk=tk, n_past=n_past, past_len=past_len),
        grid=(b, t // tq, n_past + 1),
        in_specs=[pl.BlockSpec((N_HEADS, None, tq, KV_RANK), lambda bi, i, kk: (0, bi, i, 0)),
                  pl.BlockSpec((N_HEADS, None, tq, ROPE_DIM), lambda bi, i, kk: (0, bi, i, 0)),
                  pl.BlockSpec((None, tk, KV_RANK), past_idx),
                  pl.BlockSpec((None, tk, ROPE_DIM), past_idx),
                  pl.BlockSpec((None, t, KV_RANK), new_idx),
                  pl.BlockSpec((None, t, ROPE_DIM), new_idx),
                  pl.BlockSpec((N_HEADS, KV_RANK, V_DIM), lambda bi, i, kk: (0, 0, 0))],
        out_specs=pl.BlockSpec((None, tq, N_HEADS * V_DIM), lambda bi, i, kk: (bi, i, 0)),
        out_shape=jax.ShapeDtypeStruct((b, t, N_HEADS * V_DIM), BF16),
        scratch_shapes=[pltpu.VMEM((N_HEADS * tq, 1), F32), pltpu.VMEM((N_HEADS * tq, 1), F32),
                        pltpu.VMEM((N_HEADS * tq, KV_RANK), F32)],
        compiler_params=_params("parallel", "parallel", "arbitrary"),
        name="attn",
    )(q_abs, q_rope, past_ckv, past_krope, new_ckv, new_krope, w_uv)


def _accumulate(acc_ref, kk, a, w_ref):
    part = jnp.dot(a, w_ref[...], preferred_element_type=F32)

    @pl.when(kk == 0)
    def _():
        acc_ref[...] = part

    @pl.when(kk > 0)
    def _():
        acc_ref[...] += part


def _norm_residual(acc_ref, x_ref, gate_ref, gn_ref, o_ref):
    nb, tt, d = x_ref.shape
    y = acc_ref[...].reshape(nb, tt, d)
    yn = y * lax.rsqrt(jnp.mean(y * y, axis=-1, keepdims=True) + EPS) * gn_ref[...]
    x_new = x_ref[...] + gate_ref[...] * yn
    o_ref[...] = x_new
    return x_new


def _merge_out_kernel(g0, g1, g2, oa, ob, oc, w_ref, x_ref, gate_ref, gn_ref, g2n_ref, sc_ref, sh_ref,
                      o_ref, h_ref, acc_ref):
    kk = pl.program_id(2)
    nb, tt, _ = x_ref.shape
    a = (g0[...].astype(F32) * oa[...].astype(F32) + g1[...].astype(F32) * ob[...].astype(F32)
         + g2[...].astype(F32) * oc[...].astype(F32)).astype(BF16)
    _accumulate(acc_ref, kk, a.reshape(nb * tt, a.shape[-1]), w_ref)

    @pl.when(kk == pl.num_programs(2) - 1)
    def _():
        x_new = _norm_residual(acc_ref, x_ref, gate_ref, gn_ref, o_ref)
        hn = x_new * lax.rsqrt(jnp.mean(x_new * x_new, axis=-1, keepdims=True) + EPS) * g2n_ref[...]
        h_ref[...] = (hn * (1.0 + sc_ref[...]) + sh_ref[...]).astype(h_ref.dtype)


def _merge_out(gates, oa, ob, oc, w, x, gate, g_norm, g_next, scale_next, shift_next):
    b, t, d = x.shape
    k = w.shape[0]
    nb, tt = _row_tiles(b, t)
    tk = 1024
    nkb = k // tk
    row = lambda i, j, kk: (i, j, kk)
    branch = lambda o, i, j, kk: (i, j, o * nkb + kk)
    full = lambda i, j, kk: (i, j, 0)
    per_batch = pl.BlockSpec((nb, 1, d), lambda i, j, kk: (i, 0, 0))
    vec = pl.BlockSpec((1, d), lambda i, j, kk: (0, 0))
    return pl.pallas_call(
        _merge_out_kernel,
        grid=(b // nb, t // tt, nkb),
        in_specs=[pl.BlockSpec((nb, tt, tk), functools.partial(branch, o)) for o in range(3)]
        + [pl.BlockSpec((nb, tt, tk), row)] * 3
        + [pl.BlockSpec((tk, d), lambda i, j, kk: (kk, 0)),
           pl.BlockSpec((nb, tt, d), full),
           per_batch, vec, vec, per_batch, per_batch],
        out_specs=(pl.BlockSpec((nb, tt, d), full), pl.BlockSpec((nb, tt, d), full)),
        out_shape=(jax.ShapeDtypeStruct((b, t, d), F32), jax.ShapeDtypeStruct((b, t, d), BF16)),
        scratch_shapes=[pltpu.VMEM((nb * tt, d), F32)],
        compiler_params=_params("parallel", "parallel", "arbitrary"),
        name="mm_out",
    )(gates, gates, gates, oa, ob, oc, w, x, gate, g_norm.reshape(1, d), g_next.reshape(1, d), scale_next, shift_next)


def _ffn_down_kernel(ug_ref, uv_ref, prev_ref, hist_ref, wd_ref, bd_ref, w_ref, x_ref, gate_ref, gn_ref, o_ref,
                     ext_ref, acc_ref):
    j = pl.program_id(1)
    kk = pl.program_id(2)
    nb, tt, _ = x_ref.shape
    ext_ref[:, 0:FFN_HALO, :] = jnp.where(j == 0, hist_ref[...], prev_ref[...].astype(F32))
    ext_ref[:, FFN_HALO:, :] = ug_ref[...].astype(F32)
    lead = FFN_HALO - (FFN_K - 1)
    tk = ext_ref.shape[-1]
    part = None
    for c0 in range(0, tk, FFN_SUB):
        c1 = min(c0 + FFN_SUB, tk)
        conv = bd_ref[:, c0:c1]
        for k in range(FFN_K):
            conv = conv + wd_ref[k:k + 1, c0:c1] * ext_ref[:, lead + k:lead + k + tt, c0:c1]
        act = (_silu(conv) * uv_ref[:, :, c0:c1].astype(F32)).astype(BF16)
        dot = jnp.dot(act.reshape(nb * tt, c1 - c0), w_ref[c0:c1, :], preferred_element_type=F32)
        part = dot if part is None else part + dot

    @pl.when(kk == 0)
    def _():
        acc_ref[...] = part

    @pl.when(kk > 0)
    def _():
        acc_ref[...] += part

    @pl.when(kk == pl.num_programs(2) - 1)
    def _():
        _norm_residual(acc_ref, x_ref, gate_ref, gn_ref, o_ref)


def _ffn_down(up, hist, w_dw, b_dw, w, x, gate, g_norm):
    b, t, d = x.shape
    nb, tt = _row_tiles(b, t)
    tk = D_FF // 4
    nkb = D_FF // tk
    hb = tt // FFN_HALO
    full = lambda i, j, kk: (i, j, 0)
    return pl.pallas_call(
        _ffn_down_kernel,
        grid=(b // nb, t // tt, nkb),
        in_specs=[pl.BlockSpec((nb, tt, tk), lambda i, j, kk: (i, j, kk)),
                  pl.BlockSpec((nb, tt, tk), lambda i, j, kk: (i, j, nkb + kk)),
                  pl.BlockSpec((nb, FFN_HALO, tk), lambda i, j, kk: (i, jnp.maximum(j * hb - 1, 0), kk)),
                  pl.BlockSpec((nb, FFN_HALO, tk), lambda i, j, kk: (i, 0, kk)),
                  pl.BlockSpec((FFN_K, tk), lambda i, j, kk: (0, kk)),
                  pl.BlockSpec((1, tk), lambda i, j, kk: (0, kk)),
                  pl.BlockSpec((tk, d), lambda i, j, kk: (kk, 0)),
                  pl.BlockSpec((nb, tt, d), full),
                  pl.BlockSpec((nb, 1, d), lambda i, j, kk: (i, 0, 0)),
                  pl.BlockSpec((1, d), lambda i, j, kk: (0, 0))],
        out_specs=pl.BlockSpec((nb, tt, d), full),
        out_shape=jax.ShapeDtypeStruct((b, t, d), F32),
        scratch_shapes=[pltpu.VMEM((nb, tt + FFN_HALO, tk), F32), pltpu.VMEM((nb * tt, d), F32)],
        compiler_params=_params("parallel", "parallel", "arbitrary"),
        name="ffn_down",
    )(up, up, up, hist, w_dw, b_dw.reshape(1, D_FF), w, x, gate, g_norm.reshape(1, d))


def _rope_tables(pos0, t):
    half = ROPE_DIM // 2
    inv = ROPE_THETA ** (-jnp.arange(half, dtype=F32) / half)
    pos = (pos0 + jnp.arange(t, dtype=jnp.int32)).astype(F32)
    ang = pos[:, None] * inv[None, :]
    cos, sin = jnp.cos(ang), jnp.sin(ang)
    return (jnp.concatenate([cos, cos, cos, cos], axis=-1),
            jnp.concatenate([-sin, sin, -sin, sin], axis=-1))


def _front_pad(a, rows):
    return jnp.pad(a, ((0, 0), (rows - a.shape[1], 0), (0, 0)))


def _tail(hist, new, n, cols):
    keep = min(new.shape[1], n)
    return jnp.concatenate([hist, new[:, new.shape[1] - keep:, :cols].astype(F32)], axis=1)[:, -n:]


def _attend(g, q, ckv_b, krope_b, cos, sin, p):
    t = q.shape[1]
    if g["past_ckv"].shape[1] == 0 and t % 512 == 0:
        kc, vc = _kvcat(ckv_b, krope_b, p["w_uk2"], p["w_uv2"])
        return _mha(_qcat(q, cos, sin), kc, vc)
    q_abs, q_rope = _qprep(q, p["w_ukt"], cos, sin)
    return _attn(q_abs, q_rope, g["past_ckv"], g["past_krope"], ckv_b, krope_b[:, :, :ROPE_DIM], p["w_uv"])


def _layer(groups, l, w, p):
    d = D_MODEL
    dims = [g["x"].shape[:2] for g in groups]
    flat = lambda arrs: [a.reshape(-1, a.shape[-1]) for a in arrs]
    unflat = lambda outs: [o.reshape(b, t, o.shape[-1]) for o, (b, t) in zip(outs, dims)]
    mods = [[g["mod"][:, i] for i in range(6)] for g in groups]
    tables = [_rope_tables(g["pos0"], t) for g, (_, t) in zip(groups, dims)]

    h = flat([_norm_mod(g["x"], p["g_pre_mix"], m[1], m[0]) for g, m in zip(groups, mods)])
    glu_tn = 512
    u_a = unflat(_mm(*h, [(p["w_a"], None, 0), (p["w_a"], None, D_CONV // glu_tn)], D_CONV, F32,
                     tn=glu_tn, epi="glu", name="mm_glu"))
    zmid = unflat(_mm(*h, [(p["w_mid"], None, 0)], MID_W, F32, tn=MID_TN, name="mm_mid"))
    gates = unflat(_mm(*h, [(p["w_g"], None, 0)], 3 * d, BF16, tn=1024, epi="sigmoid", name="mm_gates"))

    a_act = [_conv_a(u, _front_pad(g["hist_conv"], CONV_HALO), p["w_dwa"], p["b_dwa"], p["ln_a_g"], p["ln_a_b"])
             for u, g in zip(u_a, groups)]
    out_a = unflat(_mm(*flat(a_act), [(w["w_pa"], l, 0)], d, BF16, tn=512, name="mm_pa"))

    out_b = [_pool(z, _front_pad(g["hist_pool"], POOL_HALO), g["pos0"], p["w_pool"], p["pool_scale"])
             for z, g in zip(zmid, groups)]

    lat = [_lat(z, p["g_q_lat"], p["g_kv_lat"], cos, sin) for z, (cos, sin) in zip(zmid, tables)]
    q = unflat(_mm(*flat([o[0] for o in lat]), [(p["w_uq"], None, 0)], p["w_uq"].shape[1], F32, tn=512, name="mm_uq"))
    o = [_attend(g, qg, lg[2], lg[4], cos, sin, p) for g, qg, lg, (cos, sin) in zip(groups, q, lat, tables)]
    out_c = unflat(_mm(*flat(o), [(w["w_oc"], l, 0)], d, BF16, tn=512, name="mm_oc"))

    mixed = [_merge_out(gt, oa, ob, oc, p["w_out"], g["x"], m[2], p["g_post_mix"], p["g_pre_ffn"], m[4], m[3])
             for gt, oa, ob, oc, g, m in zip(gates, out_a, out_b, out_c, groups, mods)]
    x = [xh[0] for xh in mixed]

    h2 = flat([xh[1] for xh in mixed])
    up = unflat(_mm(*h2, [(w["w_up"], l, 0)], 2 * D_FF, BF16, tn=1024, name="mm_up"))
    x = [_ffn_down(u, _front_pad(g["hist_ffn"], FFN_HALO), p["w_dwf"], p["b_dwf"], p["w_down"], xg, m[5],
                   p["g_post_ffn"]) for u, g, xg, m in zip(up, groups, x, mods)]

    states = [(lg[1], lg[3],
               _tail(g["hist_conv"], u, CONV_K - 1, D_CONV),
               _tail(g["hist_pool"], z, POOL_MAX - 1, D_POOL),
               _tail(g["hist_ffn"], uu, FFN_K - 1, D_FF))
              for lg, g, u, z, uu in zip(lat, groups, u_a, zmid, up)]
    return x, states


def _prep_weights(l, w):
    w_uq = w["w_uq"][l].reshape(Q_RANK, N_HEADS, NOPE_DIM + ROPE_DIM)
    w_uq = jnp.concatenate([w_uq[:, :, :NOPE_DIM].reshape(Q_RANK, -1), w_uq[:, :, NOPE_DIM:].reshape(Q_RANK, -1)], axis=1)
    w_in = w["w_in"][l]
    p = dict(
        w_a=w_in[:, :OFF_B].astype(BF16),
        w_mid=jnp.pad(w_in[:, OFF_B:OFF_G], ((0, 0), (0, MID_W - (OFF_G - OFF_B)))).astype(BF16),
        w_g=w_in[:, OFF_G:].astype(BF16),
        w_pool=w["w_pool"][l].astype(BF16),
        w_uq=w_uq.astype(BF16),
        w_ukt=jnp.transpose(w["w_uk"][l], (1, 2, 0)).astype(BF16),
        w_uv=jnp.transpose(w["w_uv"][l], (1, 0, 2)).astype(BF16),
        w_uk2=w["w_uk"][l].reshape(KV_RANK, N_HEADS * NOPE_DIM).astype(BF16),
        w_uv2=w["w_uv"][l].reshape(KV_RANK, N_HEADS * V_DIM).astype(BF16),
        w_out=w["w_out"][l].astype(BF16), w_down=w["w_down"][l].astype(BF16),
    )
    for name in ("g_pre_mix", "g_post_mix", "w_dwa", "b_dwa", "ln_a_g", "ln_a_b", "pool_scale", "g_q_lat",
                 "g_kv_lat", "g_pre_ffn", "g_post_ffn", "w_dwf", "b_dwf"):
        p[name] = w[name][l]
    return p


def kernel(x_prompt, x_sample, cache_ckv, cache_krope, state_conv, state_pool, state_ffn, c_prompt, c_sample, w_mod, b_mod, g_pre_mix, g_post_mix, w_in, w_dwa, b_dwa, ln_a_g, ln_a_b, w_pa, w_pool, pool_scale, g_q_lat, g_kv_lat, w_uq, w_uk, w_uv, w_oc, w_out, g_pre_ffn, g_post_ffn, w_up, w_dwf, b_dwf, w_down):
    weights = dict(g_pre_mix=g_pre_mix, g_post_mix=g_post_mix, w_in=w_in, w_dwa=w_dwa, b_dwa=b_dwa,
                   ln_a_g=ln_a_g, ln_a_b=ln_a_b, w_pa=w_pa, w_pool=w_pool, pool_scale=pool_scale,
                   g_q_lat=g_q_lat, g_kv_lat=g_kv_lat, w_uq=w_uq, w_uk=w_uk, w_uv=w_uv, w_oc=w_oc,
                   w_out=w_out, g_pre_ffn=g_pre_ffn, g_post_ffn=g_post_ffn, w_up=w_up, w_dwf=w_dwf,
                   b_dwf=b_dwf, w_down=w_down)
    depth = w_mod.shape[0]
    bp, bs = x_prompt.shape[0], x_sample.shape[0]
    past_len = cache_ckv.shape[2]
    d = x_prompt.shape[-1]

    rows = -(-(bp + bs) // SUBLANES) * SUBLANES
    c_all = jnp.pad(jnp.concatenate([c_prompt, c_sample], axis=0), ((0, rows - bp - bs), (0, 0)))
    mod_all = _mod(c_all, w_mod, b_mod)

    xp, xs = x_prompt, x_sample
    st_p = [[] for _ in range(5)]
    st_s = [[] for _ in range(5)]
    for l in range(depth):
        p = _prep_weights(l, weights)
        prompt = dict(x=xp, mod=mod_all[l, :bp].reshape(bp, 6, 1, d), pos0=0,
                      past_ckv=jnp.zeros((bp, 0, KV_RANK), F32), past_krope=jnp.zeros((bp, 0, ROPE_DIM), F32),
                      hist_conv=jnp.zeros((bp, CONV_K - 1, D_CONV), F32),
                      hist_pool=jnp.zeros((bp, POOL_MAX - 1, D_POOL), F32),
                      hist_ffn=jnp.zeros((bp, FFN_K - 1, D_FF), F32))
        sample = dict(x=xs, mod=mod_all[l, bp:bp + bs].reshape(bs, 6, 1, d), pos0=past_len,
                      past_ckv=cache_ckv[l], past_krope=cache_krope[l], hist_conv=state_conv[l],
                      hist_pool=state_pool[l], hist_ffn=state_ffn[l])
        (xp, xs), (sp, ss) = _layer((prompt, sample), l, weights, p)
        for i in range(5):
            st_p[i].append(sp[i])
            st_s[i].append(ss[i])
    return (xp, xs) + tuple(jnp.stack(s) for s in st_p) + tuple(jnp.stack(s) for s in st_s)
```

```python
import functools

import jax
import jax.numpy as jnp
from jax import lax
from jax.experimental import pallas as pl
from jax.experimental.pallas import tpu as pltpu

F32 = jnp.float32
BF16 = jnp.bfloat16

D_MODEL = 2048
CHUNK = 64
CHUNK_SHIFT = 6
assert 1 << CHUNK_SHIFT == CHUNK
D_CONV = D_MODEL // 2
CONV_K = 31
D_POOL = D_MODEL // 2
POOL_WINDOWS = (2, 4, 8, 16)
POOL_MAX = 16
N_POOL_GROUPS = 4
POOL_GROUP = D_POOL // N_POOL_GROUPS
POOL_OUT = D_MODEL // N_POOL_GROUPS
N_HEADS = D_MODEL // 128
NOPE_DIM = 128
ROPE_DIM = 64
V_DIM = 128
Q_RANK = D_MODEL // 4
KV_RANK = D_MODEL // 4
ROPE_THETA = 10000.0
ATTN_SCALE = (NOPE_DIM + ROPE_DIM) ** -0.5
LOG2E = 1.4426950408889634
QK_DIM = 256
V_AUG = 256
D_FF = 256 * ((8 * D_MODEL // 3 + 255) // 256)
FFN_K = 3
EPS = 1e-6
NEG = -1e30
OFF_B = 2 * D_CONV
OFF_R = OFF_B + D_POOL + Q_RANK + KV_RANK
OFF_G = OFF_R + ROPE_DIM

LANES = 128
SUBLANES = 8
VMEM_LIMIT_BYTES = 56 * 1024 * 1024

MID_Q = D_POOL
MID_KV = MID_Q + Q_RANK
MID_R = MID_KV + KV_RANK
MID_W = MID_R + 2 * LANES
MID_TN = MID_W // 3
CONV_HALO = 32
POOL_HALO = 16
FFN_HALO = 8
FFN_SUB = 512


def _params(*sem):
    return pltpu.CompilerParams(dimension_semantics=sem, vmem_limit_bytes=VMEM_LIMIT_BYTES)


def _sigmoid(x):
    return 0.5 + 0.5 * jnp.tanh(0.5 * x)


def _silu(x):
    h = 0.5 * x
    return h + h * jnp.tanh(h)


def _row_tiles(b, t):
    if t >= 512:
        return 1, 512
    return b, t


def _mod_kernel(c_ref, w_ref, b_ref, o_ref):
    c = c_ref[...]
    a = _silu(c).astype(BF16)
    o_ref[...] = jnp.dot(a, w_ref[...].astype(BF16), preferred_element_type=F32) + b_ref[...]


def _mod(c_all, w_mod, b_mod):
    nl, d, n = w_mod.shape
    bp = c_all.shape[0]
    tn = 1024
    return pl.pallas_call(
        _mod_kernel,
        grid=(nl, n // tn),
        in_specs=[pl.BlockSpec((bp, d), lambda l, j: (0, 0)),
                  pl.BlockSpec((None, d, tn), lambda l, j: (l, 0, j)),
                  pl.BlockSpec((None, 1, tn), lambda l, j: (l, 0, j))],
        out_specs=pl.BlockSpec((None, bp, tn), lambda l, j: (l, 0, j)),
        out_shape=jax.ShapeDtypeStruct((nl, bp, n), F32),
        compiler_params=_params("parallel", "parallel"),
        name="mod",
    )(c_all, w_mod, b_mod.reshape(nl, 1, n))


def _norm_mod_kernel(x_ref, g_ref, sc_ref, sh_ref, o_ref):
    x = x_ref[...]
    y = x * lax.rsqrt(jnp.mean(x * x, axis=-1, keepdims=True) + EPS) * g_ref[...]
    o_ref[...] = (y * (1.0 + sc_ref[...]) + sh_ref[...]).astype(o_ref.dtype)


def _norm_mod(x, g, scale, shift):
    b, t, d = x.shape
    nb, tt = _row_tiles(b, t)
    return pl.pallas_call(
        _norm_mod_kernel,
        grid=(b // nb, t // tt),
        in_specs=[pl.BlockSpec((nb, tt, d), lambda i, j: (i, j, 0)),
                  pl.BlockSpec((1, d), lambda i, j: (0, 0)),
                  pl.BlockSpec((nb, 1, d), lambda i, j: (i, 0, 0)),
                  pl.BlockSpec((nb, 1, d), lambda i, j: (i, 0, 0))],
        out_specs=pl.BlockSpec((nb, tt, d), lambda i, j: (i, j, 0)),
        out_shape=jax.ShapeDtypeStruct((b, t, d), BF16),
        compiler_params=_params("parallel", "parallel"),
        name="norm_mod",
    )(x, g.reshape(1, d), scale, shift)


def _mm_kernel(*refs, epi, n_w, n_e):
    ap_ref, as_ref = refs[0], refs[1]
    w_refs = refs[2:2 + n_w]
    e_refs = refs[2 + n_w:2 + n_w + 2 * n_e]
    op_ref, os_ref = refs[2 + n_w + 2 * n_e], refs[3 + n_w + 2 * n_e]
    wb_refs = refs[4 + n_w + 2 * n_e:]
    m = pl.program_id(1)

    @pl.when(m == 0)
    def _():
        for w_ref, wb_ref in zip(w_refs, wb_refs):
            wb_ref[...] = w_ref[...].astype(BF16)

    def compute(a, es):
        z = jnp.dot(a, wb_refs[0][...], preferred_element_type=F32)
        if epi == "sigmoid":
            z = _sigmoid(z)
        elif epi == "glu":
            z = z * _sigmoid(jnp.dot(a, wb_refs[1][...], preferred_element_type=F32))
        elif epi == "mul":
            z = z * es[0][...].astype(F32)
        elif epi == "mul_add2":
            z = z * es[0][...].astype(F32) + es[1][...].astype(F32) + es[2][...].astype(F32)
        return z

    @pl.when(m == 0)
    def _():
        os_ref[...] = compute(as_ref[...], e_refs[1::2]).astype(os_ref.dtype)

    @pl.when(m > 0)
    def _():
        op_ref[...] = compute(ap_ref[...], e_refs[0::2]).astype(op_ref.dtype)


def _mm(a_p, a_s, ws, n, out_dtype, *, tn, epi="none", extras=(), name="mm"):
    rp, k = a_p.shape
    rs = a_s.shape[0]
    tm = min(rp, 1024)
    npt = rp // tm

    def w_spec(arr, lead, first):
        if lead is None:
            return pl.BlockSpec((k, tn), lambda j, m: (0, first + j))
        return pl.BlockSpec((None, k, tn), lambda j, m: (lead, 0, first + j))

    def e_specs(first):
        return [pl.BlockSpec((tm, tn), lambda j, m: (jnp.maximum(m - 1, 0), first + j)),
                pl.BlockSpec((rs, tn), lambda j, m: (0, first + j))]

    p_row = lambda j, m: (jnp.maximum(m - 1, 0), 0)
    p_out = lambda j, m: (jnp.maximum(m - 1, 0), j)
    e_in = [arr for e in extras for arr in e[:2]]
    return pl.pallas_call(
        functools.partial(_mm_kernel, epi=epi, n_w=len(ws), n_e=len(extras)),
        grid=(n // tn, npt + 1),
        in_specs=[pl.BlockSpec((tm, k), p_row), pl.BlockSpec((rs, k), lambda j, m: (0, 0))]
        + [w_spec(*w) for w in ws] + [s for e in extras for s in e_specs(e[2])],
        out_specs=(pl.BlockSpec((tm, tn), p_out), pl.BlockSpec((rs, tn), lambda j, m: (0, j))),
        out_shape=(jax.ShapeDtypeStruct((rp, n), out_dtype), jax.ShapeDtypeStruct((rs, n), out_dtype)),
        scratch_shapes=[pltpu.VMEM((k, tn), BF16) for _ in ws],
        compiler_params=_params("parallel", "arbitrary"),
        name=name,
    )(a_p, a_s, *[w[0] for w in ws], *e_in)


def _conv_a_kernel(u_ref, prev_ref, hist_ref, w_ref, b_ref, g_ref, be_ref, o_ref, ext_ref, sh_ref, a_ref, *, tt):
    j = pl.program_id(1)
    ext_ref[0:CONV_HALO, :] = jnp.where(j == 0, hist_ref[...], prev_ref[...])
    ext_ref[CONV_HALO:, :] = u_ref[...]
    sh_rows = sh_ref.shape[1]
    for s in range(1, SUBLANES):
        sh_ref[s - 1] = ext_ref[s:s + sh_rows, :]
    rc = 32
    cc = 512
    lead = CONV_HALO - (CONV_K - 1)

    def body(r, carry):
        r0 = pl.multiple_of(r * rc, rc)
        for c0 in range(0, D_CONV, cc):
            acc = jnp.zeros((rc, cc), F32)
            for k in range(CONV_K):
                s = (lead + k) % SUBLANES
                row = pl.multiple_of(r0 + (lead + k - s), SUBLANES)
                if s == 0:
                    x = ext_ref[pl.ds(row, rc), c0:c0 + cc]
                else:
                    x = sh_ref[s - 1, pl.ds(row, rc), c0:c0 + cc]
                acc = acc + w_ref[k:k + 1, c0:c0 + cc] * x
            a_ref[pl.ds(r0, rc), c0:c0 + cc] = acc
        return carry

    lax.fori_loop(0, tt // rc, body, 0)
    a = a_ref[...] + b_ref[...]
    mu = jnp.mean(a, axis=-1, keepdims=True)
    ac = a - mu
    var = jnp.mean(ac * ac, axis=-1, keepdims=True)
    y = ac * lax.rsqrt(var + EPS) * g_ref[...] + be_ref[...]
    o_ref[...] = _silu(y).astype(o_ref.dtype)


def _conv_a(u, hist, w_dw, b_dw, ln_g, ln_b):
    b, t, c = u.shape
    tt = min(t, 256)
    hb = tt // CONV_HALO
    vec = lambda i, j: (0, 0)
    return pl.pallas_call(
        functools.partial(_conv_a_kernel, tt=tt),
        grid=(b, t // tt),
        in_specs=[pl.BlockSpec((None, tt, c), lambda i, j: (i, j, 0)),
                  pl.BlockSpec((None, CONV_HALO, c), lambda i, j: (i, jnp.maximum(j * hb - 1, 0), 0)),
                  pl.BlockSpec((None, CONV_HALO, c), lambda i, j: (i, 0, 0)),
                  pl.BlockSpec((CONV_K, c), vec),
                  pl.BlockSpec((1, c), vec), pl.BlockSpec((1, c), vec), pl.BlockSpec((1, c), vec)],
        out_specs=pl.BlockSpec((None, tt, c), lambda i, j: (i, j, 0)),
        out_shape=jax.ShapeDtypeStruct((b, t, c), BF16),
        scratch_shapes=[pltpu.VMEM((tt + CONV_HALO, c), F32),
                        pltpu.VMEM((SUBLANES - 1, tt + CONV_HALO - SUBLANES, c), F32),
                        pltpu.VMEM((tt, c), F32)],
        compiler_params=_params("parallel", "parallel"),
        name="conv_a",
    )(u, u, hist, w_dw, b_dw.reshape(1, c), ln_g.reshape(1, c), ln_b.reshape(1, c))


def _pool_kernel(z_ref, prev_ref, hist_ref, w_ref, s_ref, gate_ref, o_ref, ext_ref, *, tt, pos0):
    j = pl.program_id(1)
    ext_ref[0:POOL_HALO, :] = jnp.where(j == 0, hist_ref[...], prev_ref[...])
    ext_ref[POOL_HALO:, :] = z_ref[...]
    pos = pos0 + j * tt + lax.broadcasted_iota(jnp.int32, (tt, 1), 0)
    for g, w in enumerate(POOL_WINDOWS):
        c0 = g * POOL_GROUP
        cur = ext_ref[:, c0:c0 + POOL_GROUP]
        sh = 1
        while sh < w:
            cur = cur + pltpu.roll(cur, sh, 0)
            sh *= 2
        win = cur[POOL_HALO:, :]
        cnt = jnp.minimum(w, pos + 1).astype(F32)
        m = (win / cnt - ext_ref[POOL_HALO:, c0:c0 + POOL_GROUP]).astype(BF16)
        cols = slice(g * POOL_OUT, (g + 1) * POOL_OUT)
        out = jnp.dot(m, w_ref[g], preferred_element_type=F32) * s_ref[:, cols]
        o_ref[:, cols] = (out * gate_ref[:, cols].astype(F32)).astype(o_ref.dtype)


def _pool(zmid, hist, pos0, w_pool, l, pool_scale, gates):
    b, t, _ = zmid.shape
    c = D_POOL
    tt = min(t, 512)
    hb = tt // POOL_HALO
    return pl.pallas_call(
        functools.partial(_pool_kernel, tt=tt, pos0=pos0),
        grid=(b, t // tt),
        in_specs=[pl.BlockSpec((None, tt, c), lambda i, j: (i, j, 0)),
                  pl.BlockSpec((None, POOL_HALO, c), lambda i, j: (i, jnp.maximum(j * hb - 1, 0), 0)),
                  pl.BlockSpec((None, POOL_HALO, c), lambda i, j: (i, 0, 0)),
                  pl.BlockSpec((None, N_POOL_GROUPS, POOL_GROUP, POOL_OUT), lambda i, j: (l, 0, 0, 0)),
                  pl.BlockSpec((1, D_MODEL), lambda i, j: (0, 0)),
                  pl.BlockSpec((None, tt, D_MODEL), lambda i, j: (i, j, 1))],
        out_specs=pl.BlockSpec((None, tt, D_MODEL), lambda i, j: (i, j, 0)),
        out_shape=jax.ShapeDtypeStruct((b, t, D_MODEL), BF16),
        scratch_shapes=[pltpu.VMEM((tt + POOL_HALO, c), F32)],
        compiler_params=_params("parallel", "parallel"),
        name="pool",
    )(zmid, zmid, hist, w_pool, pool_scale.reshape(1, D_MODEL), gates)


def _rot_half(x):
    lane = lax.broadcasted_iota(jnp.int32, x.shape, 1)
    first = (lane % ROPE_DIM) < (ROPE_DIM // 2)
    return jnp.where(first, pltpu.roll(x, LANES - ROPE_DIM // 2, 1), pltpu.roll(x, ROPE_DIM // 2, 1))


def _lat_kernel(z_ref, gq_ref, gkv_ref, cos_ref, sin_ref, ql_ref, ckv_ref, ckvb_ref, kr_ref, krb_ref):
    nb, tt, _ = z_ref.shape
    zq = z_ref[:, :, MID_Q:MID_KV]
    ql = zq * lax.rsqrt(jnp.mean(zq * zq, axis=-1, keepdims=True) + EPS) * gq_ref[...]
    ql_ref[...] = ql.astype(ql_ref.dtype)
    zkv = z_ref[:, :, MID_KV:MID_R]
    ckv = zkv * lax.rsqrt(jnp.mean(zkv * zkv, axis=-1, keepdims=True) + EPS) * gkv_ref[...]
    ckv_ref[...] = ckv
    ckvb_ref[...] = ckv.astype(BF16)
    zr = z_ref[:, :, MID_R:MID_R + LANES]
    rot = _rot_half(zr.reshape(nb * tt, LANES)).reshape(nb, tt, LANES)
    kr = zr * cos_ref[...] + rot * sin_ref[...]
    kr_ref[...] = kr[:, :, :ROPE_DIM]
    krb_ref[...] = kr.astype(BF16)


def _lat(zmid, g_q, g_kv, cos, sin):
    b, t, _ = zmid.shape
    nb, tt = _row_tiles(b, t)
    row = lambda i, j: (i, j, 0)
    vec = lambda i, j: (0, 0)
    shapes = (jax.ShapeDtypeStruct((b, t, Q_RANK), BF16),
              jax.ShapeDtypeStruct((b, t, KV_RANK), F32),
              jax.ShapeDtypeStruct((b, t, KV_RANK), BF16),
              jax.ShapeDtypeStruct((b, t, ROPE_DIM), F32),
              jax.ShapeDtypeStruct((b, t, LANES), BF16))
    return pl.pallas_call(
        _lat_kernel,
        grid=(b // nb, t // tt),
        in_specs=[pl.BlockSpec((nb, tt, MID_W), row),
                  pl.BlockSpec((1, Q_RANK), vec), pl.BlockSpec((1, KV_RANK), vec),
                  pl.BlockSpec((tt, LANES), lambda i, j: (j, 0)),
                  pl.BlockSpec((tt, LANES), lambda i, j: (j, 0))],
        out_specs=(pl.BlockSpec((nb, tt, Q_RANK), row), pl.BlockSpec((nb, tt, KV_RANK), row),
                   pl.BlockSpec((nb, tt, KV_RANK), row), pl.BlockSpec((nb, tt, ROPE_DIM), row),
                   pl.BlockSpec((nb, tt, LANES), row)),
        out_shape=shapes,
        compiler_params=_params("parallel", "parallel"),
        name="lat",
    )(zmid, g_q.reshape(1, Q_RANK), g_kv.reshape(1, KV_RANK), cos, sin)


def _qprep_kernel(q_ref, wuk_ref, cos_ref, sin_ref, qa_ref, qr_ref):
    nb, tt, _ = q_ref.shape
    rows = nb * tt
    nope = N_HEADS * NOPE_DIM
    for h in range(N_HEADS):
        qn = q_ref[:, :, h * NOPE_DIM:(h + 1) * NOPE_DIM].reshape(rows, NOPE_DIM).astype(BF16)
        qa = jnp.dot(qn, wuk_ref[h], preferred_element_type=F32) * ATTN_SCALE
        qa_ref[h] = qa.reshape(nb, tt, KV_RANK).astype(qa_ref.dtype)
    cos = cos_ref[...]
    sin = sin_ref[...]
    for c in range(N_HEADS * ROPE_DIM // LANES):
        x = q_ref[:, :, nope + c * LANES:nope + (c + 1) * LANES]
        rot = _rot_half(x.reshape(rows, LANES)).reshape(nb, tt, LANES)
        r = ((x * cos + rot * sin) * ATTN_SCALE).astype(qr_ref.dtype)
        qr_ref[2 * c] = r[:, :, :ROPE_DIM]
        qr_ref[2 * c + 1] = r[:, :, ROPE_DIM:]


def _qprep(q, w_ukt, l, cos, sin):
    b, t, qw = q.shape
    nb, tt = (1, 256) if t >= 256 else (b, t)
    return pl.pallas_call(
        _qprep_kernel,
        grid=(b // nb, t // tt),
        in_specs=[pl.BlockSpec((nb, tt, qw), lambda i, j: (i, j, 0)),
                  pl.BlockSpec((None, N_HEADS, NOPE_DIM, KV_RANK), lambda i, j: (l, 0, 0, 0)),
                  pl.BlockSpec((tt, LANES), lambda i, j: (j, 0)),
                  pl.BlockSpec((tt, LANES), lambda i, j: (j, 0))],
        out_specs=(pl.BlockSpec((N_HEADS, nb, tt, KV_RANK), lambda i, j: (0, i, j, 0)),
                   pl.BlockSpec((N_HEADS, nb, tt, ROPE_DIM), lambda i, j: (0, i, j, 0))),
        out_shape=(jax.ShapeDtypeStruct((N_HEADS, b, t, KV_RANK), BF16),
                   jax.ShapeDtypeStruct((N_HEADS, b, t, ROPE_DIM), BF16)),
        compiler_params=_params("parallel", "parallel"),
        name="qprep",
    )(q, w_ukt, cos, sin)


def _qcat_kernel(q_ref, cos_ref, sin_ref, o_ref):
    tt = q_ref.shape[0]
    scale = ATTN_SCALE * LOG2E
    nope = N_HEADS * NOPE_DIM
    cos = cos_ref[...]
    sin = sin_ref[...]
    low = lax.broadcasted_iota(jnp.int32, (tt, LANES), 1) < ROPE_DIM
    for c in range(N_HEADS * ROPE_DIM // LANES):
        x = q_ref[:, nope + c * LANES:nope + (c + 1) * LANES]
        r = (x * cos + _rot_half(x) * sin) * scale
        o_ref[2 * c, :, NOPE_DIM:] = jnp.where(low, r, 0.0).astype(o_ref.dtype)
        o_ref[2 * c + 1, :, NOPE_DIM:] = jnp.where(low, pltpu.roll(r, ROPE_DIM, 1), 0.0).astype(o_ref.dtype)
    for h in range(N_HEADS):
        o_ref[h, :, :NOPE_DIM] = (q_ref[:, h * NOPE_DIM:(h + 1) * NOPE_DIM] * scale).astype(o_ref.dtype)


def _qcat(q, cos, sin):
    b, t, qw = q.shape
    tt = 256
    return pl.pallas_call(
        _qcat_kernel,
        grid=(b, t // tt),
        in_specs=[pl.BlockSpec((None, tt, qw), lambda i, j: (i, j, 0)),
                  pl.BlockSpec((tt, LANES), lambda i, j: (j, 0)),
                  pl.BlockSpec((tt, LANES), lambda i, j: (j, 0))],
        out_specs=pl.BlockSpec((None, N_HEADS, tt, QK_DIM), lambda i, j: (i, 0, j, 0)),
        out_shape=jax.ShapeDtypeStruct((b, N_HEADS, t, QK_DIM), BF16),
        compiler_params=_params("parallel", "parallel"),
        name="qcat",
    )(q, cos, sin)


def _kvcat_kernel(ckv_ref, kr_ref, wuk_ref, wuv_ref, k_ref, v_ref):
    c = ckv_ref[...]
    kn = jnp.dot(c, wuk_ref[...], preferred_element_type=F32)
    vv = jnp.dot(c, wuv_ref[...], preferred_element_type=F32)
    kr = kr_ref[...]
    one_col = (lax.broadcasted_iota(jnp.int32, kr.shape, 1) == 0).astype(v_ref.dtype)
    for h in range(N_HEADS):
        k_ref[h, :, :NOPE_DIM] = kn[:, h * NOPE_DIM:(h + 1) * NOPE_DIM].astype(k_ref.dtype)
        k_ref[h, :, NOPE_DIM:] = kr
        v_ref[h, :, :V_DIM] = vv[:, h * V_DIM:(h + 1) * V_DIM].astype(v_ref.dtype)
        v_ref[h, :, V_DIM:] = one_col


def _kvcat(ckv_b, krope_b, w_uk2, w_uv2, l):
    b, t, _ = ckv_b.shape
    tt = 512
    return pl.pallas_call(
        _kvcat_kernel,
        grid=(b, t // tt),
        in_specs=[pl.BlockSpec((None, tt, KV_RANK), lambda i, j: (i, j, 0)),
                  pl.BlockSpec((None, tt, LANES), lambda i, j: (i, j, 0)),
                  pl.BlockSpec((None, KV_RANK, N_HEADS * NOPE_DIM), lambda i, j: (l, 0, 0)),
                  pl.BlockSpec((None, KV_RANK, N_HEADS * V_DIM), lambda i, j: (l, 0, 0))],
        out_specs=(pl.BlockSpec((None, N_HEADS, tt, QK_DIM), lambda i, j: (i, 0, j, 0)),
                   pl.BlockSpec((None, N_HEADS, tt, V_AUG), lambda i, j: (i, 0, j, 0))),
        out_shape=(jax.ShapeDtypeStruct((b, N_HEADS, t, QK_DIM), BF16),
                   jax.ShapeDtypeStruct((b, N_HEADS, t, V_AUG), BF16)),
        compiler_params=_params("parallel", "parallel"),
        name="kvcat",
    )(ckv_b, krope_b, w_uk2, w_uv2)


def _mha_kernel(q_ref, k_ref, v_ref, o_ref, m_ref, acc_ref, *, tb, gh):
    i = pl.program_id(2)
    dn = (((1,), (1,)), ((), ()))
    m_ref[...] = jnp.full(m_ref.shape, NEG, F32)
    acc_ref[...] = jnp.zeros(acc_ref.shape, F32)

    def block(j, masked):
        k0 = pl.multiple_of(j * tb, tb)
        if masked:
            qc = lax.broadcasted_iota(jnp.int32, (tb, tb), 0) >> CHUNK_SHIFT
            kc = lax.broadcasted_iota(jnp.int32, (tb, tb), 1) >> CHUNK_SHIFT
            bias = jnp.where(kc <= qc, 0.0, NEG)
        for g in range(gh):
            s = lax.dot_general(q_ref[g], k_ref[g, pl.ds(k0, tb), :], dn, preferred_element_type=F32)
            if masked:
                s = s + bias
            m_old = m_ref[g]
            m_new = jnp.maximum(m_old, jnp.max(s, axis=-1, keepdims=True))
            alpha = jnp.exp2(m_old - m_new)
            p = jnp.exp2(s - jnp.tile(m_new, (1, tb // LANES)))
            pv = jnp.dot(p.astype(BF16), v_ref[g, pl.ds(k0, tb), :], preferred_element_type=F32)
            acc_ref[g] = jnp.tile(alpha, (1, V_AUG // LANES)) * acc_ref[g] + pv
            m_ref[g] = m_new

    def full_block(j, carry):
        block(j, False)
        return carry

    lax.fori_loop(0, i, full_block, 0)
    block(i, True)
    for g in range(gh):
        acc = acc_ref[g]
        o_ref[:, g * V_DIM:(g + 1) * V_DIM] = (acc[:, :V_DIM] / acc[:, V_DIM:V_DIM + 1]).astype(o_ref.dtype)


def _mha(qc, kc, vc):
    b, nh, t, _ = qc.shape
    tb = 512
    gh = 8
    resident = dict(pipeline_mode=pl.Buffered(1))
    return pl.pallas_call(
        functools.partial(_mha_kernel, tb=tb, gh=gh),
        grid=(b, nh // gh, t // tb),
        in_specs=[pl.BlockSpec((None, gh, tb, QK_DIM), lambda bi, hg, i: (bi, hg, i, 0)),
                  pl.BlockSpec((None, gh, t, QK_DIM), lambda bi, hg, i: (bi, hg, 0, 0), **resident),
                  pl.BlockSpec((None, gh, t, V_AUG), lambda bi, hg, i: (bi, hg, 0, 0), **resident)],
        out_specs=pl.BlockSpec((None, tb, gh * V_DIM), lambda bi, hg, i: (bi, i, hg)),
        out_shape=jax.ShapeDtypeStruct((b, t, nh * V_DIM), BF16),
        scratch_shapes=[pltpu.VMEM((gh, tb, LANES), F32), pltpu.VMEM((gh, tb, V_AUG), F32)],
        compiler_params=_params("parallel", "parallel", "parallel"),
        name="mha",
    )(qc, kc, vc)


def _attn_kernel(qa_ref, qr_ref, pk_ref, pkr_ref, nk_ref, nkr_ref, wuv_ref, o_ref, m_ref, l_ref, acc_ref,
                 *, tq, tk, n_past, past_len):
    i = pl.program_id(1)
    kk = pl.program_id(2)
    rows = N_HEADS * tq
    dn = (((1,), (1,)), ((), ()))

    @pl.when(kk == 0)
    def _():
        m_ref[...] = jnp.full(m_ref.shape, NEG, F32)
        l_ref[...] = jnp.zeros(l_ref.shape, F32)
        acc_ref[...] = jnp.zeros(acc_ref.shape, F32)

    def update(k, kr, ok):
        n = k.shape[0]
        s = lax.dot_general(qa_ref[...].reshape(rows, KV_RANK), k, dn, preferred_element_type=F32)
        s = s + lax.dot_general(qr_ref[...].reshape(rows, ROPE_DIM), kr, dn, preferred_element_type=F32)
        if ok is not None:
            s = jnp.where(ok[None], s.reshape(N_HEADS, tq, n), NEG).reshape(rows, n)
        m_old = m_ref[...]
        m_new = jnp.maximum(m_old, jnp.max(s, axis=-1, keepdims=True))
        alpha = jnp.exp(m_old - m_new)
        p = jnp.exp(s - m_new)
        l_ref[...] = alpha * l_ref[...] + jnp.sum(p, axis=-1, keepdims=True)
        acc_ref[...] = alpha * acc_ref[...] + jnp.dot(p.astype(BF16), k, preferred_element_type=F32)
        m_ref[...] = m_new

    @pl.when(kk < n_past)
    def _():
        ok = None
        if n_past * tk != past_len:
            ok = kk * tk + lax.broadcasted_iota(jnp.int32, (tq, tk), 1) < past_len
        update(pk_ref[...].astype(BF16), pkr_ref[...].astype(BF16), ok)

    @pl.when(kk == n_past)
    def _():
        n = nk_ref.shape[0]
        qpos = past_len + i * tq + lax.broadcasted_iota(jnp.int32, (tq, n), 0)
        kpos = past_len + lax.broadcasted_iota(jnp.int32, (tq, n), 1)
        update(nk_ref[...], nkr_ref[...], (kpos >> CHUNK_SHIFT) <= (qpos >> CHUNK_SHIFT))
        o_lat = (acc_ref[...] / l_ref[...]).astype(BF16).reshape(N_HEADS, tq, KV_RANK)
        for h in range(N_HEADS):
            o_h = jnp.dot(o_lat[h], wuv_ref[h], preferred_element_type=F32)
            o_ref[:, h * V_DIM:(h + 1) * V_DIM] = o_h.astype(o_ref.dtype)


def _attn(q_abs, q_rope, past_ckv, past_krope, l, new_ckv, new_krope, w_uv):
    _, b, t, _ = q_abs.shape
    past_len = past_ckv.shape[2]
    tq = min(t, 128)
    tk = next((c for c in (1024, 512) if past_len and past_len % c == 0), 512)
    n_past = -(-past_len // tk)
    pad = ((0, 0), (0, 0), (0, max(n_past, 1) * tk - past_len), (0, 0))
    past_ckv, past_krope = jnp.pad(past_ckv, pad), jnp.pad(past_krope, pad)
    past_idx = lambda bi, i, kk: (l, bi, jnp.minimum(kk, max(n_past - 1, 0)), 0)
    new_idx = lambda bi, i, kk: (bi, 0, 0)
    return pl.pallas_call(
        functools.partial(_attn_kernel, tq=tq, tk=tk, n_past=n_past, past_len=past_len),
        grid=(b, t // tq, n_past + 1),
        in_specs=[pl.BlockSpec((N_HEADS, None, tq, KV_RANK), lambda bi, i, kk: (0, bi, i, 0)),
                  pl.BlockSpec((N_HEADS, None, tq, ROPE_DIM), lambda bi, i, kk: (0, bi, i, 0)),
                  pl.BlockSpec((None, None, tk, KV_RANK), past_idx),
                  pl.BlockSpec((None, None, tk, ROPE_DIM), past_idx),
                  pl.BlockSpec((None, t, KV_RANK), new_idx),
                  pl.BlockSpec((None, t, ROPE_DIM), new_idx),
                  pl.BlockSpec((None, N_HEADS, KV_RANK, V_DIM), lambda bi, i, kk: (l, 0, 0, 0))],
        out_specs=pl.BlockSpec((None, tq, N_HEADS * V_DIM), lambda bi, i, kk: (bi, i, 0)),
        out_shape=jax.ShapeDtypeStruct((b, t, N_HEADS * V_DIM), BF16),
        scratch_shapes=[pltpu.VMEM((N_HEADS * tq, 1), F32), pltpu.VMEM((N_HEADS * tq, 1), F32),
                        pltpu.VMEM((N_HEADS * tq, KV_RANK), F32)],
        compiler_params=_params("parallel", "parallel", "arbitrary"),
        name="attn",
    )(q_abs, q_rope, past_ckv, past_krope, new_ckv, new_krope, w_uv)


def _accumulate(acc_ref, kk, a, w_ref):
    part = jnp.dot(a, w_ref[...], preferred_element_type=F32)

    @pl.when(kk == 0)
    def _():
        acc_ref[...] = part

    @pl.when(kk > 0)
    def _():
        acc_ref[...] += part


def _norm_residual(acc_ref, x_ref, gate_ref, gn_ref, o_ref):
    nb, tt, d = x_ref.shape
    y = acc_ref[...].reshape(nb, tt, d)
    yn = y * lax.rsqrt(jnp.mean(y * y, axis=-1, keepdims=True) + EPS) * gn_ref[...]
    x_new = x_ref[...] + gate_ref[...] * yn
    o_ref[...] = x_new
    return x_new


def _mix_out_kernel(a_ref, w_ref, x_ref, gate_ref, gn_ref, g2n_ref, sc_ref, sh_ref, o_ref, h_ref, acc_ref):
    kk = pl.program_id(2)
    nb, tt, _ = x_ref.shape
    _accumulate(acc_ref, kk, a_ref[...].reshape(nb * tt, a_ref.shape[-1]), w_ref)

    @pl.when(kk == pl.num_programs(2) - 1)
    def _():
        x_new = _norm_residual(acc_ref, x_ref, gate_ref, gn_ref, o_ref)
        hn = x_new * lax.rsqrt(jnp.mean(x_new * x_new, axis=-1, keepdims=True) + EPS) * g2n_ref[...]
        h_ref[...] = (hn * (1.0 + sc_ref[...]) + sh_ref[...]).astype(h_ref.dtype)


def _mix_out(merged, w, l, x, gate, g_norm, g_next, scale_next, shift_next):
    b, t, d = x.shape
    k = w.shape[1]
    nb, tt = _row_tiles(b, t)
    tk = 1024
    full = lambda i, j, kk: (i, j, 0)
    per_batch = pl.BlockSpec((nb, 1, d), lambda i, j, kk: (i, 0, 0))
    vec = pl.BlockSpec((1, d), lambda i, j, kk: (0, 0))
    return pl.pallas_call(
        _mix_out_kernel,
        grid=(b // nb, t // tt, k // tk),
        in_specs=[pl.BlockSpec((nb, tt, tk), lambda i, j, kk: (i, j, kk)),
                  pl.BlockSpec((None, tk, d), lambda i, j, kk: (l, kk, 0)),
                  pl.BlockSpec((nb, tt, d), full),
                  per_batch, vec, vec, per_batch, per_batch],
        out_specs=(pl.BlockSpec((nb, tt, d), full), pl.BlockSpec((nb, tt, d), full)),
        out_shape=(jax.ShapeDtypeStruct((b, t, d), F32), jax.ShapeDtypeStruct((b, t, d), BF16)),
        scratch_shapes=[pltpu.VMEM((nb * tt, d), F32)],
        compiler_params=_params("parallel", "parallel", "arbitrary"),
        name="mm_out",
    )(merged, w, x, gate, g_norm.reshape(1, d), g_next.reshape(1, d), scale_next, shift_next)


def _ffn_down_kernel(ug_ref, uv_ref, prev_ref, hist_ref, wd_ref, bd_ref, w_ref, x_ref, gate_ref, gn_ref, o_ref,
                     ext_ref, acc_ref):
    j = pl.program_id(1)
    kk = pl.program_id(2)
    nb, tt, _ = x_ref.shape
    ext_ref[:, 0:FFN_HALO, :] = jnp.where(j == 0, hist_ref[...], prev_ref[...].astype(F32))
    ext_ref[:, FFN_HALO:, :] = ug_ref[...].astype(F32)
    lead = FFN_HALO - (FFN_K - 1)
    tk = ext_ref.shape[-1]
    part = None
    for c0 in range(0, tk, FFN_SUB):
        c1 = min(c0 + FFN_SUB, tk)
        conv = bd_ref[:, c0:c1]
        for k in range(FFN_K):
            conv = conv + wd_ref[k:k + 1, c0:c1] * ext_ref[:, lead + k:lead + k + tt, c0:c1]
        act = (_silu(conv) * uv_ref[:, :, c0:c1].astype(F32)).astype(BF16)
        dot = jnp.dot(act.reshape(nb * tt, c1 - c0), w_ref[c0:c1, :], preferred_element_type=F32)
        part = dot if part is None else part + dot

    @pl.when(kk == 0)
    def _():
        acc_ref[...] = part

    @pl.when(kk > 0)
    def _():
        acc_ref[...] += part

    @pl.when(kk == pl.num_programs(2) - 1)
    def _():
        _norm_residual(acc_ref, x_ref, gate_ref, gn_ref, o_ref)


def _ffn_down(up, hist, w_dw, b_dw, w, l, x, gate, g_norm):
    b, t, d = x.shape
    nb, tt = _row_tiles(b, t)
    tk = D_FF // 4
    nkb = D_FF // tk
    hb = tt // FFN_HALO
    full = lambda i, j, kk: (i, j, 0)
    return pl.pallas_call(
        _ffn_down_kernel,
        grid=(b // nb, t // tt, nkb),
        in_specs=[pl.BlockSpec((nb, tt, tk), lambda i, j, kk: (i, j, kk)),
                  pl.BlockSpec((nb, tt, tk), lambda i, j, kk: (i, j, nkb + kk)),
                  pl.BlockSpec((nb, FFN_HALO, tk), lambda i, j, kk: (i, jnp.maximum(j * hb - 1, 0), kk)),
                  pl.BlockSpec((nb, FFN_HALO, tk), lambda i, j, kk: (i, 0, kk)),
                  pl.BlockSpec((FFN_K, tk), lambda i, j, kk: (0, kk)),
                  pl.BlockSpec((1, tk), lambda i, j, kk: (0, kk)),
                  pl.BlockSpec((None, tk, d), lambda i, j, kk: (l, kk, 0)),
                  pl.BlockSpec((nb, tt, d), full),
                  pl.BlockSpec((nb, 1, d), lambda i, j, kk: (i, 0, 0)),
                  pl.BlockSpec((1, d), lambda i, j, kk: (0, 0))],
        out_specs=pl.BlockSpec((nb, tt, d), full),
        out_shape=jax.ShapeDtypeStruct((b, t, d), F32),
        scratch_shapes=[pltpu.VMEM((nb, tt + FFN_HALO, tk), F32), pltpu.VMEM((nb * tt, d), F32)],
        compiler_params=_params("parallel", "parallel", "arbitrary"),
        name="ffn_down",
    )(up, up, up, hist, w_dw, b_dw.reshape(1, D_FF), w, x, gate, g_norm.reshape(1, d))


def _rope_tables(pos0, t):
    half = ROPE_DIM // 2
    inv = ROPE_THETA ** (-jnp.arange(half, dtype=F32) / half)
    pos = (pos0 + jnp.arange(t, dtype=jnp.int32)).astype(F32)
    ang = pos[:, None] * inv[None, :]
    cos, sin = jnp.cos(ang), jnp.sin(ang)
    return (jnp.concatenate([cos, cos, cos, cos], axis=-1),
            jnp.concatenate([-sin, sin, -sin, sin], axis=-1))


def _front_pad(a, rows):
    return jnp.pad(a, ((0, 0), (rows - a.shape[1], 0), (0, 0)))


def _tail(hist, new, n, cols):
    keep = min(new.shape[1], n)
    return jnp.concatenate([hist, new[:, new.shape[1] - keep:, :cols].astype(F32)], axis=1)[:, -n:]


def _attend(g, l, q, ckv_b, krope_b, cos, sin, p):
    t = q.shape[1]
    if g["past_ckv"].shape[2] == 0 and t % 512 == 0:
        kc, vc = _kvcat(ckv_b, krope_b, p["w_uk2"], p["w_uv2"], l)
        return _mha(_qcat(q, cos, sin), kc, vc)
    q_abs, q_rope = _qprep(q, p["w_ukt"], l, cos, sin)
    return _attn(q_abs, q_rope, g["past_ckv"], g["past_krope"], l, ckv_b, krope_b[:, :, :ROPE_DIM], p["w_uv"])


def _layer(groups, l, w, p):
    d = D_MODEL
    dims = [g["x"].shape[:2] for g in groups]
    flat = lambda arrs: [a.reshape(-1, a.shape[-1]) for a in arrs]
    unflat = lambda outs: [o.reshape(b, t, o.shape[-1]) for o, (b, t) in zip(outs, dims)]
    mods = [[g["mod"][:, i] for i in range(6)] for g in groups]
    tables = [_rope_tables(g["pos0"], t) for g, (_, t) in zip(groups, dims)]

    h = flat([_norm_mod(g["x"], p["g_pre_mix"], m[1], m[0]) for g, m in zip(groups, mods)])
    tn = 512
    u_a = unflat(_mm(*h, [(p["w_a"], l, 0), (p["w_a"], l, D_CONV // tn)], D_CONV, F32,
                     tn=tn, epi="glu", name="mm_glu"))
    zmid = unflat(_mm(*h, [(p["w_mid"], l, 0)], MID_W, F32, tn=MID_TN, name="mm_mid"))
    gates2 = _mm(*h, [(p["w_g"], l, 0)], 3 * d, BF16, tn=1024, epi="sigmoid", name="mm_gates")
    gates = unflat(gates2)

    a_act = [_conv_a(u, _front_pad(g["hist_conv"], CONV_HALO), p["w_dwa"], p["b_dwa"], p["ln_a_g"], p["ln_a_b"])
             for u, g in zip(u_a, groups)]
    out_a = _mm(*flat(a_act), [(w["w_pa"], l, 0)], d, BF16, tn=tn, epi="mul", extras=[(*gates2, 0)], name="mm_pa")

    out_b = flat([_pool(z, _front_pad(g["hist_pool"], POOL_HALO), g["pos0"], p["w_pool"], l, p["pool_scale"], gt)
                  for z, g, gt in zip(zmid, groups, gates)])

    lat = [_lat(z, p["g_q_lat"], p["g_kv_lat"], cos, sin) for z, (cos, sin) in zip(zmid, tables)]
    q = unflat(_mm(*flat([o[0] for o in lat]), [(p["w_uq"], l, 0)], p["w_uq"].shape[2], F32, tn=tn, name="mm_uq"))
    o = [_attend(g, l, qg, lg[2], lg[4], cos, sin, p) for g, qg, lg, (cos, sin) in zip(groups, q, lat, tables)]
    merged = unflat(_mm(*flat(o), [(w["w_oc"], l, 0)], d, BF16, tn=tn, epi="mul_add2",
                        extras=[(*gates2, 2 * d // tn), (*out_a, 0), (*out_b, 0)], name="mm_oc"))

    mixed = [_mix_out(mg, p["w_out"], l, g["x"], m[2], p["g_post_mix"], p["g_pre_ffn"], m[4], m[3])
             for mg, g, m in zip(merged, groups, mods)]
    x = [xh[0] for xh in mixed]

    h2 = flat([xh[1] for xh in mixed])
    up = unflat(_mm(*h2, [(w["w_up"], l, 0)], 2 * D_FF, BF16, tn=1024, name="mm_up"))
    x = [_ffn_down(u, _front_pad(g["hist_ffn"], FFN_HALO), p["w_dwf"], p["b_dwf"], p["w_down"], l, xg, m[5],
                   p["g_post_ffn"]) for u, g, xg, m in zip(up, groups, x, mods)]

    states = [(lg[1], lg[3],
               _tail(g["hist_conv"], u, CONV_K - 1, D_CONV),
               _tail(g["hist_pool"], z, POOL_MAX - 1, D_POOL),
               _tail(g["hist_ffn"], uu, FFN_K - 1, D_FF))
              for lg, g, u, z, uu in zip(lat, groups, u_a, zmid, up)]
    return x, states


_SMALL = ("g_pre_mix", "g_post_mix", "w_dwa", "b_dwa", "ln_a_g", "ln_a_b", "pool_scale", "g_q_lat",
          "g_kv_lat", "g_pre_ffn", "g_post_ffn", "w_dwf", "b_dwf")


def _prep_weights(w):
    nl = w["w_in"].shape[0]
    w_in = w["w_in"]
    w_uq = w["w_uq"].reshape(nl, Q_RANK, N_HEADS, NOPE_DIM + ROPE_DIM)
    w_uq = jnp.concatenate([w_uq[..., :NOPE_DIM].reshape(nl, Q_RANK, -1),
                            w_uq[..., NOPE_DIM:].reshape(nl, Q_RANK, -1)], axis=2)
    mid_pad = ((0, 0), (0, 0), (0, MID_W - (OFF_G - OFF_B)))
    return dict(
        w_a=w_in[:, :, :OFF_B].astype(BF16),
        w_mid=jnp.pad(w_in[:, :, OFF_B:OFF_G], mid_pad).astype(BF16),
        w_g=w_in[:, :, OFF_G:].astype(BF16),
        w_pool=w["w_pool"].astype(BF16),
        w_uq=w_uq.astype(BF16),
        w_ukt=jnp.transpose(w["w_uk"], (0, 2, 3, 1)).astype(BF16),
        w_uv=jnp.transpose(w["w_uv"], (0, 2, 1, 3)).astype(BF16),
        w_uk2=w["w_uk"].reshape(nl, KV_RANK, N_HEADS * NOPE_DIM).astype(BF16),
        w_uv2=w["w_uv"].reshape(nl, KV_RANK, N_HEADS * V_DIM).astype(BF16),
        w_out=w["w_out"].astype(BF16), w_down=w["w_down"].astype(BF16),
    )


def kernel(x_prompt, x_sample, cache_ckv, cache_krope, state_conv, state_pool, state_ffn, c_prompt, c_sample, w_mod, b_mod, g_pre_mix, g_post_mix, w_in, w_dwa, b_dwa, ln_a_g, ln_a_b, w_pa, w_pool, pool_scale, g_q_lat, g_kv_lat, w_uq, w_uk, w_uv, w_oc, w_out, g_pre_ffn, g_post_ffn, w_up, w_dwf, b_dwf, w_down):
    weights = dict(g_pre_mix=g_pre_mix, g_post_mix=g_post_mix, w_in=w_in, w_dwa=w_dwa, b_dwa=b_dwa,
                   ln_a_g=ln_a_g, ln_a_b=ln_a_b, w_pa=w_pa, w_pool=w_pool, pool_scale=pool_scale,
                   g_q_lat=g_q_lat, g_kv_lat=g_kv_lat, w_uq=w_uq, w_uk=w_uk, w_uv=w_uv, w_oc=w_oc,
                   w_out=w_out, g_pre_ffn=g_pre_ffn, g_post_ffn=g_post_ffn, w_up=w_up, w_dwf=w_dwf,
                   b_dwf=b_dwf, w_down=w_down)
    depth = w_mod.shape[0]
    bp, bs = x_prompt.shape[0], x_sample.shape[0]
    past_len = cache_ckv.shape[2]
    d = x_prompt.shape[-1]

    rows = -(-(bp + bs) // SUBLANES) * SUBLANES
    c_all = jnp.pad(jnp.concatenate([c_prompt, c_sample], axis=0), ((0, rows - bp - bs), (0, 0)))
    mod_all = _mod(c_all, w_mod, b_mod)

    xp, xs = x_prompt, x_sample
    st_p = [[] for _ in range(5)]
    st_s = [[] for _ in range(5)]
    stacked = _prep_weights(weights)
    for l in range(depth):
        p = dict(stacked, **{name: weights[name][l] for name in _SMALL})
        prompt = dict(x=xp, mod=mod_all[l, :bp].reshape(bp, 6, 1, d), pos0=0,
                      past_ckv=jnp.zeros((depth, bp, 0, KV_RANK), F32),
                      past_krope=jnp.zeros((depth, bp, 0, ROPE_DIM), F32),
                      hist_conv=jnp.zeros((bp, CONV_K - 1, D_CONV), F32),
                      hist_pool=jnp.zeros((bp, POOL_MAX - 1, D_POOL), F32),
                      hist_ffn=jnp.zeros((bp, FFN_K - 1, D_FF), F32))
        sample = dict(x=xs, mod=mod_all[l, bp:bp + bs].reshape(bs, 6, 1, d), pos0=past_len,
                      past_ckv=cache_ckv, past_krope=cache_krope, hist_conv=state_conv[l],
                      hist_pool=state_pool[l], hist_ffn=state_ffn[l])
        (xp, xs), (sp, ss) = _layer((prompt, sample), l, weights, p)
        for i in range(5):
            st_p[i].append(sp[i])
            st_s[i].append(ss[i])
    return (xp, xs) + tuple(jnp.stack(s) for s in st_p) + tuple(jnp.stack(s) for s in st_s)
```

```python
import functools

import jax
import jax.numpy as jnp
from jax import lax
from jax.experimental import pallas as pl
from jax.experimental.pallas import tpu as pltpu

F32 = jnp.float32
BF16 = jnp.bfloat16

D_MODEL = 2048
CHUNK = 64
CHUNK_SHIFT = 6
assert 1 << CHUNK_SHIFT == CHUNK
D_CONV = D_MODEL // 2
CONV_K = 31
D_POOL = D_MODEL // 2
POOL_WINDOWS = (2, 4, 8, 16)
POOL_MAX = 16
N_POOL_GROUPS = 4
POOL_GROUP = D_POOL // N_POOL_GROUPS
POOL_OUT = D_MODEL // N_POOL_GROUPS
N_HEADS = D_MODEL // 128
NOPE_DIM = 128
ROPE_DIM = 64
V_DIM = 128
Q_RANK = D_MODEL // 4
KV_RANK = D_MODEL // 4
ROPE_THETA = 10000.0
ATTN_SCALE = (NOPE_DIM + ROPE_DIM) ** -0.5
LOG2E = 1.4426950408889634
QK_DIM = 256
V_AUG = 256
D_FF = 256 * ((8 * D_MODEL // 3 + 255) // 256)
FFN_K = 3
EPS = 1e-6
NEG = -1e30
OFF_B = 2 * D_CONV
OFF_R = OFF_B + D_POOL + Q_RANK + KV_RANK
OFF_G = OFF_R + ROPE_DIM

LANES = 128
SUBLANES = 8
VMEM_LIMIT_BYTES = 56 * 1024 * 1024

MID_Q = D_POOL
MID_KV = MID_Q + Q_RANK
MID_R = MID_KV + KV_RANK
MID_W = MID_R + 2 * LANES
MID_TN = MID_W // 3
CONV_HALO = 32
POOL_HALO = 16
FFN_HALO = 8


def _params(*sem):
    return pltpu.CompilerParams(dimension_semantics=sem, vmem_limit_bytes=VMEM_LIMIT_BYTES)


def _sigmoid(x):
    return 0.5 + 0.5 * jnp.tanh(0.5 * x)


def _silu(x):
    h = 0.5 * x
    return h + h * jnp.tanh(h)


def _row_tiles(b, t):
    if t >= 512:
        return 1, 512
    return b, t


def _mod_kernel(c_ref, w_ref, b_ref, o_ref):
    c = c_ref[...]
    a = _silu(c).astype(BF16)
    o_ref[...] = jnp.dot(a, w_ref[...].astype(BF16), preferred_element_type=F32) + b_ref[...]


def _mod(c_all, w_mod, b_mod):
    nl, d, n = w_mod.shape
    bp = c_all.shape[0]
    tn = 1024
    return pl.pallas_call(
        _mod_kernel,
        grid=(nl, n // tn),
        in_specs=[pl.BlockSpec((bp, d), lambda l, j: (0, 0)),
                  pl.BlockSpec((None, d, tn), lambda l, j: (l, 0, j)),
                  pl.BlockSpec((None, 1, tn), lambda l, j: (l, 0, j))],
        out_specs=pl.BlockSpec((None, bp, tn), lambda l, j: (l, 0, j)),
        out_shape=jax.ShapeDtypeStruct((nl, bp, n), F32),
        compiler_params=_params("parallel", "parallel"),
        name="mod",
    )(c_all, w_mod, b_mod.reshape(nl, 1, n))


def _norm_mod_kernel(x_ref, g_ref, sc_ref, sh_ref, o_ref):
    x = x_ref[...]
    y = x * lax.rsqrt(jnp.mean(x * x, axis=-1, keepdims=True) + EPS) * g_ref[...]
    o_ref[...] = (y * (1.0 + sc_ref[...]) + sh_ref[...]).astype(o_ref.dtype)


def _norm_mod(x, g, scale, shift):
    b, t, d = x.shape
    nb, tt = _row_tiles(b, t)
    return pl.pallas_call(
        _norm_mod_kernel,
        grid=(b // nb, t // tt),
        in_specs=[pl.BlockSpec((nb, tt, d), lambda i, j: (i, j, 0)),
                  pl.BlockSpec((1, d), lambda i, j: (0, 0)),
                  pl.BlockSpec((nb, 1, d), lambda i, j: (i, 0, 0)),
                  pl.BlockSpec((nb, 1, d), lambda i, j: (i, 0, 0))],
        out_specs=pl.BlockSpec((nb, tt, d), lambda i, j: (i, j, 0)),
        out_shape=jax.ShapeDtypeStruct((b, t, d), BF16),
        compiler_params=_params("parallel", "parallel"),
        name="norm_mod",
    )(x, g.reshape(1, d), scale, shift)


def _mm_kernel(*refs, epi, n_w, n_e):
    ap_ref, as_ref = refs[0], refs[1]
    w_refs = refs[2:2 + n_w]
    e_refs = refs[2 + n_w:2 + n_w + 2 * n_e]
    op_ref, os_ref = refs[2 + n_w + 2 * n_e], refs[3 + n_w + 2 * n_e]
    wb_refs = refs[4 + n_w + 2 * n_e:]
    m = pl.program_id(1)

    @pl.when(m == 0)
    def _():
        for w_ref, wb_ref in zip(w_refs, wb_refs):
            wb_ref[...] = w_ref[...].astype(BF16)

    def compute(a, es):
        z = jnp.dot(a, wb_refs[0][...], preferred_element_type=F32)
        if epi == "sigmoid":
            z = _sigmoid(z)
        elif epi == "glu":
            z = z * _sigmoid(jnp.dot(a, wb_refs[1][...], preferred_element_type=F32))
        elif epi == "mul":
            z = z * es[0][...].astype(F32)
        elif epi == "mul_add2":
            z = z * es[0][...].astype(F32) + es[1][...].astype(F32) + es[2][...].astype(F32)
        return z

    @pl.when(m == 0)
    def _():
        os_ref[...] = compute(as_ref[...], e_refs[1::2]).astype(os_ref.dtype)

    @pl.when(m > 0)
    def _():
        op_ref[...] = compute(ap_ref[...], e_refs[0::2]).astype(op_ref.dtype)


def _mm(a_p, a_s, ws, n, out_dtype, *, tn, epi="none", extras=(), name="mm"):
    rp, k = a_p.shape
    rs = a_s.shape[0]
    tm = min(rp, 1024)
    npt = rp // tm

    def w_spec(arr, lead, first):
        if lead is None:
            return pl.BlockSpec((k, tn), lambda j, m: (0, first + j))
        return pl.BlockSpec((None, k, tn), lambda j, m: (lead, 0, first + j))

    def e_specs(first):
        return [pl.BlockSpec((tm, tn), lambda j, m: (jnp.maximum(m - 1, 0), first + j)),
                pl.BlockSpec((rs, tn), lambda j, m: (0, first + j))]

    p_row = lambda j, m: (jnp.maximum(m - 1, 0), 0)
    p_out = lambda j, m: (jnp.maximum(m - 1, 0), j)
    e_in = [arr for e in extras for arr in e[:2]]
    return pl.pallas_call(
        functools.partial(_mm_kernel, epi=epi, n_w=len(ws), n_e=len(extras)),
        grid=(n // tn, npt + 1),
        in_specs=[pl.BlockSpec((tm, k), p_row), pl.BlockSpec((rs, k), lambda j, m: (0, 0))]
        + [w_spec(*w) for w in ws] + [s for e in extras for s in e_specs(e[2])],
        out_specs=(pl.BlockSpec((tm, tn), p_out), pl.BlockSpec((rs, tn), lambda j, m: (0, j))),
        out_shape=(jax.ShapeDtypeStruct((rp, n), out_dtype), jax.ShapeDtypeStruct((rs, n), out_dtype)),
        scratch_shapes=[pltpu.VMEM((k, tn), BF16) for _ in ws],
        compiler_params=_params("parallel", "arbitrary"),
        name=name,
    )(a_p, a_s, *[w[0] for w in ws], *e_in)


def _conv_a_kernel(u_ref, prev_ref, hist_ref, w_ref, b_ref, g_ref, be_ref, o_ref, ext_ref, sh_ref, a_ref, *, tt):
    j = pl.program_id(1)
    ext_ref[0:CONV_HALO, :] = jnp.where(j == 0, hist_ref[...], prev_ref[...])
    ext_ref[CONV_HALO:, :] = u_ref[...]
    sh_rows = sh_ref.shape[1]
    for s in range(1, SUBLANES):
        sh_ref[s - 1] = ext_ref[s:s + sh_rows, :]
    rc = 32
    cc = 512
    lead = CONV_HALO - (CONV_K - 1)

    def body(r, carry):
        r0 = pl.multiple_of(r * rc, rc)
        for c0 in range(0, D_CONV, cc):
            acc = jnp.zeros((rc, cc), F32)
            for k in range(CONV_K):
                s = (lead + k) % SUBLANES
                row = pl.multiple_of(r0 + (lead + k - s), SUBLANES)
                if s == 0:
                    x = ext_ref[pl.ds(row, rc), c0:c0 + cc]
                else:
                    x = sh_ref[s - 1, pl.ds(row, rc), c0:c0 + cc]
                acc = acc + w_ref[k:k + 1, c0:c0 + cc] * x
            a_ref[pl.ds(r0, rc), c0:c0 + cc] = acc
        return carry

    lax.fori_loop(0, tt // rc, body, 0)
    a = a_ref[...] + b_ref[...]
    mu = jnp.mean(a, axis=-1, keepdims=True)
    ac = a - mu
    var = jnp.mean(ac * ac, axis=-1, keepdims=True)
    y = ac * lax.rsqrt(var + EPS) * g_ref[...] + be_ref[...]
    o_ref[...] = _silu(y).astype(o_ref.dtype)


def _conv_a(u, hist, w_dw, b_dw, ln_g, ln_b):
    b, t, c = u.shape
    tt = min(t, 256)
    hb = tt // CONV_HALO
    vec = lambda i, j: (0, 0)
    return pl.pallas_call(
        functools.partial(_conv_a_kernel, tt=tt),
        grid=(b, t // tt),
        in_specs=[pl.BlockSpec((None, tt, c), lambda i, j: (i, j, 0)),
                  pl.BlockSpec((None, CONV_HALO, c), lambda i, j: (i, jnp.maximum(j * hb - 1, 0), 0)),
                  pl.BlockSpec((None, CONV_HALO, c), lambda i, j: (i, 0, 0)),
                  pl.BlockSpec((CONV_K, c), vec),
                  pl.BlockSpec((1, c), vec), pl.BlockSpec((1, c), vec), pl.BlockSpec((1, c), vec)],
        out_specs=pl.BlockSpec((None, tt, c), lambda i, j: (i, j, 0)),
        out_shape=jax.ShapeDtypeStruct((b, t, c), BF16),
        scratch_shapes=[pltpu.VMEM((tt + CONV_HALO, c), F32),
                        pltpu.VMEM((SUBLANES - 1, tt + CONV_HALO - SUBLANES, c), F32),
                        pltpu.VMEM((tt, c), F32)],
        compiler_params=_params("parallel", "parallel"),
        name="conv_a",
    )(u, u, hist, w_dw, b_dw.reshape(1, c), ln_g.reshape(1, c), ln_b.reshape(1, c))


def _pool_kernel(z_ref, prev_ref, hist_ref, w_ref, s_ref, gate_ref, o_ref, ext_ref, *, tt, pos0):
    j = pl.program_id(1)
    ext_ref[0:POOL_HALO, :] = jnp.where(j == 0, hist_ref[...], prev_ref[...])
    ext_ref[POOL_HALO:, :] = z_ref[...]
    pos = pos0 + j * tt + lax.broadcasted_iota(jnp.int32, (tt, 1), 0)
    for g, w in enumerate(POOL_WINDOWS):
        c0 = g * POOL_GROUP
        cur = ext_ref[:, c0:c0 + POOL_GROUP]
        sh = 1
        while sh < w:
            cur = cur + pltpu.roll(cur, sh, 0)
            sh *= 2
        win = cur[POOL_HALO:, :]
        cnt = jnp.minimum(w, pos + 1).astype(F32)
        m = (win / cnt - ext_ref[POOL_HALO:, c0:c0 + POOL_GROUP]).astype(BF16)
        cols = slice(g * POOL_OUT, (g + 1) * POOL_OUT)
        out = jnp.dot(m, w_ref[g], preferred_element_type=F32) * s_ref[:, cols]
        o_ref[:, cols] = (out * gate_ref[:, cols].astype(F32)).astype(o_ref.dtype)


def _pool(zmid, hist, pos0, w_pool, l, pool_scale, gates):
    b, t, _ = zmid.shape
    c = D_POOL
    tt = min(t, 512)
    hb = tt // POOL_HALO
    return pl.pallas_call(
        functools.partial(_pool_kernel, tt=tt, pos0=pos0),
        grid=(b, t // tt),
        in_specs=[pl.BlockSpec((None, tt, c), lambda i, j: (i, j, 0)),
                  pl.BlockSpec((None, POOL_HALO, c), lambda i, j: (i, jnp.maximum(j * hb - 1, 0), 0)),
                  pl.BlockSpec((None, POOL_HALO, c), lambda i, j: (i, 0, 0)),
                  pl.BlockSpec((None, N_POOL_GROUPS, POOL_GROUP, POOL_OUT), lambda i, j: (l, 0, 0, 0)),
                  pl.BlockSpec((1, D_MODEL), lambda i, j: (0, 0)),
                  pl.BlockSpec((None, tt, D_MODEL), lambda i, j: (i, j, 1))],
        out_specs=pl.BlockSpec((None, tt, D_MODEL), lambda i, j: (i, j, 0)),
        out_shape=jax.ShapeDtypeStruct((b, t, D_MODEL), BF16),
        scratch_shapes=[pltpu.VMEM((tt + POOL_HALO, c), F32)],
        compiler_params=_params("parallel", "parallel"),
        name="pool",
    )(zmid, zmid, hist, w_pool, pool_scale.reshape(1, D_MODEL), gates)


def _rot_half(x):
    lane = lax.broadcasted_iota(jnp.int32, x.shape, 1)
    first = (lane % ROPE_DIM) < (ROPE_DIM // 2)
    return jnp.where(first, pltpu.roll(x, LANES - ROPE_DIM // 2, 1), pltpu.roll(x, ROPE_DIM // 2, 1))


def _lat_kernel(z_ref, gq_ref, gkv_ref, cos_ref, sin_ref, ql_ref, ckv_ref, ckvb_ref, kr_ref, krb_ref):
    nb, tt, _ = z_ref.shape
    zq = z_ref[:, :, MID_Q:MID_KV]
    ql = zq * lax.rsqrt(jnp.mean(zq * zq, axis=-1, keepdims=True) + EPS) * gq_ref[...]
    ql_ref[...] = ql.astype(ql_ref.dtype)
    zkv = z_ref[:, :, MID_KV:MID_R]
    ckv = zkv * lax.rsqrt(jnp.mean(zkv * zkv, axis=-1, keepdims=True) + EPS) * gkv_ref[...]
    ckv_ref[...] = ckv
    ckvb_ref[...] = ckv.astype(BF16)
    zr = z_ref[:, :, MID_R:MID_R + LANES]
    rot = _rot_half(zr.reshape(nb * tt, LANES)).reshape(nb, tt, LANES)
    kr = zr * cos_ref[...] + rot * sin_ref[...]
    kr_ref[...] = kr[:, :, :ROPE_DIM]
    krb_ref[...] = kr.astype(BF16)


def _lat(zmid, g_q, g_kv, cos, sin):
    b, t, _ = zmid.shape
    nb, tt = _row_tiles(b, t)
    row = lambda i, j: (i, j, 0)
    vec = lambda i, j: (0, 0)
    shapes = (jax.ShapeDtypeStruct((b, t, Q_RANK), BF16),
              jax.ShapeDtypeStruct((b, t, KV_RANK), F32),
              jax.ShapeDtypeStruct((b, t, KV_RANK), BF16),
              jax.ShapeDtypeStruct((b, t, ROPE_DIM), F32),
              jax.ShapeDtypeStruct((b, t, LANES), BF16))
    return pl.pallas_call(
        _lat_kernel,
        grid=(b // nb, t // tt),
        in_specs=[pl.BlockSpec((nb, tt, MID_W), row),
                  pl.BlockSpec((1, Q_RANK), vec), pl.BlockSpec((1, KV_RANK), vec),
                  pl.BlockSpec((tt, LANES), lambda i, j: (j, 0)),
                  pl.BlockSpec((tt, LANES), lambda i, j: (j, 0))],
        out_specs=(pl.BlockSpec((nb, tt, Q_RANK), row), pl.BlockSpec((nb, tt, KV_RANK), row),
                   pl.BlockSpec((nb, tt, KV_RANK), row), pl.BlockSpec((nb, tt, ROPE_DIM), row),
                   pl.BlockSpec((nb, tt, LANES), row)),
        out_shape=shapes,
        compiler_params=_params("parallel", "parallel"),
        name="lat",
    )(zmid, g_q.reshape(1, Q_RANK), g_kv.reshape(1, KV_RANK), cos, sin)


def _qprep_kernel(q_ref, wuk_ref, cos_ref, sin_ref, qa_ref, qr_ref):
    nb, tt, _ = q_ref.shape
    rows = nb * tt
    nope = N_HEADS * NOPE_DIM
    for h in range(N_HEADS):
        qn = q_ref[:, :, h * NOPE_DIM:(h + 1) * NOPE_DIM].reshape(rows, NOPE_DIM).astype(BF16)
        qa = jnp.dot(qn, wuk_ref[h], preferred_element_type=F32) * ATTN_SCALE
        qa_ref[h] = qa.reshape(nb, tt, KV_RANK).astype(qa_ref.dtype)
    cos = cos_ref[...]
    sin = sin_ref[...]
    for c in range(N_HEADS * ROPE_DIM // LANES):
        x = q_ref[:, :, nope + c * LANES:nope + (c + 1) * LANES]
        rot = _rot_half(x.reshape(rows, LANES)).reshape(nb, tt, LANES)
        r = ((x * cos + rot * sin) * ATTN_SCALE).astype(qr_ref.dtype)
        qr_ref[2 * c] = r[:, :, :ROPE_DIM]
        qr_ref[2 * c + 1] = r[:, :, ROPE_DIM:]


def _qprep(q, w_ukt, l, cos, sin):
    b, t, qw = q.shape
    nb, tt = (1, 256) if t >= 256 else (b, t)
    return pl.pallas_call(
        _qprep_kernel,
        grid=(b // nb, t // tt),
        in_specs=[pl.BlockSpec((nb, tt, qw), lambda i, j: (i, j, 0)),
                  pl.BlockSpec((None, N_HEADS, NOPE_DIM, KV_RANK), lambda i, j: (l, 0, 0, 0)),
                  pl.BlockSpec((tt, LANES), lambda i, j: (j, 0)),
                  pl.BlockSpec((tt, LANES), lambda i, j: (j, 0))],
        out_specs=(pl.BlockSpec((N_HEADS, nb, tt, KV_RANK), lambda i, j: (0, i, j, 0)),
                   pl.BlockSpec((N_HEADS, nb, tt, ROPE_DIM), lambda i, j: (0, i, j, 0))),
        out_shape=(jax.ShapeDtypeStruct((N_HEADS, b, t, KV_RANK), BF16),
                   jax.ShapeDtypeStruct((N_HEADS, b, t, ROPE_DIM), BF16)),
        compiler_params=_params("parallel", "parallel"),
        name="qprep",
    )(q, w_ukt, cos, sin)


def _qcat_kernel(q_ref, cos_ref, sin_ref, o_ref):
    tt = q_ref.shape[0]
    scale = ATTN_SCALE * LOG2E
    nope = N_HEADS * NOPE_DIM
    cos = cos_ref[...]
    sin = sin_ref[...]
    low = lax.broadcasted_iota(jnp.int32, (tt, LANES), 1) < ROPE_DIM
    for c in range(N_HEADS * ROPE_DIM // LANES):
        x = q_ref[:, nope + c * LANES:nope + (c + 1) * LANES]
        r = (x * cos + _rot_half(x) * sin) * scale
        o_ref[2 * c, :, NOPE_DIM:] = jnp.where(low, r, 0.0).astype(o_ref.dtype)
        o_ref[2 * c + 1, :, NOPE_DIM:] = jnp.where(low, pltpu.roll(r, ROPE_DIM, 1), 0.0).astype(o_ref.dtype)
    for h in range(N_HEADS):
        o_ref[h, :, :NOPE_DIM] = (q_ref[:, h * NOPE_DIM:(h + 1) * NOPE_DIM] * scale).astype(o_ref.dtype)


def _qcat(q, cos, sin):
    b, t, qw = q.shape
    tt = 256
    return pl.pallas_call(
        _qcat_kernel,
        grid=(b, t // tt),
        in_specs=[pl.BlockSpec((None, tt, qw), lambda i, j: (i, j, 0)),
                  pl.BlockSpec((tt, LANES), lambda i, j: (j, 0)),
                  pl.BlockSpec((tt, LANES), lambda i, j: (j, 0))],
        out_specs=pl.BlockSpec((None, N_HEADS, tt, QK_DIM), lambda i, j: (i, 0, j, 0)),
        out_shape=jax.ShapeDtypeStruct((b, N_HEADS, t, QK_DIM), BF16),
        compiler_params=_params("parallel", "parallel"),
        name="qcat",
    )(q, cos, sin)


def _kvcat_kernel(ckv_ref, kr_ref, wuk_ref, wuv_ref, k_ref, v_ref):
    c = ckv_ref[...]
    kn = jnp.dot(c, wuk_ref[...], preferred_element_type=F32)
    vv = jnp.dot(c, wuv_ref[...], preferred_element_type=F32)
    kr = kr_ref[...]
    one_col = (lax.broadcasted_iota(jnp.int32, kr.shape, 1) == 0).astype(v_ref.dtype)
    for h in range(N_HEADS):
        k_ref[h, :, :NOPE_DIM] = kn[:, h * NOPE_DIM:(h + 1) * NOPE_DIM].astype(k_ref.dtype)
        k_ref[h, :, NOPE_DIM:] = kr
        v_ref[h, :, :V_DIM] = vv[:, h * V_DIM:(h + 1) * V_DIM].astype(v_ref.dtype)
        v_ref[h, :, V_DIM:] = one_col


def _kvcat(ckv_b, krope_b, w_uk2, w_uv2, l):
    b, t, _ = ckv_b.shape
    tt = 512
    return pl.pallas_call(
        _kvcat_kernel,
        grid=(b, t // tt),
        in_specs=[pl.BlockSpec((None, tt, KV_RANK), lambda i, j: (i, j, 0)),
                  pl.BlockSpec((None, tt, LANES), lambda i, j: (i, j, 0)),
                  pl.BlockSpec((None, KV_RANK, N_HEADS * NOPE_DIM), lambda i, j: (l, 0, 0)),
                  pl.BlockSpec((None, KV_RANK, N_HEADS * V_DIM), lambda i, j: (l, 0, 0))],
        out_specs=(pl.BlockSpec((None, N_HEADS, tt, QK_DIM), lambda i, j: (i, 0, j, 0)),
                   pl.BlockSpec((None, N_HEADS, tt, V_AUG), lambda i, j: (i, 0, j, 0))),
        out_shape=(jax.ShapeDtypeStruct((b, N_HEADS, t, QK_DIM), BF16),
                   jax.ShapeDtypeStruct((b, N_HEADS, t, V_AUG), BF16)),
        compiler_params=_params("parallel", "parallel"),
        name="kvcat",
    )(ckv_b, krope_b, w_uk2, w_uv2)


def _mha_kernel(q_ref, k_ref, v_ref, o_ref, m_ref, acc_ref, *, tb, gh):
    i = pl.program_id(2)
    dn = (((1,), (1,)), ((), ()))
    m_ref[...] = jnp.full(m_ref.shape, NEG, F32)
    acc_ref[...] = jnp.zeros(acc_ref.shape, F32)

    def block(j, masked):
        k0 = pl.multiple_of(j * tb, tb)
        if masked:
            qc = lax.broadcasted_iota(jnp.int32, (tb, tb), 0) >> CHUNK_SHIFT
            kc = lax.broadcasted_iota(jnp.int32, (tb, tb), 1) >> CHUNK_SHIFT
            bias = jnp.where(kc <= qc, 0.0, NEG)
        for g in range(gh):
            s = lax.dot_general(q_ref[g], k_ref[g, pl.ds(k0, tb), :], dn, preferred_element_type=F32)
            if masked:
                s = s + bias
            m_old = m_ref[g]
            m_new = jnp.maximum(m_old, jnp.max(s, axis=-1, keepdims=True))
            alpha = jnp.exp2(m_old - m_new)
            p = jnp.exp2(s - jnp.tile(m_new, (1, tb // LANES)))
            pv = jnp.dot(p.astype(BF16), v_ref[g, pl.ds(k0, tb), :], preferred_element_type=F32)
            acc_ref[g] = jnp.tile(alpha, (1, V_AUG // LANES)) * acc_ref[g] + pv
            m_ref[g] = m_new

    def full_block(j, carry):
        block(j, False)
        return carry

    lax.fori_loop(0, i, full_block, 0)
    block(i, True)
    for g in range(gh):
        acc = acc_ref[g]
        o_ref[:, g * V_DIM:(g + 1) * V_DIM] = (acc[:, :V_DIM] / acc[:, V_DIM:V_DIM + 1]).astype(o_ref.dtype)


def _mha(qc, kc, vc):
    b, nh, t, _ = qc.shape
    tb = 512
    gh = 8
    resident = dict(pipeline_mode=pl.Buffered(1))
    return pl.pallas_call(
        functools.partial(_mha_kernel, tb=tb, gh=gh),
        grid=(b, nh // gh, t // tb),
        in_specs=[pl.BlockSpec((None, gh, tb, QK_DIM), lambda bi, hg, i: (bi, hg, i, 0)),
                  pl.BlockSpec((None, gh, t, QK_DIM), lambda bi, hg, i: (bi, hg, 0, 0), **resident),
                  pl.BlockSpec((None, gh, t, V_AUG), lambda bi, hg, i: (bi, hg, 0, 0), **resident)],
        out_specs=pl.BlockSpec((None, tb, gh * V_DIM), lambda bi, hg, i: (bi, i, hg)),
        out_shape=jax.ShapeDtypeStruct((b, t, nh * V_DIM), BF16),
        scratch_shapes=[pltpu.VMEM((gh, tb, LANES), F32), pltpu.VMEM((gh, tb, V_AUG), F32)],
        compiler_params=_params("parallel", "parallel", "parallel"),
        name="mha",
    )(qc, kc, vc)


def _attn_kernel(qa_ref, qr_ref, pk_ref, pkr_ref, nk_ref, nkr_ref, wuv_ref, o_ref, m_ref, l_ref, acc_ref,
                 *, tq, tk, n_past, past_len):
    i = pl.program_id(1)
    kk = pl.program_id(2)
    rows = N_HEADS * tq
    dn = (((1,), (1,)), ((), ()))

    @pl.when(kk == 0)
    def _():
        m_ref[...] = jnp.full(m_ref.shape, NEG, F32)
        l_ref[...] = jnp.zeros(l_ref.shape, F32)
        acc_ref[...] = jnp.zeros(acc_ref.shape, F32)

    def update(k, kr, ok):
        n = k.shape[0]
        s = lax.dot_general(qa_ref[...].reshape(rows, KV_RANK), k, dn, preferred_element_type=F32)
        s = s + lax.dot_general(qr_ref[...].reshape(rows, ROPE_DIM), kr, dn, preferred_element_type=F32)
        if ok is not None:
            s = jnp.where(ok[None], s.reshape(N_HEADS, tq, n), NEG).reshape(rows, n)
        m_old = m_ref[...]
        m_new = jnp.maximum(m_old, jnp.max(s, axis=-1, keepdims=True))
        alpha = jnp.exp(m_old - m_new)
        p = jnp.exp(s - m_new)
        l_ref[...] = alpha * l_ref[...] + jnp.sum(p, axis=-1, keepdims=True)
        acc_ref[...] = alpha * acc_ref[...] + jnp.dot(p.astype(BF16), k, preferred_element_type=F32)
        m_ref[...] = m_new

    @pl.when(kk < n_past)
    def _():
        ok = None
        if n_past * tk != past_len:
            ok = kk * tk + lax.broadcasted_iota(jnp.int32, (tq, tk), 1) < past_len
        update(pk_ref[...].astype(BF16), pkr_ref[...].astype(BF16), ok)

    @pl.when(kk == n_past)
    def _():
        n = nk_ref.shape[0]
        qpos = past_len + i * tq + lax.broadcasted_iota(jnp.int32, (tq, n), 0)
        kpos = past_len + lax.broadcasted_iota(jnp.int32, (tq, n), 1)
        update(nk_ref[...], nkr_ref[...], (kpos >> CHUNK_SHIFT) <= (qpos >> CHUNK_SHIFT))
        o_lat = (acc_ref[...] / l_ref[...]).astype(BF16).reshape(N_HEADS, tq, KV_RANK)
        for h in range(N_HEADS):
            o_h = jnp.dot(o_lat[h], wuv_ref[h], preferred_element_type=F32)
            o_ref[:, h * V_DIM:(h + 1) * V_DIM] = o_h.astype(o_ref.dtype)


def _attn(q_abs, q_rope, past_ckv, past_krope, l, new_ckv, new_krope, w_uv):
    _, b, t, _ = q_abs.shape
    past_len = past_ckv.shape[2]
    tq = min(t, 128)
    tk = next((c for c in (1024, 512) if past_len and past_len % c == 0), 512)
    n_past = -(-past_len // tk)
    pad = ((0, 0), (0, 0), (0, max(n_past, 1) * tk - past_len), (0, 0))
    past_ckv, past_krope = jnp.pad(past_ckv, pad), jnp.pad(past_krope, pad)
    past_idx = lambda bi, i, kk: (l, bi, jnp.minimum(kk, max(n_past - 1, 0)), 0)
    new_idx = lambda bi, i, kk: (bi, 0, 0)
    return pl.pallas_call(
        functools.partial(_attn_kernel, tq=tq, tk=tk, n_past=n_past, past_len=past_len),
        grid=(b, t // tq, n_past + 1),
        in_specs=[pl.BlockSpec((N_HEADS, None, tq, KV_RANK), lambda bi, i, kk: (0, bi, i, 0)),
                  pl.BlockSpec((N_HEADS, None, tq, ROPE_DIM), lambda bi, i, kk: (0, bi, i, 0)),
                  pl.BlockSpec((None, None, tk, KV_RANK), past_idx),
                  pl.BlockSpec((None, None, tk, ROPE_DIM), past_idx),
                  pl.BlockSpec((None, t, KV_RANK), new_idx),
                  pl.BlockSpec((None, t, ROPE_DIM), new_idx),
                  pl.BlockSpec((None, N_HEADS, KV_RANK, V_DIM), lambda bi, i, kk: (l, 0, 0, 0))],
        out_specs=pl.BlockSpec((None, tq, N_HEADS * V_DIM), lambda bi, i, kk: (bi, i, 0)),
        out_shape=jax.ShapeDtypeStruct((b, t, N_HEADS * V_DIM), BF16),
        scratch_shapes=[pltpu.VMEM((N_HEADS * tq, 1), F32), pltpu.VMEM((N_HEADS * tq, 1), F32),
                        pltpu.VMEM((N_HEADS * tq, KV_RANK), F32)],
        compiler_params=_params("parallel", "parallel", "arbitrary"),
        name="attn",
    )(q_abs, q_rope, past_ckv, past_krope, new_ckv, new_krope, w_uv)


def _norm_residual(y, x_ref, gate_ref, gn_ref, o_ref):
    nb, tt, d = x_ref.shape
    y = y.reshape(nb, tt, d)
    yn = y * lax.rsqrt(jnp.mean(y * y, axis=-1, keepdims=True) + EPS) * gn_ref[...]
    x_new = x_ref[...] + gate_ref[...] * yn
    o_ref[...] = x_new
    return x_new


def _mix_out_kernel(a_ref, w_ref, x_ref, gate_ref, gn_ref, g2n_ref, sc_ref, sh_ref, o_ref, h_ref,
                    acc0_ref, acc1_ref, *, nt):
    j = pl.program_id(1)
    nb, tt, _ = x_ref.shape
    accs = (acc0_ref, acc1_ref)

    def project(acc_ref):
        a = a_ref[...]
        acc_ref[...] = jnp.dot(a.reshape(nb * tt, a.shape[-1]), w_ref[...], preferred_element_type=F32)

    def finish(acc_ref):
        x_new = _norm_residual(acc_ref[...], x_ref, gate_ref, gn_ref, o_ref)
        hn = x_new * lax.rsqrt(jnp.mean(x_new * x_new, axis=-1, keepdims=True) + EPS) * g2n_ref[...]
        h_ref[...] = (hn * (1.0 + sc_ref[...]) + sh_ref[...]).astype(h_ref.dtype)

    @pl.when(j == 0)
    def _():
        project(accs[0])

    for parity in range(2):
        @pl.when(jnp.logical_and(jnp.logical_and(j > 0, j < nt), j % 2 == parity))
        def _():
            project(accs[parity])
            finish(accs[1 - parity])

    @pl.when(j == nt)
    def _():
        finish(accs[(nt - 1) % 2])


def _mix_out(merged, w, l, x, gate, g_norm, g_next, scale_next, shift_next):
    b, t, d = x.shape
    k = w.shape[1]
    nb, tt = _row_tiles(b, t)
    nt = t // tt
    done = lambda i, j: (i, jnp.maximum(j - 1, 0), 0)
    per_batch = pl.BlockSpec((nb, 1, d), lambda i, j: (i, 0, 0))
    vec = pl.BlockSpec((1, d), lambda i, j: (0, 0))
    return pl.pallas_call(
        functools.partial(_mix_out_kernel, nt=nt),
        grid=(b // nb, nt + 1),
        in_specs=[pl.BlockSpec((nb, tt, k), lambda i, j: (i, jnp.minimum(j, nt - 1), 0)),
                  pl.BlockSpec((None, k, d), lambda i, j: (l, 0, 0), pipeline_mode=pl.Buffered(1)),
                  pl.BlockSpec((nb, tt, d), done),
                  per_batch, vec, vec, per_batch, per_batch],
        out_specs=(pl.BlockSpec((nb, tt, d), done), pl.BlockSpec((nb, tt, d), done)),
        out_shape=(jax.ShapeDtypeStruct((b, t, d), F32), jax.ShapeDtypeStruct((b, t, d), BF16)),
        scratch_shapes=[pltpu.VMEM((nb * tt, d), F32), pltpu.VMEM((nb * tt, d), F32)],
        compiler_params=_params("parallel", "arbitrary"),
        name="mm_out",
    )(merged, w, x, gate, g_norm.reshape(1, d), g_next.reshape(1, d), scale_next, shift_next)


def _ffn_down_kernel(ug_ref, uv_ref, prev_ref, hist_ref, wd_ref, bd_ref, w_ref, x_ref, gate_ref, gn_ref, o_ref,
                     ext_ref, act_ref, acc_ref, *, nkb):
    j = pl.program_id(1)
    s = pl.program_id(2)
    nb, tt, _ = x_ref.shape
    lead = FFN_HALO - (FFN_K - 1)

    def activate(slot):
        ext_ref[:, 0:FFN_HALO, :] = jnp.where(j == 0, hist_ref[...], prev_ref[...].astype(F32))
        ext_ref[:, FFN_HALO:, :] = ug_ref[...].astype(F32)
        conv = bd_ref[...]
        for k in range(FFN_K):
            conv = conv + wd_ref[k:k + 1, :] * ext_ref[:, lead + k:lead + k + tt, :]
        act = (_silu(conv) * uv_ref[...].astype(F32)).astype(BF16)
        act_ref[slot] = act.reshape(nb * tt, act.shape[-1])

    def project(slot):
        acc_ref[...] += jnp.dot(act_ref[slot], w_ref[...], preferred_element_type=F32)

    @pl.when(s == 0)
    def _():
        acc_ref[...] = jnp.zeros(acc_ref.shape, F32)
        activate(0)

    @pl.when(jnp.logical_and(s > 0, s < nkb))
    def _():
        slot = s % 2
        activate(slot)
        project(1 - slot)

    @pl.when(s == nkb)
    def _():
        project((nkb - 1) % 2)
        _norm_residual(acc_ref[...], x_ref, gate_ref, gn_ref, o_ref)


def _ffn_down(up, hist, w_dw, b_dw, w, l, x, gate, g_norm):
    b, t, d = x.shape
    nb, tt = _row_tiles(b, t)
    tk = D_FF // 4
    nkb = D_FF // tk
    hb = tt // FFN_HALO
    full = lambda i, j, s: (i, j, 0)
    ka = lambda s: jnp.minimum(s, nkb - 1)
    kp = lambda s: jnp.maximum(s - 1, 0)
    return pl.pallas_call(
        functools.partial(_ffn_down_kernel, nkb=nkb),
        grid=(b // nb, t // tt, nkb + 1),
        in_specs=[pl.BlockSpec((nb, tt, tk), lambda i, j, s: (i, j, ka(s))),
                  pl.BlockSpec((nb, tt, tk), lambda i, j, s: (i, j, nkb + ka(s))),
                  pl.BlockSpec((nb, FFN_HALO, tk), lambda i, j, s: (i, jnp.maximum(j * hb - 1, 0), ka(s))),
                  pl.BlockSpec((nb, FFN_HALO, tk), lambda i, j, s: (i, 0, ka(s))),
                  pl.BlockSpec((FFN_K, tk), lambda i, j, s: (0, ka(s))),
                  pl.BlockSpec((1, tk), lambda i, j, s: (0, ka(s))),
                  pl.BlockSpec((None, tk, d), lambda i, j, s: (l, kp(s), 0)),
                  pl.BlockSpec((nb, tt, d), full),
                  pl.BlockSpec((nb, 1, d), lambda i, j, s: (i, 0, 0)),
                  pl.BlockSpec((1, d), lambda i, j, s: (0, 0))],
        out_specs=pl.BlockSpec((nb, tt, d), full),
        out_shape=jax.ShapeDtypeStruct((b, t, d), F32),
        scratch_shapes=[pltpu.VMEM((nb, tt + FFN_HALO, tk), F32), pltpu.VMEM((2, nb * tt, tk), BF16),
                        pltpu.VMEM((nb * tt, d), F32)],
        compiler_params=_params("parallel", "parallel", "arbitrary"),
        name="ffn_down",
    )(up, up, up, hist, w_dw, b_dw.reshape(1, D_FF), w, x, gate, g_norm.reshape(1, d))


def _rope_tables(pos0, t):
    half = ROPE_DIM // 2
    inv = ROPE_THETA ** (-jnp.arange(half, dtype=F32) / half)
    pos = (pos0 + jnp.arange(t, dtype=jnp.int32)).astype(F32)
    ang = pos[:, None] * inv[None, :]
    cos, sin = jnp.cos(ang), jnp.sin(ang)
    return (jnp.concatenate([cos, cos, cos, cos], axis=-1),
            jnp.concatenate([-sin, sin, -sin, sin], axis=-1))


def _front_pad(a, rows):
    return jnp.pad(a, ((0, 0), (rows - a.shape[1], 0), (0, 0)))


def _tail(hist, new, n, cols):
    keep = min(new.shape[1], n)
    return jnp.concatenate([hist, new[:, new.shape[1] - keep:, :cols].astype(F32)], axis=1)[:, -n:]


def _attend(g, l, q, ckv_b, krope_b, cos, sin, p):
    t = q.shape[1]
    if g["past_ckv"].shape[2] == 0 and t % 512 == 0:
        kc, vc = _kvcat(ckv_b, krope_b, p["w_uk2"], p["w_uv2"], l)
        return _mha(_qcat(q, cos, sin), kc, vc)
    q_abs, q_rope = _qprep(q, p["w_ukt"], l, cos, sin)
    return _attn(q_abs, q_rope, g["past_ckv"], g["past_krope"], l, ckv_b, krope_b[:, :, :ROPE_DIM], p["w_uv"])


def _layer(groups, l, w, p):
    d = D_MODEL
    dims = [g["x"].shape[:2] for g in groups]
    flat = lambda arrs: [a.reshape(-1, a.shape[-1]) for a in arrs]
    unflat = lambda outs: [o.reshape(b, t, o.shape[-1]) for o, (b, t) in zip(outs, dims)]
    mods = [[g["mod"][:, i] for i in range(6)] for g in groups]
    tables = [_rope_tables(g["pos0"], t) for g, (_, t) in zip(groups, dims)]

    h = flat([_norm_mod(g["x"], p["g_pre_mix"], m[1], m[0]) for g, m in zip(groups, mods)])
    tn = 512
    u_a = unflat(_mm(*h, [(p["w_a"], l, 0), (p["w_a"], l, D_CONV // tn)], D_CONV, F32,
                     tn=tn, epi="glu", name="mm_glu"))
    zmid = unflat(_mm(*h, [(p["w_mid"], l, 0)], MID_W, F32, tn=MID_TN, name="mm_mid"))
    gates2 = _mm(*h, [(p["w_g"], l, 0)], 3 * d, BF16, tn=1024, epi="sigmoid", name="mm_gates")
    gates = unflat(gates2)

    a_act = [_conv_a(u, _front_pad(g["hist_conv"], CONV_HALO), p["w_dwa"], p["b_dwa"], p["ln_a_g"], p["ln_a_b"])
             for u, g in zip(u_a, groups)]
    out_a = _mm(*flat(a_act), [(w["w_pa"], l, 0)], d, BF16, tn=tn, epi="mul", extras=[(*gates2, 0)], name="mm_pa")

    out_b = flat([_pool(z, _front_pad(g["hist_pool"], POOL_HALO), g["pos0"], p["w_pool"], l, p["pool_scale"], gt)
                  for z, g, gt in zip(zmid, groups, gates)])

    lat = [_lat(z, p["g_q_lat"], p["g_kv_lat"], cos, sin) for z, (cos, sin) in zip(zmid, tables)]
    q = unflat(_mm(*flat([o[0] for o in lat]), [(p["w_uq"], l, 0)], p["w_uq"].shape[2], F32, tn=tn, name="mm_uq"))
    o = [_attend(g, l, qg, lg[2], lg[4], cos, sin, p) for g, qg, lg, (cos, sin) in zip(groups, q, lat, tables)]
    merged = unflat(_mm(*flat(o), [(w["w_oc"], l, 0)], d, BF16, tn=tn, epi="mul_add2",
                        extras=[(*gates2, 2 * d // tn), (*out_a, 0), (*out_b, 0)], name="mm_oc"))

    mixed = [_mix_out(mg, p["w_out"], l, g["x"], m[2], p["g_post_mix"], p["g_pre_ffn"], m[4], m[3])
             for mg, g, m in zip(merged, groups, mods)]
    x = [xh[0] for xh in mixed]

    h2 = flat([xh[1] for xh in mixed])
    up = unflat(_mm(*h2, [(w["w_up"], l, 0)], 2 * D_FF, BF16, tn=1024, name="mm_up"))
    x = [_ffn_down(u, _front_pad(g["hist_ffn"], FFN_HALO), p["w_dwf"], p["b_dwf"], p["w_down"], l, xg, m[5],
                   p["g_post_ffn"]) for u, g, xg, m in zip(up, groups, x, mods)]

    states = [(lg[1], lg[3],
               _tail(g["hist_conv"], u, CONV_K - 1, D_CONV),
               _tail(g["hist_pool"], z, POOL_MAX - 1, D_POOL),
               _tail(g["hist_ffn"], uu, FFN_K - 1, D_FF))
              for lg, g, u, z, uu in zip(lat, groups, u_a, zmid, up)]
    return x, states


_SMALL = ("g_pre_mix", "g_post_mix", "w_dwa", "b_dwa", "ln_a_g", "ln_a_b", "pool_scale", "g_q_lat",
          "g_kv_lat", "g_pre_ffn", "g_post_ffn", "w_dwf", "b_dwf")


def _prep_weights(w):
    nl = w["w_in"].shape[0]
    w_in = w["w_in"]
    w_uq = w["w_uq"].reshape(nl, Q_RANK, N_HEADS, NOPE_DIM + ROPE_DIM)
    w_uq = jnp.concatenate([w_uq[..., :NOPE_DIM].reshape(nl, Q_RANK, -1),
                            w_uq[..., NOPE_DIM:].reshape(nl, Q_RANK, -1)], axis=2)
    mid_pad = ((0, 0), (0, 0), (0, MID_W - (OFF_G - OFF_B)))
    return dict(
        w_a=w_in[:, :, :OFF_B].astype(BF16),
        w_mid=jnp.pad(w_in[:, :, OFF_B:OFF_G], mid_pad).astype(BF16),
        w_g=w_in[:, :, OFF_G:].astype(BF16),
        w_pool=w["w_pool"].astype(BF16),
        w_uq=w_uq.astype(BF16),
        w_ukt=jnp.transpose(w["w_uk"], (0, 2, 3, 1)).astype(BF16),
        w_uv=jnp.transpose(w["w_uv"], (0, 2, 1, 3)).astype(BF16),
        w_uk2=w["w_uk"].reshape(nl, KV_RANK, N_HEADS * NOPE_DIM).astype(BF16),
        w_uv2=w["w_uv"].reshape(nl, KV_RANK, N_HEADS * V_DIM).astype(BF16),
        w_out=w["w_out"].astype(BF16), w_down=w["w_down"].astype(BF16),
    )


def kernel(x_prompt, x_sample, cache_ckv, cache_krope, state_conv, state_pool, state_ffn, c_prompt, c_sample, w_mod, b_mod, g_pre_mix, g_post_mix, w_in, w_dwa, b_dwa, ln_a_g, ln_a_b, w_pa, w_pool, pool_scale, g_q_lat, g_kv_lat, w_uq, w_uk, w_uv, w_oc, w_out, g_pre_ffn, g_post_ffn, w_up, w_dwf, b_dwf, w_down):
    weights = dict(g_pre_mix=g_pre_mix, g_post_mix=g_post_mix, w_in=w_in, w_dwa=w_dwa, b_dwa=b_dwa,
                   ln_a_g=ln_a_g, ln_a_b=ln_a_b, w_pa=w_pa, w_pool=w_pool, pool_scale=pool_scale,
                   g_q_lat=g_q_lat, g_kv_lat=g_kv_lat, w_uq=w_uq, w_uk=w_uk, w_uv=w_uv, w_oc=w_oc,
                   w_out=w_out, g_pre_ffn=g_pre_ffn, g_post_ffn=g_post_ffn, w_up=w_up, w_dwf=w_dwf,
                   b_dwf=b_dwf, w_down=w_down)
    depth = w_mod.shape[0]
    bp, bs = x_prompt.shape[0], x_sample.shape[0]
    past_len = cache_ckv.shape[2]
    d = x_prompt.shape[-1]

    rows = -(-(bp + bs) // SUBLANES) * SUBLANES
    c_all = jnp.pad(jnp.concatenate([c_prompt, c_sample], axis=0), ((0, rows - bp - bs), (0, 0)))
    mod_all = _mod(c_all, w_mod, b_mod)

    xp, xs = x_prompt, x_sample
    st_p = [[] for _ in range(5)]
    st_s = [[] for _ in range(5)]
    stacked = _prep_weights(weights)
    for l in range(depth):
        p = dict(stacked, **{name: weights[name][l] for name in _SMALL})
        prompt = dict(x=xp, mod=mod_all[l, :bp].reshape(bp, 6, 1, d), pos0=0,
                      past_ckv=jnp.zeros((depth, bp, 0, KV_RANK), F32),
                      past_krope=jnp.zeros((depth, bp, 0, ROPE_DIM), F32),
                      hist_conv=jnp.zeros((bp, CONV_K - 1, D_CONV), F32),
                      hist_pool=jnp.zeros((bp, POOL_MAX - 1, D_POOL), F32),
                      hist_ffn=jnp.zeros((bp, FFN_K - 1, D_FF), F32))
        sample = dict(x=xs, mod=mod_all[l, bp:bp + bs].reshape(bs, 6, 1, d), pos0=past_len,
                      past_ckv=cache_ckv, past_krope=cache_krope, hist_conv=state_conv[l],
                      hist_pool=state_pool[l], hist_ffn=state_ffn[l])
        (xp, xs), (sp, ss) = _layer((prompt, sample), l, weights, p)
        for i in range(5):
            st_p[i].append(sp[i])
            st_s[i].append(ss[i])
    return (xp, xs) + tuple(jnp.stack(s) for s in st_p) + tuple(jnp.stack(s) for s in st_s)
```

```python
import functools

import jax
import jax.numpy as jnp
from jax import lax
from jax.experimental import pallas as pl
from jax.experimental.pallas import tpu as pltpu

F32 = jnp.float32
BF16 = jnp.bfloat16

D_MODEL = 2048
CHUNK = 64
CHUNK_SHIFT = 6
assert 1 << CHUNK_SHIFT == CHUNK
D_CONV = D_MODEL // 2
CONV_K = 31
D_POOL = D_MODEL // 2
POOL_WINDOWS = (2, 4, 8, 16)
POOL_MAX = 16
N_POOL_GROUPS = 4
POOL_GROUP = D_POOL // N_POOL_GROUPS
POOL_OUT = D_MODEL // N_POOL_GROUPS
N_HEADS = D_MODEL // 128
NOPE_DIM = 128
ROPE_DIM = 64
V_DIM = 128
Q_RANK = D_MODEL // 4
KV_RANK = D_MODEL // 4
ROPE_THETA = 10000.0
ATTN_SCALE = (NOPE_DIM + ROPE_DIM) ** -0.5
LOG2E = 1.4426950408889634
QK_DIM = 256
V_AUG = 256
D_FF = 256 * ((8 * D_MODEL // 3 + 255) // 256)
FFN_K = 3
EPS = 1e-6
NEG = -1e30
OFF_B = 2 * D_CONV
OFF_R = OFF_B + D_POOL + Q_RANK + KV_RANK
OFF_G = OFF_R + ROPE_DIM

LANES = 128
SUBLANES = 8
VMEM_LIMIT_BYTES = 56 * 1024 * 1024

MID_Q = D_POOL
MID_KV = MID_Q + Q_RANK
MID_R = MID_KV + KV_RANK
MID_W = MID_R + 2 * LANES
MID_TN = MID_W // 3
CONV_HALO = 32
POOL_HALO = 16
FFN_HALO = 8


def _params(*sem):
    return pltpu.CompilerParams(dimension_semantics=sem, vmem_limit_bytes=VMEM_LIMIT_BYTES)


def _sigmoid(x):
    return 0.5 + 0.5 * jnp.tanh(0.5 * x)


def _silu(x):
    h = 0.5 * x
    return h + h * jnp.tanh(h)


def _row_tiles(b, t):
    if t >= 512:
        return 1, 512
    return b, t


def _mod_kernel(c_ref, w_ref, b_ref, o_ref):
    c = c_ref[...]
    a = _silu(c).astype(BF16)
    o_ref[...] = jnp.dot(a, w_ref[...].astype(BF16), preferred_element_type=F32) + b_ref[...]


def _mod(c_all, w_mod, b_mod):
    nl, d, n = w_mod.shape
    bp = c_all.shape[0]
    tn = 1024
    return pl.pallas_call(
        _mod_kernel,
        grid=(nl, n // tn),
        in_specs=[pl.BlockSpec((bp, d), lambda l, j: (0, 0)),
                  pl.BlockSpec((None, d, tn), lambda l, j: (l, 0, j)),
                  pl.BlockSpec((None, 1, tn), lambda l, j: (l, 0, j))],
        out_specs=pl.BlockSpec((None, bp, tn), lambda l, j: (l, 0, j)),
        out_shape=jax.ShapeDtypeStruct((nl, bp, n), F32),
        compiler_params=_params("parallel", "parallel"),
        name="mod",
    )(c_all, w_mod, b_mod.reshape(nl, 1, n))


def _norm_mod_kernel(x_ref, g_ref, sc_ref, sh_ref, o_ref):
    x = x_ref[...]
    y = x * lax.rsqrt(jnp.mean(x * x, axis=-1, keepdims=True) + EPS) * g_ref[...]
    o_ref[...] = (y * (1.0 + sc_ref[...]) + sh_ref[...]).astype(o_ref.dtype)


def _norm_mod(x, g, scale, shift):
    b, t, d = x.shape
    nb, tt = _row_tiles(b, t)
    return pl.pallas_call(
        _norm_mod_kernel,
        grid=(b // nb, t // tt),
        in_specs=[pl.BlockSpec((nb, tt, d), lambda i, j: (i, j, 0)),
                  pl.BlockSpec((1, d), lambda i, j: (0, 0)),
                  pl.BlockSpec((nb, 1, d), lambda i, j: (i, 0, 0)),
                  pl.BlockSpec((nb, 1, d), lambda i, j: (i, 0, 0))],
        out_specs=pl.BlockSpec((nb, tt, d), lambda i, j: (i, j, 0)),
        out_shape=jax.ShapeDtypeStruct((b, t, d), BF16),
        compiler_params=_params("parallel", "parallel"),
        name="norm_mod",
    )(x, g.reshape(1, d), scale, shift)


def _mm_kernel(*refs, epi, n_w, n_e):
    ap_ref, as_ref = refs[0], refs[1]
    w_refs = refs[2:2 + n_w]
    e_refs = refs[2 + n_w:2 + n_w + 2 * n_e]
    op_ref, os_ref = refs[2 + n_w + 2 * n_e], refs[3 + n_w + 2 * n_e]
    wb_refs = refs[4 + n_w + 2 * n_e:]
    m = pl.program_id(1)

    @pl.when(m == 0)
    def _():
        for w_ref, wb_ref in zip(w_refs, wb_refs):
            wb_ref[...] = w_ref[...].astype(BF16)

    def compute(a, es):
        z = jnp.dot(a, wb_refs[0][...], preferred_element_type=F32)
        if epi == "sigmoid":
            z = _sigmoid(z)
        elif epi == "glu":
            z = z * _sigmoid(jnp.dot(a, wb_refs[1][...], preferred_element_type=F32))
        elif epi == "mul":
            z = z * es[0][...].astype(F32)
        elif epi == "mul_add2":
            z = z * es[0][...].astype(F32) + es[1][...].astype(F32) + es[2][...].astype(F32)
        return z

    @pl.when(m == 0)
    def _():
        os_ref[...] = compute(as_ref[...], e_refs[1::2]).astype(os_ref.dtype)

    @pl.when(m > 0)
    def _():
        op_ref[...] = compute(ap_ref[...], e_refs[0::2]).astype(op_ref.dtype)


def _mm(a_p, a_s, ws, n, out_dtype, *, tn, epi="none", extras=(), name="mm"):
    rp, k = a_p.shape
    rs = a_s.shape[0]
    tm = min(rp, 1024)
    npt = rp // tm

    def w_spec(arr, lead, first):
        if lead is None:
            return pl.BlockSpec((k, tn), lambda j, m: (0, first + j))
        return pl.BlockSpec((None, k, tn), lambda j, m: (lead, 0, first + j))

    def e_specs(first):
        return [pl.BlockSpec((tm, tn), lambda j, m: (jnp.maximum(m - 1, 0), first + j)),
                pl.BlockSpec((rs, tn), lambda j, m: (0, first + j))]

    p_row = lambda j, m: (jnp.maximum(m - 1, 0), 0)
    p_out = lambda j, m: (jnp.maximum(m - 1, 0), j)
    e_in = [arr for e in extras for arr in e[:2]]
    return pl.pallas_call(
        functools.partial(_mm_kernel, epi=epi, n_w=len(ws), n_e=len(extras)),
        grid=(n // tn, npt + 1),
        in_specs=[pl.BlockSpec((tm, k), p_row), pl.BlockSpec((rs, k), lambda j, m: (0, 0))]
        + [w_spec(*w) for w in ws] + [s for e in extras for s in e_specs(e[2])],
        out_specs=(pl.BlockSpec((tm, tn), p_out), pl.BlockSpec((rs, tn), lambda j, m: (0, j))),
        out_shape=(jax.ShapeDtypeStruct((rp, n), out_dtype), jax.ShapeDtypeStruct((rs, n), out_dtype)),
        scratch_shapes=[pltpu.VMEM((k, tn), BF16) for _ in ws],
        compiler_params=_params("parallel", "arbitrary"),
        name=name,
    )(a_p, a_s, *[w[0] for w in ws], *e_in)


def _conv_a_kernel(u_ref, prev_ref, hist_ref, w_ref, b_ref, g_ref, be_ref, o_ref, ext_ref, sh_ref, a_ref, *, tt):
    j = pl.program_id(1)
    ext_ref[0:CONV_HALO, :] = jnp.where(j == 0, hist_ref[...], prev_ref[...])
    ext_ref[CONV_HALO:, :] = u_ref[...]
    sh_rows = sh_ref.shape[1]
    for s in range(1, SUBLANES):
        sh_ref[s - 1] = ext_ref[s:s + sh_rows, :]
    rc = 32
    cc = 512
    lead = CONV_HALO - (CONV_K - 1)

    def body(r, carry):
        r0 = pl.multiple_of(r * rc, rc)
        for c0 in range(0, D_CONV, cc):
            acc = jnp.zeros((rc, cc), F32)
            for k in range(CONV_K):
                s = (lead + k) % SUBLANES
                row = pl.multiple_of(r0 + (lead + k - s), SUBLANES)
                if s == 0:
                    x = ext_ref[pl.ds(row, rc), c0:c0 + cc]
                else:
                    x = sh_ref[s - 1, pl.ds(row, rc), c0:c0 + cc]
                acc = acc + w_ref[k:k + 1, c0:c0 + cc] * x
            a_ref[pl.ds(r0, rc), c0:c0 + cc] = acc
        return carry

    lax.fori_loop(0, tt // rc, body, 0)
    a = a_ref[...] + b_ref[...]
    mu = jnp.mean(a, axis=-1, keepdims=True)
    ac = a - mu
    var = jnp.mean(ac * ac, axis=-1, keepdims=True)
    y = ac * lax.rsqrt(var + EPS) * g_ref[...] + be_ref[...]
    o_ref[...] = _silu(y).astype(o_ref.dtype)


def _conv_a(u, hist, w_dw, b_dw, ln_g, ln_b):
    b, t, c = u.shape
    tt = min(t, 256)
    hb = tt // CONV_HALO
    vec = lambda i, j: (0, 0)
    return pl.pallas_call(
        functools.partial(_conv_a_kernel, tt=tt),
        grid=(b, t // tt),
        in_specs=[pl.BlockSpec((None, tt, c), lambda i, j: (i, j, 0)),
                  pl.BlockSpec((None, CONV_HALO, c), lambda i, j: (i, jnp.maximum(j * hb - 1, 0), 0)),
                  pl.BlockSpec((None, CONV_HALO, c), lambda i, j: (i, 0, 0)),
                  pl.BlockSpec((CONV_K, c), vec),
                  pl.BlockSpec((1, c), vec), pl.BlockSpec((1, c), vec), pl.BlockSpec((1, c), vec)],
        out_specs=pl.BlockSpec((None, tt, c), lambda i, j: (i, j, 0)),
        out_shape=jax.ShapeDtypeStruct((b, t, c), BF16),
        scratch_shapes=[pltpu.VMEM((tt + CONV_HALO, c), F32),
                        pltpu.VMEM((SUBLANES - 1, tt + CONV_HALO - SUBLANES, c), F32),
                        pltpu.VMEM((tt, c), F32)],
        compiler_params=_params("parallel", "parallel"),
        name="conv_a",
    )(u, u, hist, w_dw, b_dw.reshape(1, c), ln_g.reshape(1, c), ln_b.reshape(1, c))


def _pool_kernel(z_ref, prev_ref, hist_ref, w_ref, s_ref, gate_ref, o_ref, ext_ref, *, tt, pos0):
    j = pl.program_id(1)
    ext_ref[0:POOL_HALO, :] = jnp.where(j == 0, hist_ref[...], prev_ref[...])
    ext_ref[POOL_HALO:, :] = z_ref[...]
    pos = pos0 + j * tt + lax.broadcasted_iota(jnp.int32, (tt, 1), 0)
    for g, w in enumerate(POOL_WINDOWS):
        c0 = g * POOL_GROUP
        cur = ext_ref[:, c0:c0 + POOL_GROUP]
        sh = 1
        while sh < w:
            cur = cur + pltpu.roll(cur, sh, 0)
            sh *= 2
        win = cur[POOL_HALO:, :]
        cnt = jnp.minimum(w, pos + 1).astype(F32)
        m = (win / cnt - ext_ref[POOL_HALO:, c0:c0 + POOL_GROUP]).astype(BF16)
        cols = slice(g * POOL_OUT, (g + 1) * POOL_OUT)
        out = jnp.dot(m, w_ref[g], preferred_element_type=F32) * s_ref[:, cols]
        o_ref[:, cols] = (out * gate_ref[:, cols].astype(F32)).astype(o_ref.dtype)


def _pool(zmid, hist, pos0, w_pool, l, pool_scale, gates):
    b, t, _ = zmid.shape
    c = D_POOL
    tt = min(t, 512)
    hb = tt // POOL_HALO
    return pl.pallas_call(
        functools.partial(_pool_kernel, tt=tt, pos0=pos0),
        grid=(b, t // tt),
        in_specs=[pl.BlockSpec((None, tt, c), lambda i, j: (i, j, 0)),
                  pl.BlockSpec((None, POOL_HALO, c), lambda i, j: (i, jnp.maximum(j * hb - 1, 0), 0)),
                  pl.BlockSpec((None, POOL_HALO, c), lambda i, j: (i, 0, 0)),
                  pl.BlockSpec((None, N_POOL_GROUPS, POOL_GROUP, POOL_OUT), lambda i, j: (l, 0, 0, 0)),
                  pl.BlockSpec((1, D_MODEL), lambda i, j: (0, 0)),
                  pl.BlockSpec((None, tt, D_MODEL), lambda i, j: (i, j, 1))],
        out_specs=pl.BlockSpec((None, tt, D_MODEL), lambda i, j: (i, j, 0)),
        out_shape=jax.ShapeDtypeStruct((b, t, D_MODEL), BF16),
        scratch_shapes=[pltpu.VMEM((tt + POOL_HALO, c), F32)],
        compiler_params=_params("parallel", "parallel"),
        name="pool",
    )(zmid, zmid, hist, w_pool, pool_scale.reshape(1, D_MODEL), gates)


def _rot_half(x):
    lane = lax.broadcasted_iota(jnp.int32, x.shape, 1)
    first = (lane % ROPE_DIM) < (ROPE_DIM // 2)
    return jnp.where(first, pltpu.roll(x, LANES - ROPE_DIM // 2, 1), pltpu.roll(x, ROPE_DIM // 2, 1))


def _lat_kernel(z_ref, gq_ref, gkv_ref, cos_ref, sin_ref, ql_ref, ckv_ref, ckvb_ref, kr_ref, krb_ref):
    nb, tt, _ = z_ref.shape
    zq = z_ref[:, :, MID_Q:MID_KV]
    ql = zq * lax.rsqrt(jnp.mean(zq * zq, axis=-1, keepdims=True) + EPS) * gq_ref[...]
    ql_ref[...] = ql.astype(ql_ref.dtype)
    zkv = z_ref[:, :, MID_KV:MID_R]
    ckv = zkv * lax.rsqrt(jnp.mean(zkv * zkv, axis=-1, keepdims=True) + EPS) * gkv_ref[...]
    ckv_ref[...] = ckv
    ckvb_ref[...] = ckv.astype(BF16)
    zr = z_ref[:, :, MID_R:MID_R + LANES]
    rot = _rot_half(zr.reshape(nb * tt, LANES)).reshape(nb, tt, LANES)
    kr = zr * cos_ref[...] + rot * sin_ref[...]
    kr_ref[...] = kr[:, :, :ROPE_DIM]
    krb_ref[...] = kr.astype(BF16)


def _lat(zmid, g_q, g_kv, cos, sin):
    b, t, _ = zmid.shape
    nb, tt = _row_tiles(b, t)
    row = lambda i, j: (i, j, 0)
    vec = lambda i, j: (0, 0)
    shapes = (jax.ShapeDtypeStruct((b, t, Q_RANK), BF16),
              jax.ShapeDtypeStruct((b, t, KV_RANK), F32),
              jax.ShapeDtypeStruct((b, t, KV_RANK), BF16),
              jax.ShapeDtypeStruct((b, t, ROPE_DIM), F32),
              jax.ShapeDtypeStruct((b, t, LANES), BF16))
    return pl.pallas_call(
        _lat_kernel,
        grid=(b // nb, t // tt),
        in_specs=[pl.BlockSpec((nb, tt, MID_W), row),
                  pl.BlockSpec((1, Q_RANK), vec), pl.BlockSpec((1, KV_RANK), vec),
                  pl.BlockSpec((tt, LANES), lambda i, j: (j, 0)),
                  pl.BlockSpec((tt, LANES), lambda i, j: (j, 0))],
        out_specs=(pl.BlockSpec((nb, tt, Q_RANK), row), pl.BlockSpec((nb, tt, KV_RANK), row),
                   pl.BlockSpec((nb, tt, KV_RANK), row), pl.BlockSpec((nb, tt, ROPE_DIM), row),
                   pl.BlockSpec((nb, tt, LANES), row)),
        out_shape=shapes,
        compiler_params=_params("parallel", "parallel"),
        name="lat",
    )(zmid, g_q.reshape(1, Q_RANK), g_kv.reshape(1, KV_RANK), cos, sin)


def _qprep_kernel(q_ref, wuk_ref, cos_ref, sin_ref, qa_ref, qr_ref):
    nb, tt, _ = q_ref.shape
    rows = nb * tt
    nope = N_HEADS * NOPE_DIM
    for h in range(N_HEADS):
        qn = q_ref[:, :, h * NOPE_DIM:(h + 1) * NOPE_DIM].reshape(rows, NOPE_DIM).astype(BF16)
        qa = jnp.dot(qn, wuk_ref[h], preferred_element_type=F32) * ATTN_SCALE
        qa_ref[h] = qa.reshape(nb, tt, KV_RANK).astype(qa_ref.dtype)
    cos = cos_ref[...]
    sin = sin_ref[...]
    for c in range(N_HEADS * ROPE_DIM // LANES):
        x = q_ref[:, :, nope + c * LANES:nope + (c + 1) * LANES]
        rot = _rot_half(x.reshape(rows, LANES)).reshape(nb, tt, LANES)
        r = ((x * cos + rot * sin) * ATTN_SCALE).astype(qr_ref.dtype)
        qr_ref[2 * c] = r[:, :, :ROPE_DIM]
        qr_ref[2 * c + 1] = r[:, :, ROPE_DIM:]


def _qprep(q, w_ukt, l, cos, sin):
    b, t, qw = q.shape
    nb, tt = (1, 256) if t >= 256 else (b, t)
    return pl.pallas_call(
        _qprep_kernel,
        grid=(b // nb, t // tt),
        in_specs=[pl.BlockSpec((nb, tt, qw), lambda i, j: (i, j, 0)),
                  pl.BlockSpec((None, N_HEADS, NOPE_DIM, KV_RANK), lambda i, j: (l, 0, 0, 0)),
                  pl.BlockSpec((tt, LANES), lambda i, j: (j, 0)),
                  pl.BlockSpec((tt, LANES), lambda i, j: (j, 0))],
        out_specs=(pl.BlockSpec((N_HEADS, nb, tt, KV_RANK), lambda i, j: (0, i, j, 0)),
                   pl.BlockSpec((N_HEADS, nb, tt, ROPE_DIM), lambda i, j: (0, i, j, 0))),
        out_shape=(jax.ShapeDtypeStruct((N_HEADS, b, t, KV_RANK), BF16),
                   jax.ShapeDtypeStruct((N_HEADS, b, t, ROPE_DIM), BF16)),
        compiler_params=_params("parallel", "parallel"),
        name="qprep",
    )(q, w_ukt, cos, sin)


def _kvcat_kernel(ckv_ref, kr_ref, wuk_ref, wuv_ref, k_ref, v_ref):
    c = ckv_ref[...]
    kn = jnp.dot(c, wuk_ref[...], preferred_element_type=F32)
    vv = jnp.dot(c, wuv_ref[...], preferred_element_type=F32)
    kr = kr_ref[...]
    one_col = (lax.broadcasted_iota(jnp.int32, kr.shape, 1) == 0).astype(v_ref.dtype)
    for h in range(N_HEADS):
        k_ref[h, :, :NOPE_DIM] = kn[:, h * NOPE_DIM:(h + 1) * NOPE_DIM].astype(k_ref.dtype)
        k_ref[h, :, NOPE_DIM:] = kr
        v_ref[h, :, :V_DIM] = vv[:, h * V_DIM:(h + 1) * V_DIM].astype(v_ref.dtype)
        v_ref[h, :, V_DIM:] = one_col


def _kvcat(ckv_b, krope_b, w_uk2, w_uv2, l):
    b, t, _ = ckv_b.shape
    tt = 512
    return pl.pallas_call(
        _kvcat_kernel,
        grid=(b, t // tt),
        in_specs=[pl.BlockSpec((None, tt, KV_RANK), lambda i, j: (i, j, 0)),
                  pl.BlockSpec((None, tt, LANES), lambda i, j: (i, j, 0)),
                  pl.BlockSpec((None, KV_RANK, N_HEADS * NOPE_DIM), lambda i, j: (l, 0, 0)),
                  pl.BlockSpec((None, KV_RANK, N_HEADS * V_DIM), lambda i, j: (l, 0, 0))],
        out_specs=(pl.BlockSpec((None, N_HEADS, tt, QK_DIM), lambda i, j: (i, 0, j, 0)),
                   pl.BlockSpec((None, N_HEADS, tt, V_AUG), lambda i, j: (i, 0, j, 0))),
        out_shape=(jax.ShapeDtypeStruct((b, N_HEADS, t, QK_DIM), BF16),
                   jax.ShapeDtypeStruct((b, N_HEADS, t, V_AUG), BF16)),
        compiler_params=_params("parallel", "parallel"),
        name="kvcat",
    )(ckv_b, krope_b, w_uk2, w_uv2)


def _mha_kernel(qn_ref, qr_ref, cos_ref, sin_ref, k_ref, v_ref, o_ref, q_ref, m_ref, acc_ref, *, tb, gh):
    i = pl.program_id(2)
    dn = (((1,), (1,)), ((), ()))
    m_ref[...] = jnp.full(m_ref.shape, NEG, F32)
    acc_ref[...] = jnp.zeros(acc_ref.shape, F32)
    scale = ATTN_SCALE * LOG2E
    low = lax.broadcasted_iota(jnp.int32, (tb, LANES), 1) < ROPE_DIM
    for c in range(gh * ROPE_DIM // LANES):
        x = qr_ref[:, c * LANES:(c + 1) * LANES]
        r = (x * cos_ref[...] + _rot_half(x) * sin_ref[...]) * scale
        q_ref[2 * c, :, NOPE_DIM:] = jnp.where(low, r, 0.0).astype(q_ref.dtype)
        q_ref[2 * c + 1, :, NOPE_DIM:] = jnp.where(low, pltpu.roll(r, ROPE_DIM, 1), 0.0).astype(q_ref.dtype)
    for g in range(gh):
        q_ref[g, :, :NOPE_DIM] = (qn_ref[:, g * NOPE_DIM:(g + 1) * NOPE_DIM] * scale).astype(q_ref.dtype)

    def block(j, masked):
        k0 = pl.multiple_of(j * tb, tb)
        if masked:
            qc = lax.broadcasted_iota(jnp.int32, (tb, tb), 0) >> CHUNK_SHIFT
            kc = lax.broadcasted_iota(jnp.int32, (tb, tb), 1) >> CHUNK_SHIFT
            bias = jnp.where(kc <= qc, 0.0, NEG)
        for g in range(gh):
            s = lax.dot_general(q_ref[g], k_ref[g, pl.ds(k0, tb), :], dn, preferred_element_type=F32)
            if masked:
                s = s + bias
            m_old = m_ref[g]
            m_new = jnp.maximum(m_old, jnp.max(s, axis=-1, keepdims=True))
            alpha = jnp.exp2(m_old - m_new)
            p = jnp.exp2(s - jnp.tile(m_new, (1, tb // LANES)))
            pv = jnp.dot(p.astype(BF16), v_ref[g, pl.ds(k0, tb), :], preferred_element_type=F32)
            acc_ref[g] = jnp.tile(alpha, (1, V_AUG // LANES)) * acc_ref[g] + pv
            m_ref[g] = m_new

    def full_block(j, carry):
        block(j, False)
        return carry

    lax.fori_loop(0, i, full_block, 0)
    block(i, True)
    for g in range(gh):
        acc = acc_ref[g]
        o_ref[:, g * V_DIM:(g + 1) * V_DIM] = (acc[:, :V_DIM] / acc[:, V_DIM:V_DIM + 1]).astype(o_ref.dtype)


def _mha(q, cos, sin, kc, vc):
    b, t, _ = q.shape
    nh = kc.shape[1]
    tb = 512
    gh = 8
    rope0 = nh * NOPE_DIM // (gh * ROPE_DIM)
    resident = dict(pipeline_mode=pl.Buffered(1))
    return pl.pallas_call(
        functools.partial(_mha_kernel, tb=tb, gh=gh),
        grid=(b, nh // gh, t // tb),
        in_specs=[pl.BlockSpec((None, tb, gh * NOPE_DIM), lambda bi, hg, i: (bi, i, hg)),
                  pl.BlockSpec((None, tb, gh * ROPE_DIM), lambda bi, hg, i: (bi, i, rope0 + hg)),
                  pl.BlockSpec((tb, LANES), lambda bi, hg, i: (i, 0)),
                  pl.BlockSpec((tb, LANES), lambda bi, hg, i: (i, 0)),
                  pl.BlockSpec((None, gh, t, QK_DIM), lambda bi, hg, i: (bi, hg, 0, 0), **resident),
                  pl.BlockSpec((None, gh, t, V_AUG), lambda bi, hg, i: (bi, hg, 0, 0), **resident)],
        out_specs=pl.BlockSpec((None, tb, gh * V_DIM), lambda bi, hg, i: (bi, i, hg)),
        out_shape=jax.ShapeDtypeStruct((b, t, nh * V_DIM), BF16),
        scratch_shapes=[pltpu.VMEM((gh, tb, QK_DIM), BF16), pltpu.VMEM((gh, tb, LANES), F32),
                        pltpu.VMEM((gh, tb, V_AUG), F32)],
        compiler_params=_params("parallel", "parallel", "parallel"),
        name="mha",
    )(q, q, cos, sin, kc, vc)


def _attn_kernel(qa_ref, qr_ref, pk_ref, pkr_ref, nk_ref, nkr_ref, wuv_ref, o_ref, m_ref, l_ref, acc_ref,
                 *, tq, tk, n_past, past_len):
    i = pl.program_id(1)
    kk = pl.program_id(2)
    rows = N_HEADS * tq
    dn = (((1,), (1,)), ((), ()))

    @pl.when(kk == 0)
    def _():
        m_ref[...] = jnp.full(m_ref.shape, NEG, F32)
        l_ref[...] = jnp.zeros(l_ref.shape, F32)
        acc_ref[...] = jnp.zeros(acc_ref.shape, F32)

    def update(k, kr, ok):
        n = k.shape[0]
        s = lax.dot_general(qa_ref[...].reshape(rows, KV_RANK), k, dn, preferred_element_type=F32)
        s = s + lax.dot_general(qr_ref[...].reshape(rows, ROPE_DIM), kr, dn, preferred_element_type=F32)
        if ok is not None:
            s = jnp.where(ok[None], s.reshape(N_HEADS, tq, n), NEG).reshape(rows, n)
        m_old = m_ref[...]
        m_new = jnp.maximum(m_old, jnp.max(s, axis=-1, keepdims=True))
        alpha = jnp.exp(m_old - m_new)
        p = jnp.exp(s - m_new)
        l_ref[...] = alpha * l_ref[...] + jnp.sum(p, axis=-1, keepdims=True)
        acc_ref[...] = alpha * acc_ref[...] + jnp.dot(p.astype(BF16), k, preferred_element_type=F32)
        m_ref[...] = m_new

    @pl.when(kk < n_past)
    def _():
        ok = None
        if n_past * tk != past_len:
            ok = kk * tk + lax.broadcasted_iota(jnp.int32, (tq, tk), 1) < past_len
        update(pk_ref[...].astype(BF16), pkr_ref[...].astype(BF16), ok)

    @pl.when(kk == n_past)
    def _():
        n = nk_ref.shape[0]
        qpos = past_len + i * tq + lax.broadcasted_iota(jnp.int32, (tq, n), 0)
        kpos = past_len + lax.broadcasted_iota(jnp.int32, (tq, n), 1)
        update(nk_ref[...], nkr_ref[...], (kpos >> CHUNK_SHIFT) <= (qpos >> CHUNK_SHIFT))
        o_lat = (acc_ref[...] / l_ref[...]).astype(BF16).reshape(N_HEADS, tq, KV_RANK)
        for h in range(N_HEADS):
            o_h = jnp.dot(o_lat[h], wuv_ref[h], preferred_element_type=F32)
            o_ref[:, h * V_DIM:(h + 1) * V_DIM] = o_h.astype(o_ref.dtype)


def _attn(q_abs, q_rope, past_ckv, past_krope, l, new_ckv, new_krope, w_uv):
    _, b, t, _ = q_abs.shape
    past_len = past_ckv.shape[2]
    tq = min(t, 128)
    tk = next((c for c in (1024, 512) if past_len and past_len % c == 0), 512)
    n_past = -(-past_len // tk)
    pad = ((0, 0), (0, 0), (0, max(n_past, 1) * tk - past_len), (0, 0))
    past_ckv, past_krope = jnp.pad(past_ckv, pad), jnp.pad(past_krope, pad)
    past_idx = lambda bi, i, kk: (l, bi, jnp.minimum(kk, max(n_past - 1, 0)), 0)
    new_idx = lambda bi, i, kk: (bi, 0, 0)
    return pl.pallas_call(
        functools.partial(_attn_kernel, tq=tq, tk=tk, n_past=n_past, past_len=past_len),
        grid=(b, t // tq, n_past + 1),
        in_specs=[pl.BlockSpec((N_HEADS, None, tq, KV_RANK), lambda bi, i, kk: (0, bi, i, 0)),
                  pl.BlockSpec((N_HEADS, None, tq, ROPE_DIM), lambda bi, i, kk: (0, bi, i, 0)),
                  pl.BlockSpec((None, None, tk, KV_RANK), past_idx),
                  pl.BlockSpec((None, None, tk, ROPE_DIM), past_idx),
                  pl.BlockSpec((None, t, KV_RANK), new_idx),
                  pl.BlockSpec((None, t, ROPE_DIM), new_idx),
                  pl.BlockSpec((None, N_HEADS, KV_RANK, V_DIM), lambda bi, i, kk: (l, 0, 0, 0))],
        out_specs=pl.BlockSpec((None, tq, N_HEADS * V_DIM), lambda bi, i, kk: (bi, i, 0)),
        out_shape=jax.ShapeDtypeStruct((b, t, N_HEADS * V_DIM), BF16),
        scratch_shapes=[pltpu.VMEM((N_HEADS * tq, 1), F32), pltpu.VMEM((N_HEADS * tq, 1), F32),
                        pltpu.VMEM((N_HEADS * tq, KV_RANK), F32)],
        compiler_params=_params("parallel", "parallel", "arbitrary"),
        name="attn",
    )(q_abs, q_rope, past_ckv, past_krope, new_ckv, new_krope, w_uv)


def _norm_residual(y, x_ref, gate_ref, gn_ref, o_ref):
    nb, tt, d = x_ref.shape
    y = y.reshape(nb, tt, d)
    yn = y * lax.rsqrt(jnp.mean(y * y, axis=-1, keepdims=True) + EPS) * gn_ref[...]
    x_new = x_ref[...] + gate_ref[...] * yn
    o_ref[...] = x_new
    return x_new


def _mix_out_kernel(a_ref, w_ref, x_ref, gate_ref, gn_ref, g2n_ref, sc_ref, sh_ref, o_ref, h_ref,
                    acc0_ref, acc1_ref, *, nt):
    j = pl.program_id(1)
    nb, tt, _ = x_ref.shape
    accs = (acc0_ref, acc1_ref)

    def project(acc_ref):
        a = a_ref[...]
        acc_ref[...] = jnp.dot(a.reshape(nb * tt, a.shape[-1]), w_ref[...], preferred_element_type=F32)

    def finish(acc_ref):
        x_new = _norm_residual(acc_ref[...], x_ref, gate_ref, gn_ref, o_ref)
        hn = x_new * lax.rsqrt(jnp.mean(x_new * x_new, axis=-1, keepdims=True) + EPS) * g2n_ref[...]
        h_ref[...] = (hn * (1.0 + sc_ref[...]) + sh_ref[...]).astype(h_ref.dtype)

    @pl.when(j == 0)
    def _():
        project(accs[0])

    for parity in range(2):
        @pl.when(jnp.logical_and(jnp.logical_and(j > 0, j < nt), j % 2 == parity))
        def _():
            project(accs[parity])
            finish(accs[1 - parity])

    @pl.when(j == nt)
    def _():
        finish(accs[(nt - 1) % 2])


def _mix_out(merged, w, l, x, gate, g_norm, g_next, scale_next, shift_next):
    b, t, d = x.shape
    k = w.shape[1]
    nb, tt = _row_tiles(b, t)
    nt = t // tt
    done = lambda i, j: (i, jnp.maximum(j - 1, 0), 0)
    per_batch = pl.BlockSpec((nb, 1, d), lambda i, j: (i, 0, 0))
    vec = pl.BlockSpec((1, d), lambda i, j: (0, 0))
    return pl.pallas_call(
        functools.partial(_mix_out_kernel, nt=nt),
        grid=(b // nb, nt + 1),
        in_specs=[pl.BlockSpec((nb, tt, k), lambda i, j: (i, jnp.minimum(j, nt - 1), 0)),
                  pl.BlockSpec((None, k, d), lambda i, j: (l, 0, 0), pipeline_mode=pl.Buffered(1)),
                  pl.BlockSpec((nb, tt, d), done),
                  per_batch, vec, vec, per_batch, per_batch],
        out_specs=(pl.BlockSpec((nb, tt, d), done), pl.BlockSpec((nb, tt, d), done)),
        out_shape=(jax.ShapeDtypeStruct((b, t, d), F32), jax.ShapeDtypeStruct((b, t, d), BF16)),
        scratch_shapes=[pltpu.VMEM((nb * tt, d), F32), pltpu.VMEM((nb * tt, d), F32)],
        compiler_params=_params("parallel", "arbitrary"),
        name="mm_out",
    )(merged, w, x, gate, g_norm.reshape(1, d), g_next.reshape(1, d), scale_next, shift_next)


def _ffn_down_kernel(ug_ref, uv_ref, prev_ref, hist_ref, wd_ref, bd_ref, w_ref, x_ref, gate_ref, gn_ref, o_ref,
                     ext_ref, act_ref, acc_ref, *, nkb):
    j = pl.program_id(1)
    s = pl.program_id(2)
    nb, tt, _ = x_ref.shape
    lead = FFN_HALO - (FFN_K - 1)

    def activate(slot):
        ext_ref[:, 0:FFN_HALO, :] = jnp.where(j == 0, hist_ref[...], prev_ref[...].astype(F32))
        ext_ref[:, FFN_HALO:, :] = ug_ref[...].astype(F32)
        conv = bd_ref[...]
        for k in range(FFN_K):
            conv = conv + wd_ref[k:k + 1, :] * ext_ref[:, lead + k:lead + k + tt, :]
        act = (_silu(conv) * uv_ref[...].astype(F32)).astype(BF16)
        act_ref[slot] = act.reshape(nb * tt, act.shape[-1])

    def project(slot):
        acc_ref[...] += jnp.dot(act_ref[slot], w_ref[...], preferred_element_type=F32)

    @pl.when(s == 0)
    def _():
        acc_ref[...] = jnp.zeros(acc_ref.shape, F32)
        activate(0)

    @pl.when(jnp.logical_and(s > 0, s < nkb))
    def _():
        slot = s % 2
        activate(slot)
        project(1 - slot)

    @pl.when(s == nkb)
    def _():
        project((nkb - 1) % 2)
        _norm_residual(acc_ref[...], x_ref, gate_ref, gn_ref, o_ref)


def _ffn_down(up, hist, w_dw, b_dw, w, l, x, gate, g_norm):
    b, t, d = x.shape
    nb, tt = _row_tiles(b, t)
    tk = D_FF // 4
    nkb = D_FF // tk
    hb = tt // FFN_HALO
    full = lambda i, j, s: (i, j, 0)
    ka = lambda s: jnp.minimum(s, nkb - 1)
    kp = lambda s: jnp.maximum(s - 1, 0)
    return pl.pallas_call(
        functools.partial(_ffn_down_kernel, nkb=nkb),
        grid=(b // nb, t // tt, nkb + 1),
        in_specs=[pl.BlockSpec((nb, tt, tk), lambda i, j, s: (i, j, ka(s))),
                  pl.BlockSpec((nb, tt, tk), lambda i, j, s: (i, j, nkb + ka(s))),
                  pl.BlockSpec((nb, FFN_HALO, tk), lambda i, j, s: (i, jnp.maximum(j * hb - 1, 0), ka(s))),
                  pl.BlockSpec((nb, FFN_HALO, tk), lambda i, j, s: (i, 0, ka(s))),
                  pl.BlockSpec((FFN_K, tk), lambda i, j, s: (0, ka(s))),
                  pl.BlockSpec((1, tk), lambda i, j, s: (0, ka(s))),
                  pl.BlockSpec((None, tk, d), lambda i, j, s: (l, kp(s), 0)),
                  pl.BlockSpec((nb, tt, d), full),
                  pl.BlockSpec((nb, 1, d), lambda i, j, s: (i, 0, 0)),
                  pl.BlockSpec((1, d), lambda i, j, s: (0, 0))],
        out_specs=pl.BlockSpec((nb, tt, d), full),
        out_shape=jax.ShapeDtypeStruct((b, t, d), F32),
        scratch_shapes=[pltpu.VMEM((nb, tt + FFN_HALO, tk), F32), pltpu.VMEM((2, nb * tt, tk), BF16),
                        pltpu.VMEM((nb * tt, d), F32)],
        compiler_params=_params("parallel", "parallel", "arbitrary"),
        name="ffn_down",
    )(up, up, up, hist, w_dw, b_dw.reshape(1, D_FF), w, x, gate, g_norm.reshape(1, d))


def _rope_tables(pos0, t):
    half = ROPE_DIM // 2
    inv = ROPE_THETA ** (-jnp.arange(half, dtype=F32) / half)
    pos = (pos0 + jnp.arange(t, dtype=jnp.int32)).astype(F32)
    ang = pos[:, None] * inv[None, :]
    cos, sin = jnp.cos(ang), jnp.sin(ang)
    return (jnp.concatenate([cos, cos, cos, cos], axis=-1),
            jnp.concatenate([-sin, sin, -sin, sin], axis=-1))


def _front_pad(a, rows):
    return jnp.pad(a, ((0, 0), (rows - a.shape[1], 0), (0, 0)))


def _tail(hist, new, n, cols):
    keep = min(new.shape[1], n)
    return jnp.concatenate([hist, new[:, new.shape[1] - keep:, :cols].astype(F32)], axis=1)[:, -n:]


def _attend(g, l, q, ckv_b, krope_b, cos, sin, p):
    t = q.shape[1]
    if g["past_ckv"].shape[2] == 0 and t % 512 == 0:
        kc, vc = _kvcat(ckv_b, krope_b, p["w_uk2"], p["w_uv2"], l)
        return _mha(q, cos, sin, kc, vc)
    q_abs, q_rope = _qprep(q, p["w_ukt"], l, cos, sin)
    return _attn(q_abs, q_rope, g["past_ckv"], g["past_krope"], l, ckv_b, krope_b[:, :, :ROPE_DIM], p["w_uv"])


def _layer(groups, l, w, p):
    d = D_MODEL
    dims = [g["x"].shape[:2] for g in groups]
    flat = lambda arrs: [a.reshape(-1, a.shape[-1]) for a in arrs]
    unflat = lambda outs: [o.reshape(b, t, o.shape[-1]) for o, (b, t) in zip(outs, dims)]
    mods = [[g["mod"][:, i] for i in range(6)] for g in groups]
    tables = [_rope_tables(g["pos0"], t) for g, (_, t) in zip(groups, dims)]

    h = flat([_norm_mod(g["x"], p["g_pre_mix"], m[1], m[0]) for g, m in zip(groups, mods)])
    tn = 512
    u_a = unflat(_mm(*h, [(p["w_a"], l, 0), (p["w_a"], l, D_CONV // tn)], D_CONV, F32,
                     tn=tn, epi="glu", name="mm_glu"))
    zmid = unflat(_mm(*h, [(p["w_mid"], l, 0)], MID_W, F32, tn=MID_TN, name="mm_mid"))
    gates2 = _mm(*h, [(p["w_g"], l, 0)], 3 * d, BF16, tn=1024, epi="sigmoid", name="mm_gates")
    gates = unflat(gates2)

    a_act = [_conv_a(u, _front_pad(g["hist_conv"], CONV_HALO), p["w_dwa"], p["b_dwa"], p["ln_a_g"], p["ln_a_b"])
             for u, g in zip(u_a, groups)]
    out_a = _mm(*flat(a_act), [(w["w_pa"], l, 0)], d, BF16, tn=tn, epi="mul", extras=[(*gates2, 0)], name="mm_pa")

    out_b = flat([_pool(z, _front_pad(g["hist_pool"], POOL_HALO), g["pos0"], p["w_pool"], l, p["pool_scale"], gt)
                  for z, g, gt in zip(zmid, groups, gates)])

    lat = [_lat(z, p["g_q_lat"], p["g_kv_lat"], cos, sin) for z, (cos, sin) in zip(zmid, tables)]
    q = unflat(_mm(*flat([o[0] for o in lat]), [(p["w_uq"], l, 0)], p["w_uq"].shape[2], F32, tn=tn, name="mm_uq"))
    o = [_attend(g, l, qg, lg[2], lg[4], cos, sin, p) for g, qg, lg, (cos, sin) in zip(groups, q, lat, tables)]
    merged = unflat(_mm(*flat(o), [(w["w_oc"], l, 0)], d, BF16, tn=tn, epi="mul_add2",
                        extras=[(*gates2, 2 * d // tn), (*out_a, 0), (*out_b, 0)], name="mm_oc"))

    mixed = [_mix_out(mg, p["w_out"], l, g["x"], m[2], p["g_post_mix"], p["g_pre_ffn"], m[4], m[3])
             for mg, g, m in zip(merged, groups, mods)]
    x = [xh[0] for xh in mixed]

    h2 = flat([xh[1] for xh in mixed])
    up = unflat(_mm(*h2, [(w["w_up"], l, 0)], 2 * D_FF, BF16, tn=1024, name="mm_up"))
    x = [_ffn_down(u, _front_pad(g["hist_ffn"], FFN_HALO), p["w_dwf"], p["b_dwf"], p["w_down"], l, xg, m[5],
                   p["g_post_ffn"]) for u, g, xg, m in zip(up, groups, x, mods)]

    states = [(lg[1], lg[3],
               _tail(g["hist_conv"], u, CONV_K - 1, D_CONV),
               _tail(g["hist_pool"], z, POOL_MAX - 1, D_POOL),
               _tail(g["hist_ffn"], uu, FFN_K - 1, D_FF))
              for lg, g, u, z, uu in zip(lat, groups, u_a, zmid, up)]
    return x, states


_SMALL = ("g_pre_mix", "g_post_mix", "w_dwa", "b_dwa", "ln_a_g", "ln_a_b", "pool_scale", "g_q_lat",
          "g_kv_lat", "g_pre_ffn", "g_post_ffn", "w_dwf", "b_dwf")


def _prep_weights(w):
    nl = w["w_in"].shape[0]
    w_in = w["w_in"]
    w_uq = w["w_uq"].reshape(nl, Q_RANK, N_HEADS, NOPE_DIM + ROPE_DIM)
    w_uq = jnp.concatenate([w_uq[..., :NOPE_DIM].reshape(nl, Q_RANK, -1),
                            w_uq[..., NOPE_DIM:].reshape(nl, Q_RANK, -1)], axis=2)
    mid_pad = ((0, 0), (0, 0), (0, MID_W - (OFF_G - OFF_B)))
    return dict(
        w_a=w_in[:, :, :OFF_B].astype(BF16),
        w_mid=jnp.pad(w_in[:, :, OFF_B:OFF_G], mid_pad).astype(BF16),
        w_g=w_in[:, :, OFF_G:].astype(BF16),
        w_pool=w["w_pool"].astype(BF16),
        w_uq=w_uq.astype(BF16),
        w_ukt=jnp.transpose(w["w_uk"], (0, 2, 3, 1)).astype(BF16),
        w_uv=jnp.transpose(w["w_uv"], (0, 2, 1, 3)).astype(BF16),
        w_uk2=w["w_uk"].reshape(nl, KV_RANK, N_HEADS * NOPE_DIM).astype(BF16),
        w_uv2=w["w_uv"].reshape(nl, KV_RANK, N_HEADS * V_DIM).astype(BF16),
        w_out=w["w_out"].astype(BF16), w_down=w["w_down"].astype(BF16),
    )


def kernel(x_prompt, x_sample, cache_ckv, cache_krope, state_conv, state_pool, state_ffn, c_prompt, c_sample, w_mod, b_mod, g_pre_mix, g_post_mix, w_in, w_dwa, b_dwa, ln_a_g, ln_a_b, w_pa, w_pool, pool_scale, g_q_lat, g_kv_lat, w_uq, w_uk, w_uv, w_oc, w_out, g_pre_ffn, g_post_ffn, w_up, w_dwf, b_dwf, w_down):
    weights = dict(g_pre_mix=g_pre_mix, g_post_mix=g_post_mix, w_in=w_in, w_dwa=w_dwa, b_dwa=b_dwa,
                   ln_a_g=ln_a_g, ln_a_b=ln_a_b, w_pa=w_pa, w_pool=w_pool, pool_scale=pool_scale,
                   g_q_lat=g_q_lat, g_kv_lat=g_kv_lat, w_uq=w_uq, w_uk=w_uk, w_uv=w_uv, w_oc=w_oc,
                   w_out=w_out, g_pre_ffn=g_pre_ffn, g_post_ffn=g_post_ffn, w_up=w_up, w_dwf=w_dwf,
                   b_dwf=b_dwf, w_down=w_down)
    depth = w_mod.shape[0]
    bp, bs = x_prompt.shape[0], x_sample.shape[0]
    past_len = cache_ckv.shape[2]
    d = x_prompt.shape[-1]

    rows = -(-(bp + bs) // SUBLANES) * SUBLANES
    c_all = jnp.pad(jnp.concatenate([c_prompt, c_sample], axis=0), ((0, rows - bp - bs), (0, 0)))
    mod_all = _mod(c_all, w_mod, b_mod)

    xp, xs = x_prompt, x_sample
    st_p = [[] for _ in range(5)]
    st_s = [[] for _ in range(5)]
    stacked = _prep_weights(weights)
    for l in range(depth):
        p = dict(stacked, **{name: weights[name][l] for name in _SMALL})
        prompt = dict(x=xp, mod=mod_all[l, :bp].reshape(bp, 6, 1, d), pos0=0,
                      past_ckv=jnp.zeros((depth, bp, 0, KV_RANK), F32),
                      past_krope=jnp.zeros((depth, bp, 0, ROPE_DIM), F32),
                      hist_conv=jnp.zeros((bp, CONV_K - 1, D_CONV), F32),
                      hist_pool=jnp.zeros((bp, POOL_MAX - 1, D_POOL), F32),
                      hist_ffn=jnp.zeros((bp, FFN_K - 1, D_FF), F32))
        sample = dict(x=xs, mod=mod_all[l, bp:bp + bs].reshape(bs, 6, 1, d), pos0=past_len,
                      past_ckv=cache_ckv, past_krope=cache_krope, hist_conv=state_conv[l],
                      hist_pool=state_pool[l], hist_ffn=state_ffn[l])
        (xp, xs), (sp, ss) = _layer((prompt, sample), l, weights, p)
        for i in range(5):
            st_p[i].append(sp[i])
            st_s[i].append(ss[i])
    return (xp, xs) + tuple(jnp.stack(s) for s in st_p) + tuple(jnp.stack(s) for s in st_s)
```

```python
import functools

import jax
import jax.numpy as jnp
from jax import lax
from jax.experimental import pallas as pl
from jax.experimental.pallas import tpu as pltpu

F32 = jnp.float32
BF16 = jnp.bfloat16

D_MODEL = 2048
CHUNK = 64
CHUNK_SHIFT = 6
assert 1 << CHUNK_SHIFT == CHUNK
D_CONV = D_MODEL // 2
CONV_K = 31
D_POOL = D_MODEL // 2
POOL_WINDOWS = (2, 4, 8, 16)
POOL_MAX = 16
N_POOL_GROUPS = 4
POOL_GROUP = D_POOL // N_POOL_GROUPS
POOL_OUT = D_MODEL // N_POOL_GROUPS
N_HEADS = D_MODEL // 128
NOPE_DIM = 128
ROPE_DIM = 64
V_DIM = 128
Q_RANK = D_MODEL // 4
KV_RANK = D_MODEL // 4
ROPE_THETA = 10000.0
ATTN_SCALE = (NOPE_DIM + ROPE_DIM) ** -0.5
LOG2E = 1.4426950408889634
QK_DIM = 256
V_AUG = 256
D_FF = 256 * ((8 * D_MODEL // 3 + 255) // 256)
FFN_K = 3
EPS = 1e-6
NEG = -1e30
OFF_B = 2 * D_CONV
OFF_R = OFF_B + D_POOL + Q_RANK + KV_RANK
OFF_G = OFF_R + ROPE_DIM

LANES = 128
SUBLANES = 8
VMEM_LIMIT_BYTES = 56 * 1024 * 1024

MID_Q = D_POOL
MID_KV = MID_Q + Q_RANK
MID_R = MID_KV + KV_RANK
MID_W = MID_R + 2 * LANES
MID_TN = MID_W // 3
CONV_HALO = 32
POOL_HALO = 16
FFN_HALO = 8


def _params(*sem):
    return pltpu.CompilerParams(dimension_semantics=sem, vmem_limit_bytes=VMEM_LIMIT_BYTES)


def _sigmoid(x):
    return 0.5 + 0.5 * jnp.tanh(0.5 * x)


def _silu(x):
    h = 0.5 * x
    return h + h * jnp.tanh(h)


def _row_tiles(b, t):
    if t >= 512:
        return 1, 512
    return b, t


def _mod_kernel(c_ref, w_ref, b_ref, o_ref):
    c = c_ref[...]
    a = _silu(c).astype(BF16)
    o_ref[...] = jnp.dot(a, w_ref[...].astype(BF16), preferred_element_type=F32) + b_ref[...]


def _mod(c_all, w_mod, b_mod):
    nl, d, n = w_mod.shape
    bp = c_all.shape[0]
    tn = 1024
    return pl.pallas_call(
        _mod_kernel,
        grid=(nl, n // tn),
        in_specs=[pl.BlockSpec((bp, d), lambda l, j: (0, 0)),
                  pl.BlockSpec((None, d, tn), lambda l, j: (l, 0, j)),
                  pl.BlockSpec((None, 1, tn), lambda l, j: (l, 0, j))],
        out_specs=pl.BlockSpec((None, bp, tn), lambda l, j: (l, 0, j)),
        out_shape=jax.ShapeDtypeStruct((nl, bp, n), F32),
        compiler_params=_params("parallel", "parallel"),
        name="mod",
    )(c_all, w_mod, b_mod.reshape(nl, 1, n))


def _norm_mod_kernel(x_ref, g_ref, sc_ref, sh_ref, o_ref):
    x = x_ref[...]
    y = x * lax.rsqrt(jnp.mean(x * x, axis=-1, keepdims=True) + EPS) * g_ref[...]
    o_ref[...] = (y * (1.0 + sc_ref[...]) + sh_ref[...]).astype(o_ref.dtype)


def _norm_mod(x, g, scale, shift):
    b, t, d = x.shape
    nb, tt = _row_tiles(b, t)
    return pl.pallas_call(
        _norm_mod_kernel,
        grid=(b // nb, t // tt),
        in_specs=[pl.BlockSpec((nb, tt, d), lambda i, j: (i, j, 0)),
                  pl.BlockSpec((1, d), lambda i, j: (0, 0)),
                  pl.BlockSpec((nb, 1, d), lambda i, j: (i, 0, 0)),
                  pl.BlockSpec((nb, 1, d), lambda i, j: (i, 0, 0))],
        out_specs=pl.BlockSpec((nb, tt, d), lambda i, j: (i, j, 0)),
        out_shape=jax.ShapeDtypeStruct((b, t, d), BF16),
        compiler_params=_params("parallel", "parallel"),
        name="norm_mod",
    )(x, g.reshape(1, d), scale, shift)


def _mm_kernel(*refs, epi, n_w, n_e):
    ap_ref, as_ref = refs[0], refs[1]
    w_refs = refs[2:2 + n_w]
    e_refs = refs[2 + n_w:2 + n_w + 2 * n_e]
    op_ref, os_ref = refs[2 + n_w + 2 * n_e], refs[3 + n_w + 2 * n_e]
    wb_refs = refs[4 + n_w + 2 * n_e:]
    m = pl.program_id(1)

    @pl.when(m == 0)
    def _():
        for w_ref, wb_ref in zip(w_refs, wb_refs):
            wb_ref[...] = w_ref[...].astype(BF16)

    def compute(a, es):
        z = jnp.dot(a, wb_refs[0][...], preferred_element_type=F32)
        if epi == "sigmoid":
            z = _sigmoid(z)
        elif epi == "glu":
            z = z * _sigmoid(jnp.dot(a, wb_refs[1][...], preferred_element_type=F32))
        elif epi == "mul":
            z = z * es[0][...].astype(F32)
        elif epi == "mul_add2":
            z = z * es[0][...].astype(F32) + es[1][...].astype(F32) + es[2][...].astype(F32)
        return z

    @pl.when(m == 0)
    def _():
        os_ref[...] = compute(as_ref[...], e_refs[1::2]).astype(os_ref.dtype)

    @pl.when(m > 0)
    def _():
        op_ref[...] = compute(ap_ref[...], e_refs[0::2]).astype(op_ref.dtype)


def _mm(a_p, a_s, ws, n, out_dtype, *, tn, epi="none", extras=(), name="mm"):
    rp, k = a_p.shape
    rs = a_s.shape[0]
    tm = min(rp, 1024)
    npt = rp // tm

    def w_spec(arr, lead, first):
        if lead is None:
            return pl.BlockSpec((k, tn), lambda j, m: (0, first + j))
        return pl.BlockSpec((None, k, tn), lambda j, m: (lead, 0, first + j))

    def e_specs(first):
        return [pl.BlockSpec((tm, tn), lambda j, m: (jnp.maximum(m - 1, 0), first + j)),
                pl.BlockSpec((rs, tn), lambda j, m: (0, first + j))]

    p_row = lambda j, m: (jnp.maximum(m - 1, 0), 0)
    p_out = lambda j, m: (jnp.maximum(m - 1, 0), j)
    e_in = [arr for e in extras for arr in e[:2]]
    return pl.pallas_call(
        functools.partial(_mm_kernel, epi=epi, n_w=len(ws), n_e=len(extras)),
        grid=(n // tn, npt + 1),
        in_specs=[pl.BlockSpec((tm, k), p_row), pl.BlockSpec((rs, k), lambda j, m: (0, 0))]
        + [w_spec(*w) for w in ws] + [s for e in extras for s in e_specs(e[2])],
        out_specs=(pl.BlockSpec((tm, tn), p_out), pl.BlockSpec((rs, tn), lambda j, m: (0, j))),
        out_shape=(jax.ShapeDtypeStruct((rp, n), out_dtype), jax.ShapeDtypeStruct((rs, n), out_dtype)),
        scratch_shapes=[pltpu.VMEM((k, tn), BF16) for _ in ws],
        compiler_params=_params("parallel", "arbitrary"),
        name=name,
    )(a_p, a_s, *[w[0] for w in ws], *e_in)


def _conv_a_kernel(u_ref, prev_ref, hist_ref, w_ref, b_ref, g_ref, be_ref, o_ref, ext_ref, sh_ref, a_ref, *, tt):
    j = pl.program_id(1)
    ext_ref[0:CONV_HALO, :] = jnp.where(j == 0, hist_ref[...], prev_ref[...])
    ext_ref[CONV_HALO:, :] = u_ref[...]
    sh_rows = sh_ref.shape[1]
    for s in range(1, SUBLANES):
        sh_ref[s - 1] = ext_ref[s:s + sh_rows, :]
    rc = 32
    cc = 512
    lead = CONV_HALO - (CONV_K - 1)

    def body(r, carry):
        r0 = pl.multiple_of(r * rc, rc)
        for c0 in range(0, D_CONV, cc):
            acc = jnp.zeros((rc, cc), F32)
            for k in range(CONV_K):
                s = (lead + k) % SUBLANES
                row = pl.multiple_of(r0 + (lead + k - s), SUBLANES)
                if s == 0:
                    x = ext_ref[pl.ds(row, rc), c0:c0 + cc]
                else:
                    x = sh_ref[s - 1, pl.ds(row, rc), c0:c0 + cc]
                acc = acc + w_ref[k:k + 1, c0:c0 + cc] * x
            a_ref[pl.ds(r0, rc), c0:c0 + cc] = acc
        return carry

    lax.fori_loop(0, tt // rc, body, 0)
    a = a_ref[...] + b_ref[...]
    mu = jnp.mean(a, axis=-1, keepdims=True)
    ac = a - mu
    var = jnp.mean(ac * ac, axis=-1, keepdims=True)
    y = ac * lax.rsqrt(var + EPS) * g_ref[...] + be_ref[...]
    o_ref[...] = _silu(y).astype(o_ref.dtype)


def _conv_a(u, hist, w_dw, b_dw, ln_g, ln_b):
    b, t, c = u.shape
    tt = min(t, 256)
    hb = tt // CONV_HALO
    vec = lambda i, j: (0, 0)
    return pl.pallas_call(
        functools.partial(_conv_a_kernel, tt=tt),
        grid=(b, t // tt),
        in_specs=[pl.BlockSpec((None, tt, c), lambda i, j: (i, j, 0)),
                  pl.BlockSpec((None, CONV_HALO, c), lambda i, j: (i, jnp.maximum(j * hb - 1, 0), 0)),
                  pl.BlockSpec((None, CONV_HALO, c), lambda i, j: (i, 0, 0)),
                  pl.BlockSpec((CONV_K, c), vec),
                  pl.BlockSpec((1, c), vec), pl.BlockSpec((1, c), vec), pl.BlockSpec((1, c), vec)],
        out_specs=pl.BlockSpec((None, tt, c), lambda i, j: (i, j, 0)),
        out_shape=jax.ShapeDtypeStruct((b, t, c), BF16),
        scratch_shapes=[pltpu.VMEM((tt + CONV_HALO, c), F32),
                        pltpu.VMEM((SUBLANES - 1, tt + CONV_HALO - SUBLANES, c), F32),
                        pltpu.VMEM((tt, c), F32)],
        compiler_params=_params("parallel", "parallel"),
        name="conv_a",
    )(u, u, hist, w_dw, b_dw.reshape(1, c), ln_g.reshape(1, c), ln_b.reshape(1, c))


def _pool_kernel(z_ref, prev_ref, hist_ref, w_ref, s_ref, gate_ref, o_ref, ext_ref, *, tt, pos0):
    j = pl.program_id(1)
    ext_ref[0:POOL_HALO, :] = jnp.where(j == 0, hist_ref[...], prev_ref[...])
    ext_ref[POOL_HALO:, :] = z_ref[...]
    pos = pos0 + j * tt + lax.broadcasted_iota(jnp.int32, (tt, 1), 0)
    for g, w in enumerate(POOL_WINDOWS):
        c0 = g * POOL_GROUP
        cur = ext_ref[:, c0:c0 + POOL_GROUP]
        sh = 1
        while sh < w:
            cur = cur + pltpu.roll(cur, sh, 0)
            sh *= 2
        win = cur[POOL_HALO:, :]
        cnt = jnp.minimum(w, pos + 1).astype(F32)
        m = (win / cnt - ext_ref[POOL_HALO:, c0:c0 + POOL_GROUP]).astype(BF16)
        cols = slice(g * POOL_OUT, (g + 1) * POOL_OUT)
        out = jnp.dot(m, w_ref[g], preferred_element_type=F32) * s_ref[:, cols]
        o_ref[:, cols] = (out * gate_ref[:, cols].astype(F32)).astype(o_ref.dtype)


def _pool(zmid, hist, pos0, w_pool, l, pool_scale, gates):
    b, t, _ = zmid.shape
    c = D_POOL
    tt = min(t, 512)
    hb = tt // POOL_HALO
    return pl.pallas_call(
        functools.partial(_pool_kernel, tt=tt, pos0=pos0),
        grid=(b, t // tt),
        in_specs=[pl.BlockSpec((None, tt, c), lambda i, j: (i, j, 0)),
                  pl.BlockSpec((None, POOL_HALO, c), lambda i, j: (i, jnp.maximum(j * hb - 1, 0), 0)),
                  pl.BlockSpec((None, POOL_HALO, c), lambda i, j: (i, 0, 0)),
                  pl.BlockSpec((None, N_POOL_GROUPS, POOL_GROUP, POOL_OUT), lambda i, j: (l, 0, 0, 0)),
                  pl.BlockSpec((1, D_MODEL), lambda i, j: (0, 0)),
                  pl.BlockSpec((None, tt, D_MODEL), lambda i, j: (i, j, 1))],
        out_specs=pl.BlockSpec((None, tt, D_MODEL), lambda i, j: (i, j, 0)),
        out_shape=jax.ShapeDtypeStruct((b, t, D_MODEL), BF16),
        scratch_shapes=[pltpu.VMEM((tt + POOL_HALO, c), F32)],
        compiler_params=_params("parallel", "parallel"),
        name="pool",
    )(zmid, zmid, hist, w_pool, pool_scale.reshape(1, D_MODEL), gates)


def _rot_half(x):
    lane = lax.broadcasted_iota(jnp.int32, x.shape, 1)
    first = (lane % ROPE_DIM) < (ROPE_DIM // 2)
    return jnp.where(first, pltpu.roll(x, LANES - ROPE_DIM // 2, 1), pltpu.roll(x, ROPE_DIM // 2, 1))


def _lat_kernel(z_ref, gq_ref, gkv_ref, cos_ref, sin_ref, ql_ref, ckv_ref, ckvb_ref, kr_ref, krb_ref):
    nb, tt, _ = z_ref.shape
    zq = z_ref[:, :, MID_Q:MID_KV]
    ql = zq * lax.rsqrt(jnp.mean(zq * zq, axis=-1, keepdims=True) + EPS) * gq_ref[...]
    ql_ref[...] = ql.astype(ql_ref.dtype)
    zkv = z_ref[:, :, MID_KV:MID_R]
    ckv = zkv * lax.rsqrt(jnp.mean(zkv * zkv, axis=-1, keepdims=True) + EPS) * gkv_ref[...]
    ckv_ref[...] = ckv
    ckvb_ref[...] = ckv.astype(BF16)
    zr = z_ref[:, :, MID_R:MID_R + LANES]
    rot = _rot_half(zr.reshape(nb * tt, LANES)).reshape(nb, tt, LANES)
    kr = zr * cos_ref[...] + rot * sin_ref[...]
    kr_ref[...] = kr[:, :, :ROPE_DIM]
    krb_ref[...] = kr.astype(BF16)


def _lat(zmid, g_q, g_kv, cos, sin):
    b, t, _ = zmid.shape
    nb, tt = _row_tiles(b, t)
    row = lambda i, j: (i, j, 0)
    vec = lambda i, j: (0, 0)
    shapes = (jax.ShapeDtypeStruct((b, t, Q_RANK), BF16),
              jax.ShapeDtypeStruct((b, t, KV_RANK), F32),
              jax.ShapeDtypeStruct((b, t, KV_RANK), BF16),
              jax.ShapeDtypeStruct((b, t, ROPE_DIM), F32),
              jax.ShapeDtypeStruct((b, t, LANES), BF16))
    return pl.pallas_call(
        _lat_kernel,
        grid=(b // nb, t // tt),
        in_specs=[pl.BlockSpec((nb, tt, MID_W), row),
                  pl.BlockSpec((1, Q_RANK), vec), pl.BlockSpec((1, KV_RANK), vec),
                  pl.BlockSpec((tt, LANES), lambda i, j: (j, 0)),
                  pl.BlockSpec((tt, LANES), lambda i, j: (j, 0))],
        out_specs=(pl.BlockSpec((nb, tt, Q_RANK), row), pl.BlockSpec((nb, tt, KV_RANK), row),
                   pl.BlockSpec((nb, tt, KV_RANK), row), pl.BlockSpec((nb, tt, ROPE_DIM), row),
                   pl.BlockSpec((nb, tt, LANES), row)),
        out_shape=shapes,
        compiler_params=_params("parallel", "parallel"),
        name="lat",
    )(zmid, g_q.reshape(1, Q_RANK), g_kv.reshape(1, KV_RANK), cos, sin)


def _qprep_kernel(q_ref, wuk_ref, cos_ref, sin_ref, qa_ref, qr_ref):
    nb, tt, _ = q_ref.shape
    rows = nb * tt
    nope = N_HEADS * NOPE_DIM
    for h in range(N_HEADS):
        qn = q_ref[:, :, h * NOPE_DIM:(h + 1) * NOPE_DIM].reshape(rows, NOPE_DIM).astype(BF16)
        qa = jnp.dot(qn, wuk_ref[h], preferred_element_type=F32) * ATTN_SCALE
        qa_ref[h] = qa.reshape(nb, tt, KV_RANK).astype(qa_ref.dtype)
    cos = cos_ref[...]
    sin = sin_ref[...]
    for c in range(N_HEADS * ROPE_DIM // LANES):
        x = q_ref[:, :, nope + c * LANES:nope + (c + 1) * LANES]
        rot = _rot_half(x.reshape(rows, LANES)).reshape(nb, tt, LANES)
        r = ((x * cos + rot * sin) * ATTN_SCALE).astype(qr_ref.dtype)
        qr_ref[2 * c] = r[:, :, :ROPE_DIM]
        qr_ref[2 * c + 1] = r[:, :, ROPE_DIM:]


def _qprep(q, w_ukt, l, cos, sin):
    b, t, qw = q.shape
    nb, tt = (1, 256) if t >= 256 else (b, t)
    return pl.pallas_call(
        _qprep_kernel,
        grid=(b // nb, t // tt),
        in_specs=[pl.BlockSpec((nb, tt, qw), lambda i, j: (i, j, 0)),
                  pl.BlockSpec((None, N_HEADS, NOPE_DIM, KV_RANK), lambda i, j: (l, 0, 0, 0)),
                  pl.BlockSpec((tt, LANES), lambda i, j: (j, 0)),
                  pl.BlockSpec((tt, LANES), lambda i, j: (j, 0))],
        out_specs=(pl.BlockSpec((N_HEADS, nb, tt, KV_RANK), lambda i, j: (0, i, j, 0)),
                   pl.BlockSpec((N_HEADS, nb, tt, ROPE_DIM), lambda i, j: (0, i, j, 0))),
        out_shape=(jax.ShapeDtypeStruct((N_HEADS, b, t, KV_RANK), BF16),
                   jax.ShapeDtypeStruct((N_HEADS, b, t, ROPE_DIM), BF16)),
        compiler_params=_params("parallel", "parallel"),
        name="qprep",
    )(q, w_ukt, cos, sin)


def _kvcat_kernel(ckv_ref, kr_ref, wuk_ref, wuv_ref, k_ref, v_ref):
    c = ckv_ref[...]
    kn = jnp.dot(c, wuk_ref[...], preferred_element_type=F32)
    vv = jnp.dot(c, wuv_ref[...], preferred_element_type=F32)
    kr = kr_ref[...]
    one_col = (lax.broadcasted_iota(jnp.int32, kr.shape, 1) == 0).astype(v_ref.dtype)
    for h in range(N_HEADS):
        k_ref[h, :, :NOPE_DIM] = kn[:, h * NOPE_DIM:(h + 1) * NOPE_DIM].astype(k_ref.dtype)
        k_ref[h, :, NOPE_DIM:] = kr
        v_ref[h, :, :V_DIM] = vv[:, h * V_DIM:(h + 1) * V_DIM].astype(v_ref.dtype)
        v_ref[h, :, V_DIM:] = one_col


def _kvcat(ckv_b, krope_b, w_uk2, w_uv2, l):
    b, t, _ = ckv_b.shape
    tt = 512
    return pl.pallas_call(
        _kvcat_kernel,
        grid=(b, t // tt),
        in_specs=[pl.BlockSpec((None, tt, KV_RANK), lambda i, j: (i, j, 0)),
                  pl.BlockSpec((None, tt, LANES), lambda i, j: (i, j, 0)),
                  pl.BlockSpec((None, KV_RANK, N_HEADS * NOPE_DIM), lambda i, j: (l, 0, 0)),
                  pl.BlockSpec((None, KV_RANK, N_HEADS * V_DIM), lambda i, j: (l, 0, 0))],
        out_specs=(pl.BlockSpec((None, N_HEADS, tt, QK_DIM), lambda i, j: (i, 0, j, 0)),
                   pl.BlockSpec((None, N_HEADS, tt, V_AUG), lambda i, j: (i, 0, j, 0))),
        out_shape=(jax.ShapeDtypeStruct((b, N_HEADS, t, QK_DIM), BF16),
                   jax.ShapeDtypeStruct((b, N_HEADS, t, V_AUG), BF16)),
        compiler_params=_params("parallel", "parallel"),
        name="kvcat",
    )(ckv_b, krope_b, w_uk2, w_uv2)


def _mha_kernel(qn_ref, qr_ref, cos_ref, sin_ref, k_ref, v_ref, o_ref, q_ref, m_ref, acc_ref, *, tb, gh):
    i = pl.program_id(2)
    dn = (((1,), (1,)), ((), ()))
    m_ref[...] = jnp.full(m_ref.shape, NEG, F32)
    acc_ref[...] = jnp.zeros(acc_ref.shape, F32)
    scale = ATTN_SCALE * LOG2E
    low = lax.broadcasted_iota(jnp.int32, (tb, LANES), 1) < ROPE_DIM
    for c in range(gh * ROPE_DIM // LANES):
        x = qr_ref[:, c * LANES:(c + 1) * LANES]
        r = (x * cos_ref[...] + _rot_half(x) * sin_ref[...]) * scale
        q_ref[2 * c, :, NOPE_DIM:] = jnp.where(low, r, 0.0).astype(q_ref.dtype)
        q_ref[2 * c + 1, :, NOPE_DIM:] = jnp.where(low, pltpu.roll(r, ROPE_DIM, 1), 0.0).astype(q_ref.dtype)
    for g in range(gh):
        q_ref[g, :, :NOPE_DIM] = (qn_ref[:, g * NOPE_DIM:(g + 1) * NOPE_DIM] * scale).astype(q_ref.dtype)

    def block(j, masked):
        k0 = pl.multiple_of(j * tb, tb)
        if masked:
            qc = lax.broadcasted_iota(jnp.int32, (tb, tb), 0) >> CHUNK_SHIFT
            kc = lax.broadcasted_iota(jnp.int32, (tb, tb), 1) >> CHUNK_SHIFT
            bias = jnp.where(kc <= qc, 0.0, NEG)
        for g in range(gh):
            s = lax.dot_general(q_ref[g], k_ref[g, pl.ds(k0, tb), :], dn, preferred_element_type=F32)
            if masked:
                s = s + bias
            m_old = m_ref[g]
            m_new = jnp.maximum(m_old, jnp.max(s, axis=-1, keepdims=True))
            alpha = jnp.exp2(m_old - m_new)
            p = jnp.exp2(s - jnp.tile(m_new, (1, tb // LANES)))
            pv = jnp.dot(p.astype(BF16), v_ref[g, pl.ds(k0, tb), :], preferred_element_type=F32)
            acc_ref[g] = jnp.tile(alpha, (1, V_AUG // LANES)) * acc_ref[g] + pv
            m_ref[g] = m_new

    def full_block(j, carry):
        block(j, False)
        return carry

    lax.fori_loop(0, i, full_block, 0)
    block(i, True)
    for g in range(gh):
        acc = acc_ref[g]
        o_ref[:, g * V_DIM:(g + 1) * V_DIM] = (acc[:, :V_DIM] / acc[:, V_DIM:V_DIM + 1]).astype(o_ref.dtype)


def _mha(q, cos, sin, kc, vc):
    b, t, _ = q.shape
    nh = kc.shape[1]
    tb = 512
    gh = 8
    rope0 = nh * NOPE_DIM // (gh * ROPE_DIM)
    resident = dict(pipeline_mode=pl.Buffered(1))
    return pl.pallas_call(
        functools.partial(_mha_kernel, tb=tb, gh=gh),
        grid=(b, nh // gh, t // tb),
        in_specs=[pl.BlockSpec((None, tb, gh * NOPE_DIM), lambda bi, hg, i: (bi, i, hg)),
                  pl.BlockSpec((None, tb, gh * ROPE_DIM), lambda bi, hg, i: (bi, i, rope0 + hg)),
                  pl.BlockSpec((tb, LANES), lambda bi, hg, i: (i, 0)),
                  pl.BlockSpec((tb, LANES), lambda bi, hg, i: (i, 0)),
                  pl.BlockSpec((None, gh, t, QK_DIM), lambda bi, hg, i: (bi, hg, 0, 0), **resident),
                  pl.BlockSpec((None, gh, t, V_AUG), lambda bi, hg, i: (bi, hg, 0, 0), **resident)],
        out_specs=pl.BlockSpec((None, tb, gh * V_DIM), lambda bi, hg, i: (bi, i, hg)),
        out_shape=jax.ShapeDtypeStruct((b, t, nh * V_DIM), BF16),
        scratch_shapes=[pltpu.VMEM((gh, tb, QK_DIM), BF16), pltpu.VMEM((gh, tb, LANES), F32),
                        pltpu.VMEM((gh, tb, V_AUG), F32)],
        compiler_params=_params("parallel", "parallel", "parallel"),
        name="mha",
    )(q, q, cos, sin, kc, vc)


def _attn_kernel(qa_ref, qr_ref, pk_ref, pkr_ref, nk_ref, nkr_ref, wuv_ref, o_ref, m_ref, l_ref, acc_ref,
                 *, tq, tk, n_past, past_len):
    i = pl.program_id(1)
    kk = pl.program_id(2)
    rows = N_HEADS * tq
    dn = (((1,), (1,)), ((), ()))

    @pl.when(kk == 0)
    def _():
        m_ref[...] = jnp.full(m_ref.shape, NEG, F32)
        l_ref[...] = jnp.zeros(l_ref.shape, F32)
        acc_ref[...] = jnp.zeros(acc_ref.shape, F32)

    def update(k, kr, ok):
        n = k.shape[0]
        s = lax.dot_general(qa_ref[...].reshape(rows, KV_RANK), k, dn, preferred_element_type=F32)
        s = s + lax.dot_general(qr_ref[...].reshape(rows, ROPE_DIM), kr, dn, preferred_element_type=F32)
        if ok is not None:
            s = jnp.where(ok[None], s.reshape(N_HEADS, tq, n), NEG).reshape(rows, n)
        m_old = m_ref[...]
        m_new = jnp.maximum(m_old, jnp.max(s, axis=-1, keepdims=True))
        alpha = jnp.exp(m_old - m_new)
        p = jnp.exp(s - m_new)
        l_ref[...] = alpha * l_ref[...] + jnp.sum(p, axis=-1, keepdims=True)
        acc_ref[...] = alpha * acc_ref[...] + jnp.dot(p.astype(BF16), k, preferred_element_type=F32)
        m_ref[...] = m_new

    @pl.when(kk < n_past)
    def _():
        ok = None
        if n_past * tk != past_len:
            ok = kk * tk + lax.broadcasted_iota(jnp.int32, (tq, tk), 1) < past_len
        update(pk_ref[...].astype(BF16), pkr_ref[...].astype(BF16), ok)

    @pl.when(kk == n_past)
    def _():
        n = nk_ref.shape[0]
        qpos = past_len + i * tq + lax.broadcasted_iota(jnp.int32, (tq, n), 0)
        kpos = past_len + lax.broadcasted_iota(jnp.int32, (tq, n), 1)
        update(nk_ref[...], nkr_ref[...], (kpos >> CHUNK_SHIFT) <= (qpos >> CHUNK_SHIFT))
        o_lat = (acc_ref[...] / l_ref[...]).astype(BF16).reshape(N_HEADS, tq, KV_RANK)
        for h in range(N_HEADS):
            o_h = jnp.dot(o_lat[h], wuv_ref[h], preferred_element_type=F32)
            o_ref[:, h * V_DIM:(h + 1) * V_DIM] = o_h.astype(o_ref.dtype)


def _attn(q_abs, q_rope, past_ckv, past_krope, l, new_ckv, new_krope, w_uv):
    _, b, t, _ = q_abs.shape
    past_len = past_ckv.shape[2]
    tq = min(t, 128)
    tk = next((c for c in (2048, 1024, 512) if past_len and past_len % c == 0), 512)
    n_past = -(-past_len // tk)
    pad = ((0, 0), (0, 0), (0, max(n_past, 1) * tk - past_len), (0, 0))
    past_ckv, past_krope = jnp.pad(past_ckv, pad), jnp.pad(past_krope, pad)
    past_idx = lambda bi, i, kk: (l, bi, jnp.minimum(kk, max(n_past - 1, 0)), 0)
    new_idx = lambda bi, i, kk: (bi, 0, 0)
    return pl.pallas_call(
        functools.partial(_attn_kernel, tq=tq, tk=tk, n_past=n_past, past_len=past_len),
        grid=(b, t // tq, n_past + 1),
        in_specs=[pl.BlockSpec((N_HEADS, None, tq, KV_RANK), lambda bi, i, kk: (0, bi, i, 0)),
                  pl.BlockSpec((N_HEADS, None, tq, ROPE_DIM), lambda bi, i, kk: (0, bi, i, 0)),
                  pl.BlockSpec((None, None, tk, KV_RANK), past_idx),
                  pl.BlockSpec((None, None, tk, ROPE_DIM), past_idx),
                  pl.BlockSpec((None, t, KV_RANK), new_idx),
                  pl.BlockSpec((None, t, ROPE_DIM), new_idx),
                  pl.BlockSpec((None, N_HEADS, KV_RANK, V_DIM), lambda bi, i, kk: (l, 0, 0, 0))],
        out_specs=pl.BlockSpec((None, tq, N_HEADS * V_DIM), lambda bi, i, kk: (bi, i, 0)),
        out_shape=jax.ShapeDtypeStruct((b, t, N_HEADS * V_DIM), BF16),
        scratch_shapes=[pltpu.VMEM((N_HEADS * tq, 1), F32), pltpu.VMEM((N_HEADS * tq, 1), F32),
                        pltpu.VMEM((N_HEADS * tq, KV_RANK), F32)],
        compiler_params=_params("parallel", "parallel", "arbitrary"),
        name="attn",
    )(q_abs, q_rope, past_ckv, past_krope, new_ckv, new_krope, w_uv)


def _norm_residual(y, x_ref, gate_ref, gn_ref, o_ref):
    nb, tt, d = x_ref.shape
    y = y.reshape(nb, tt, d)
    yn = y * lax.rsqrt(jnp.mean(y * y, axis=-1, keepdims=True) + EPS) * gn_ref[...]
    x_new = x_ref[...] + gate_ref[...] * yn
    o_ref[...] = x_new
    return x_new


def _mix_out_kernel(a_ref, w_ref, x_ref, gate_ref, gn_ref, g2n_ref, sc_ref, sh_ref, o_ref, h_ref,
                    acc0_ref, acc1_ref, *, nt):
    j = pl.program_id(1)
    nb, tt, _ = x_ref.shape
    accs = (acc0_ref, acc1_ref)

    def project(acc_ref):
        a = a_ref[...]
        acc_ref[...] = jnp.dot(a.reshape(nb * tt, a.shape[-1]), w_ref[...], preferred_element_type=F32)

    def finish(acc_ref):
        x_new = _norm_residual(acc_ref[...], x_ref, gate_ref, gn_ref, o_ref)
        _adaln(x_new, g2n_ref, sc_ref, sh_ref, h_ref)

    @pl.when(j == 0)
    def _():
        project(accs[0])

    for parity in range(2):
        @pl.when(jnp.logical_and(jnp.logical_and(j > 0, j < nt), j % 2 == parity))
        def _():
            project(accs[parity])
            finish(accs[1 - parity])

    @pl.when(j == nt)
    def _():
        finish(accs[(nt - 1) % 2])


def _mix_out(merged, w, l, x, gate, g_norm, g_next, scale_next, shift_next):
    b, t, d = x.shape
    k = w.shape[1]
    nb, tt = _row_tiles(b, t)
    nt = t // tt
    done = lambda i, j: (i, jnp.maximum(j - 1, 0), 0)
    per_batch = pl.BlockSpec((nb, 1, d), lambda i, j: (i, 0, 0))
    vec = pl.BlockSpec((1, d), lambda i, j: (0, 0))
    return pl.pallas_call(
        functools.partial(_mix_out_kernel, nt=nt),
        grid=(b // nb, nt + 1),
        in_specs=[pl.BlockSpec((nb, tt, k), lambda i, j: (i, jnp.minimum(j, nt - 1), 0)),
                  pl.BlockSpec((None, k, d), lambda i, j: (l, 0, 0), pipeline_mode=pl.Buffered(1)),
                  pl.BlockSpec((nb, tt, d), done),
                  per_batch, vec, vec, per_batch, per_batch],
        out_specs=(pl.BlockSpec((nb, tt, d), done), pl.BlockSpec((nb, tt, d), done)),
        out_shape=(jax.ShapeDtypeStruct((b, t, d), F32), jax.ShapeDtypeStruct((b, t, d), BF16)),
        scratch_shapes=[pltpu.VMEM((nb * tt, d), F32), pltpu.VMEM((nb * tt, d), F32)],
        compiler_params=_params("parallel", "arbitrary"),
        name="mm_out",
    )(merged, w, x, gate, g_norm.reshape(1, d), g_next.reshape(1, d), scale_next, shift_next)


def _adaln(x, g_ref, sc_ref, sh_ref, h_ref):
    hn = x * lax.rsqrt(jnp.mean(x * x, axis=-1, keepdims=True) + EPS) * g_ref[...]
    h_ref[...] = (hn * (1.0 + sc_ref[...]) + sh_ref[...]).astype(h_ref.dtype)


def _ffn_down_kernel(ug_ref, uv_ref, prev_ref, hist_ref, wd_ref, bd_ref, w_ref, x_ref, gate_ref, gn_ref, *rest,
                     nkb, emit_next):
    if emit_next:
        g2n_ref, sc_ref, sh_ref, o_ref, h_ref, ext_ref, act_ref, acc_ref = rest
    else:
        o_ref, ext_ref, act_ref, acc_ref = rest
    j = pl.program_id(1)
    s = pl.program_id(2)
    nb, tt, _ = x_ref.shape
    lead = FFN_HALO - (FFN_K - 1)

    def activate(slot):
        ext_ref[:, 0:FFN_HALO, :] = jnp.where(j == 0, hist_ref[...], prev_ref[...].astype(F32))
        ext_ref[:, FFN_HALO:, :] = ug_ref[...].astype(F32)
        conv = bd_ref[...]
        for k in range(FFN_K):
            conv = conv + wd_ref[k:k + 1, :] * ext_ref[:, lead + k:lead + k + tt, :]
        act = (_silu(conv) * uv_ref[...].astype(F32)).astype(BF16)
        act_ref[slot] = act.reshape(nb * tt, act.shape[-1])

    def project(slot):
        acc_ref[...] += jnp.dot(act_ref[slot], w_ref[...], preferred_element_type=F32)

    @pl.when(s == 0)
    def _():
        acc_ref[...] = jnp.zeros(acc_ref.shape, F32)
        activate(0)

    @pl.when(jnp.logical_and(s > 0, s < nkb))
    def _():
        slot = s % 2
        activate(slot)
        project(1 - slot)

    @pl.when(s == nkb)
    def _():
        project((nkb - 1) % 2)
        x_new = _norm_residual(acc_ref[...], x_ref, gate_ref, gn_ref, o_ref)
        if emit_next:
            _adaln(x_new, g2n_ref, sc_ref, sh_ref, h_ref)


def _ffn_down(up, hist, w_dw, b_dw, w, l, x, gate, g_norm, nxt=None):
    b, t, d = x.shape
    nb, tt = _row_tiles(b, t)
    tk = D_FF // 4
    nkb = D_FF // tk
    hb = tt // FFN_HALO
    full = lambda i, j, s: (i, j, 0)
    ka = lambda s: jnp.minimum(s, nkb - 1)
    kp = lambda s: jnp.maximum(s - 1, 0)
    per_batch = pl.BlockSpec((nb, 1, d), lambda i, j, s: (i, 0, 0))
    vec = pl.BlockSpec((1, d), lambda i, j, s: (0, 0))
    x_spec = pl.BlockSpec((nb, tt, d), full)
    x_shape = jax.ShapeDtypeStruct((b, t, d), F32)
    if nxt is None:
        extra_in, extra_specs, out_specs, out_shape = [], [], x_spec, x_shape
    else:
        extra_in = [nxt[0].reshape(1, d), nxt[1], nxt[2]]
        extra_specs = [vec, per_batch, per_batch]
        out_specs = (x_spec, x_spec)
        out_shape = (x_shape, jax.ShapeDtypeStruct((b, t, d), BF16))
    return pl.pallas_call(
        functools.partial(_ffn_down_kernel, nkb=nkb, emit_next=nxt is not None),
        grid=(b // nb, t // tt, nkb + 1),
        in_specs=[pl.BlockSpec((nb, tt, tk), lambda i, j, s: (i, j, ka(s))),
                  pl.BlockSpec((nb, tt, tk), lambda i, j, s: (i, j, nkb + ka(s))),
                  pl.BlockSpec((nb, FFN_HALO, tk), lambda i, j, s: (i, jnp.maximum(j * hb - 1, 0), ka(s))),
                  pl.BlockSpec((nb, FFN_HALO, tk), lambda i, j, s: (i, 0, ka(s))),
                  pl.BlockSpec((FFN_K, tk), lambda i, j, s: (0, ka(s))),
                  pl.BlockSpec((1, tk), lambda i, j, s: (0, ka(s))),
                  pl.BlockSpec((None, tk, d), lambda i, j, s: (l, kp(s), 0)),
                  x_spec, per_batch, vec] + extra_specs,
        out_specs=out_specs,
        out_shape=out_shape,
        scratch_shapes=[pltpu.VMEM((nb, tt + FFN_HALO, tk), F32), pltpu.VMEM((2, nb * tt, tk), BF16),
                        pltpu.VMEM((nb * tt, d), F32)],
        compiler_params=_params("parallel", "parallel", "arbitrary"),
        name="ffn_down",
    )(up, up, up, hist, w_dw, b_dw.reshape(1, D_FF), w, x, gate, g_norm.reshape(1, d), *extra_in)


def _rope_tables(pos0, t):
    half = ROPE_DIM // 2
    inv = ROPE_THETA ** (-jnp.arange(half, dtype=F32) / half)
    pos = (pos0 + jnp.arange(t, dtype=jnp.int32)).astype(F32)
    ang = pos[:, None] * inv[None, :]
    cos, sin = jnp.cos(ang), jnp.sin(ang)
    return (jnp.concatenate([cos, cos, cos, cos], axis=-1),
            jnp.concatenate([-sin, sin, -sin, sin], axis=-1))


def _front_pad(a, rows):
    return jnp.pad(a, ((0, 0), (rows - a.shape[1], 0), (0, 0)))


def _tail(hist, new, n, cols):
    keep = min(new.shape[1], n)
    return jnp.concatenate([hist, new[:, new.shape[1] - keep:, :cols].astype(F32)], axis=1)[:, -n:]


def _attend(g, l, q, ckv_b, krope_b, cos, sin, p):
    t = q.shape[1]
    if g["past_ckv"].shape[2] == 0 and t % 512 == 0:
        kc, vc = _kvcat(ckv_b, krope_b, p["w_uk2"], p["w_uv2"], l)
        return _mha(q, cos, sin, kc, vc)
    q_abs, q_rope = _qprep(q, p["w_ukt"], l, cos, sin)
    return _attn(q_abs, q_rope, g["past_ckv"], g["past_krope"], l, ckv_b, krope_b[:, :, :ROPE_DIM], p["w_uv"])


def _layer(groups, l, w, p, nxt):
    d = D_MODEL
    dims = [g["x"].shape[:2] for g in groups]
    flat = lambda arrs: [a.reshape(-1, a.shape[-1]) for a in arrs]
    unflat = lambda outs: [o.reshape(b, t, o.shape[-1]) for o, (b, t) in zip(outs, dims)]
    mods = [[g["mod"][:, i] for i in range(6)] for g in groups]
    tables = [_rope_tables(g["pos0"], t) for g, (_, t) in zip(groups, dims)]

    h = flat([g["h"] if g.get("h") is not None else _norm_mod(g["x"], p["g_pre_mix"], m[1], m[0])
              for g, m in zip(groups, mods)])
    tn = 512
    u_a = unflat(_mm(*h, [(p["w_a"], l, 0), (p["w_a"], l, D_CONV // tn)], D_CONV, F32,
                     tn=tn, epi="glu", name="mm_glu"))
    zmid = unflat(_mm(*h, [(p["w_mid"], l, 0)], MID_W, F32, tn=MID_TN, name="mm_mid"))
    gates2 = _mm(*h, [(p["w_g"], l, 0)], 3 * d, BF16, tn=1024, epi="sigmoid", name="mm_gates")
    gates = unflat(gates2)

    a_act = [_conv_a(u, _front_pad(g["hist_conv"], CONV_HALO), p["w_dwa"], p["b_dwa"], p["ln_a_g"], p["ln_a_b"])
             for u, g in zip(u_a, groups)]
    out_a = _mm(*flat(a_act), [(w["w_pa"], l, 0)], d, BF16, tn=tn, epi="mul", extras=[(*gates2, 0)], name="mm_pa")

    out_b = flat([_pool(z, _front_pad(g["hist_pool"], POOL_HALO), g["pos0"], p["w_pool"], l, p["pool_scale"], gt)
                  for z, g, gt in zip(zmid, groups, gates)])

    lat = [_lat(z, p["g_q_lat"], p["g_kv_lat"], cos, sin) for z, (cos, sin) in zip(zmid, tables)]
    q = unflat(_mm(*flat([o[0] for o in lat]), [(p["w_uq"], l, 0)], p["w_uq"].shape[2], F32, tn=tn, name="mm_uq"))
    o = [_attend(g, l, qg, lg[2], lg[4], cos, sin, p) for g, qg, lg, (cos, sin) in zip(groups, q, lat, tables)]
    merged = unflat(_mm(*flat(o), [(w["w_oc"], l, 0)], d, BF16, tn=tn, epi="mul_add2",
                        extras=[(*gates2, 2 * d // tn), (*out_a, 0), (*out_b, 0)], name="mm_oc"))

    mixed = [_mix_out(mg, p["w_out"], l, g["x"], m[2], p["g_post_mix"], p["g_pre_ffn"], m[4], m[3])
             for mg, g, m in zip(merged, groups, mods)]
    x = [xh[0] for xh in mixed]

    h2 = flat([xh[1] for xh in mixed])
    up = unflat(_mm(*h2, [(w["w_up"], l, 0)], 2 * D_FF, BF16, tn=1024, name="mm_up"))
    nxts = [None] * len(groups) if nxt is None else [(nxt[0], mn[:, 1], mn[:, 0]) for mn in nxt[1]]
    outs = [_ffn_down(u, _front_pad(g["hist_ffn"], FFN_HALO), p["w_dwf"], p["b_dwf"], p["w_down"], l, xg, m[5],
                      p["g_post_ffn"], nx) for u, g, xg, m, nx in zip(up, groups, x, mods, nxts)]
    x = [o if nxt is None else o[0] for o in outs]
    h_next = [None if nxt is None else o[1] for o in outs]

    states = [(lg[1], lg[3],
               _tail(g["hist_conv"], u, CONV_K - 1, D_CONV),
               _tail(g["hist_pool"], z, POOL_MAX - 1, D_POOL),
               _tail(g["hist_ffn"], uu, FFN_K - 1, D_FF))
              for lg, g, u, z, uu in zip(lat, groups, u_a, zmid, up)]
    return x, h_next, states


_SMALL = ("g_pre_mix", "g_post_mix", "w_dwa", "b_dwa", "ln_a_g", "ln_a_b", "pool_scale", "g_q_lat",
          "g_kv_lat", "g_pre_ffn", "g_post_ffn", "w_dwf", "b_dwf")


def _prep_weights(w):
    nl = w["w_in"].shape[0]
    w_in = w["w_in"]
    w_uq = w["w_uq"].reshape(nl, Q_RANK, N_HEADS, NOPE_DIM + ROPE_DIM)
    w_uq = jnp.concatenate([w_uq[..., :NOPE_DIM].reshape(nl, Q_RANK, -1),
                            w_uq[..., NOPE_DIM:].reshape(nl, Q_RANK, -1)], axis=2)
    mid_pad = ((0, 0), (0, 0), (0, MID_W - (OFF_G - OFF_B)))
    return dict(
        w_a=w_in[:, :, :OFF_B].astype(BF16),
        w_mid=jnp.pad(w_in[:, :, OFF_B:OFF_G], mid_pad).astype(BF16),
        w_g=w_in[:, :, OFF_G:].astype(BF16),
        w_pool=w["w_pool"].astype(BF16),
        w_uq=w_uq.astype(BF16),
        w_ukt=jnp.transpose(w["w_uk"], (0, 2, 3, 1)).astype(BF16),
        w_uv=jnp.transpose(w["w_uv"], (0, 2, 1, 3)).astype(BF16),
        w_uk2=w["w_uk"].reshape(nl, KV_RANK, N_HEADS * NOPE_DIM).astype(BF16),
        w_uv2=w["w_uv"].reshape(nl, KV_RANK, N_HEADS * V_DIM).astype(BF16),
        w_out=w["w_out"].astype(BF16), w_down=w["w_down"].astype(BF16),
    )


def kernel(x_prompt, x_sample, cache_ckv, cache_krope, state_conv, state_pool, state_ffn, c_prompt, c_sample, w_mod, b_mod, g_pre_mix, g_post_mix, w_in, w_dwa, b_dwa, ln_a_g, ln_a_b, w_pa, w_pool, pool_scale, g_q_lat, g_kv_lat, w_uq, w_uk, w_uv, w_oc, w_out, g_pre_ffn, g_post_ffn, w_up, w_dwf, b_dwf, w_down):
    weights = dict(g_pre_mix=g_pre_mix, g_post_mix=g_post_mix, w_in=w_in, w_dwa=w_dwa, b_dwa=b_dwa,
                   ln_a_g=ln_a_g, ln_a_b=ln_a_b, w_pa=w_pa, w_pool=w_pool, pool_scale=pool_scale,
                   g_q_lat=g_q_lat, g_kv_lat=g_kv_lat, w_uq=w_uq, w_uk=w_uk, w_uv=w_uv, w_oc=w_oc,
                   w_out=w_out, g_pre_ffn=g_pre_ffn, g_post_ffn=g_post_ffn, w_up=w_up, w_dwf=w_dwf,
                   b_dwf=b_dwf, w_down=w_down)
    depth = w_mod.shape[0]
    bp, bs = x_prompt.shape[0], x_sample.shape[0]
    past_len = cache_ckv.shape[2]
    d = x_prompt.shape[-1]

    rows = -(-(bp + bs) // SUBLANES) * SUBLANES
    c_all = jnp.pad(jnp.concatenate([c_prompt, c_sample], axis=0), ((0, rows - bp - bs), (0, 0)))
    mod_all = _mod(c_all, w_mod, b_mod)

    xp, xs = x_prompt, x_sample
    st_p = [[] for _ in range(5)]
    st_s = [[] for _ in range(5)]
    stacked = _prep_weights(weights)
    mods = [(mod_all[l, :bp].reshape(bp, 6, 1, d), mod_all[l, bp:bp + bs].reshape(bs, 6, 1, d)) for l in range(depth)]
    hp = hs = None
    for l in range(depth):
        p = dict(stacked, **{name: weights[name][l] for name in _SMALL})
        nxt = (g_pre_mix[l + 1], mods[l + 1]) if l + 1 < depth else None
        prompt = dict(x=xp, h=hp, mod=mods[l][0], pos0=0,
                      past_ckv=jnp.zeros((depth, bp, 0, KV_RANK), F32),
                      past_krope=jnp.zeros((depth, bp, 0, ROPE_DIM), F32),
                      hist_conv=jnp.zeros((bp, CONV_K - 1, D_CONV), F32),
                      hist_pool=jnp.zeros((bp, POOL_MAX - 1, D_POOL), F32),
                      hist_ffn=jnp.zeros((bp, FFN_K - 1, D_FF), F32))
        sample = dict(x=xs, h=hs, mod=mods[l][1], pos0=past_len,
                      past_ckv=cache_ckv, past_krope=cache_krope, hist_conv=state_conv[l],
                      hist_pool=state_pool[l], hist_ffn=state_ffn[l])
        (xp, xs), (hp, hs), (sp, ss) = _layer((prompt, sample), l, weights, p, nxt)
        for i in range(5):
            st_p[i].append(sp[i])
            st_s[i].append(ss[i])
    return (xp, xs) + tuple(jnp.stack(s) for s in st_p) + tuple(jnp.stack(s) for s in st_s)
```

```python
import functools

import jax
import jax.numpy as jnp
from jax import lax
from jax.experimental import pallas as pl
from jax.experimental.pallas import tpu as pltpu

F32 = jnp.float32
BF16 = jnp.bfloat16

D_MODEL = 2048
CHUNK = 64
CHUNK_SHIFT = 6
assert 1 << CHUNK_SHIFT == CHUNK
D_CONV = D_MODEL // 2
CONV_K = 31
D_POOL = D_MODEL // 2
POOL_WINDOWS = (2, 4, 8, 16)
POOL_MAX = 16
N_POOL_GROUPS = 4
POOL_GROUP = D_POOL // N_POOL_GROUPS
POOL_OUT = D_MODEL // N_POOL_GROUPS
N_HEADS = D_MODEL // 128
NOPE_DIM = 128
ROPE_DIM = 64
V_DIM = 128
Q_RANK = D_MODEL // 4
KV_RANK = D_MODEL // 4
ROPE_THETA = 10000.0
ATTN_SCALE = (NOPE_DIM + ROPE_DIM) ** -0.5
LOG2E = 1.4426950408889634
QK_DIM = 256
V_AUG = 256
D_FF = 256 * ((8 * D_MODEL // 3 + 255) // 256)
FFN_K = 3
EPS = 1e-6
NEG = -1e30
OFF_B = 2 * D_CONV
OFF_R = OFF_B + D_POOL + Q_RANK + KV_RANK
OFF_G = OFF_R + ROPE_DIM

LANES = 128
SUBLANES = 8
VMEM_LIMIT_BYTES = 56 * 1024 * 1024

MID_Q = D_POOL
MID_KV = MID_Q + Q_RANK
MID_R = MID_KV + KV_RANK
MID_W = MID_R + 2 * LANES
MID_TN = MID_W // 3
CONV_HALO = 32
POOL_HALO = 16
FFN_HALO = 8


def _params(*sem):
    return pltpu.CompilerParams(dimension_semantics=sem, vmem_limit_bytes=VMEM_LIMIT_BYTES)


def _sigmoid(x):
    return 0.5 + 0.5 * jnp.tanh(0.5 * x)


def _silu(x):
    h = 0.5 * x
    return h + h * jnp.tanh(h)


def _row_tiles(b, t):
    if t >= 512:
        return 1, 512
    return b, t


def _mod_kernel(c_ref, w_ref, b_ref, o_ref):
    c = c_ref[...]
    a = _silu(c).astype(BF16)
    o_ref[...] = jnp.dot(a, w_ref[...].astype(BF16), preferred_element_type=F32) + b_ref[...]


def _mod(c_all, w_mod, b_mod):
    nl, d, n = w_mod.shape
    bp = c_all.shape[0]
    tn = 1024
    return pl.pallas_call(
        _mod_kernel,
        grid=(nl, n // tn),
        in_specs=[pl.BlockSpec((bp, d), lambda l, j: (0, 0)),
                  pl.BlockSpec((None, d, tn), lambda l, j: (l, 0, j)),
                  pl.BlockSpec((None, 1, tn), lambda l, j: (l, 0, j))],
        out_specs=pl.BlockSpec((None, bp, tn), lambda l, j: (l, 0, j)),
        out_shape=jax.ShapeDtypeStruct((nl, bp, n), F32),
        compiler_params=_params("parallel", "parallel"),
        name="mod",
    )(c_all, w_mod, b_mod.reshape(nl, 1, n))


def _norm_mod_kernel(x_ref, g_ref, sc_ref, sh_ref, o_ref):
    x = x_ref[...]
    y = x * lax.rsqrt(jnp.mean(x * x, axis=-1, keepdims=True) + EPS) * g_ref[...]
    o_ref[...] = (y * (1.0 + sc_ref[...]) + sh_ref[...]).astype(o_ref.dtype)


def _norm_mod(x, g, scale, shift):
    b, t, d = x.shape
    nb, tt = _row_tiles(b, t)
    return pl.pallas_call(
        _norm_mod_kernel,
        grid=(b // nb, t // tt),
        in_specs=[pl.BlockSpec((nb, tt, d), lambda i, j: (i, j, 0)),
                  pl.BlockSpec((1, d), lambda i, j: (0, 0)),
                  pl.BlockSpec((nb, 1, d), lambda i, j: (i, 0, 0)),
                  pl.BlockSpec((nb, 1, d), lambda i, j: (i, 0, 0))],
        out_specs=pl.BlockSpec((nb, tt, d), lambda i, j: (i, j, 0)),
        out_shape=jax.ShapeDtypeStruct((b, t, d), BF16),
        compiler_params=_params("parallel", "parallel"),
        name="norm_mod",
    )(x, g.reshape(1, d), scale, shift)


def _mm_kernel(*refs, epi, n_w, n_e):
    ap_ref, as_ref = refs[0], refs[1]
    w_refs = refs[2:2 + n_w]
    e_refs = refs[2 + n_w:2 + n_w + 2 * n_e]
    op_ref, os_ref = refs[2 + n_w + 2 * n_e], refs[3 + n_w + 2 * n_e]
    wb_refs = refs[4 + n_w + 2 * n_e:]
    m = pl.program_id(1)

    @pl.when(m == 0)
    def _():
        for w_ref, wb_ref in zip(w_refs, wb_refs):
            wb_ref[...] = w_ref[...].astype(BF16)

    def compute(a, es):
        z = jnp.dot(a, wb_refs[0][...], preferred_element_type=F32)
        if epi == "sigmoid":
            z = _sigmoid(z)
        elif epi == "glu":
            z = z * _sigmoid(jnp.dot(a, wb_refs[1][...], preferred_element_type=F32))
        elif epi == "mul":
            z = z * es[0][...].astype(F32)
        elif epi == "mul_add2":
            z = z * es[0][...].astype(F32) + es[1][...].astype(F32) + es[2][...].astype(F32)
        return z

    @pl.when(m == 0)
    def _():
        os_ref[...] = compute(as_ref[...], e_refs[1::2]).astype(os_ref.dtype)

    @pl.when(m > 0)
    def _():
        op_ref[...] = compute(ap_ref[...], e_refs[0::2]).astype(op_ref.dtype)


def _mm(a_p, a_s, ws, n, out_dtype, *, tn, epi="none", extras=(), name="mm"):
    rp, k = a_p.shape
    rs = a_s.shape[0]
    tm = min(rp, 1024)
    npt = rp // tm

    def w_spec(arr, lead, first):
        if lead is None:
            return pl.BlockSpec((k, tn), lambda j, m: (0, first + j))
        return pl.BlockSpec((None, k, tn), lambda j, m: (lead, 0, first + j))

    def e_specs(first):
        return [pl.BlockSpec((tm, tn), lambda j, m: (jnp.maximum(m - 1, 0), first + j)),
                pl.BlockSpec((rs, tn), lambda j, m: (0, first + j))]

    p_row = lambda j, m: (jnp.maximum(m - 1, 0), 0)
    p_out = lambda j, m: (jnp.maximum(m - 1, 0), j)
    e_in = [arr for e in extras for arr in e[:2]]
    return pl.pallas_call(
        functools.partial(_mm_kernel, epi=epi, n_w=len(ws), n_e=len(extras)),
        grid=(n // tn, npt + 1),
        in_specs=[pl.BlockSpec((tm, k), p_row), pl.BlockSpec((rs, k), lambda j, m: (0, 0))]
        + [w_spec(*w) for w in ws] + [s for e in extras for s in e_specs(e[2])],
        out_specs=(pl.BlockSpec((tm, tn), p_out), pl.BlockSpec((rs, tn), lambda j, m: (0, j))),
        out_shape=(jax.ShapeDtypeStruct((rp, n), out_dtype), jax.ShapeDtypeStruct((rs, n), out_dtype)),
        scratch_shapes=[pltpu.VMEM((k, tn), BF16) for _ in ws],
        compiler_params=_params("parallel", "arbitrary"),
        name=name,
    )(a_p, a_s, *[w[0] for w in ws], *e_in)


def _conv_a_kernel(u_ref, prev_ref, hist_ref, w_ref, b_ref, g_ref, be_ref, o_ref, ext_ref, sh_ref, a_ref, *, tt):
    j = pl.program_id(1)
    ext_ref[0:CONV_HALO, :] = jnp.where(j == 0, hist_ref[...], prev_ref[...])
    ext_ref[CONV_HALO:, :] = u_ref[...]
    sh_rows = sh_ref.shape[1]
    for s in range(1, SUBLANES):
        sh_ref[s - 1] = ext_ref[s:s + sh_rows, :]
    rc = 32
    cc = 512
    lead = CONV_HALO - (CONV_K - 1)

    def body(r, carry):
        r0 = pl.multiple_of(r * rc, rc)
        for c0 in range(0, D_CONV, cc):
            acc = jnp.zeros((rc, cc), F32)
            for k in range(CONV_K):
                s = (lead + k) % SUBLANES
                row = pl.multiple_of(r0 + (lead + k - s), SUBLANES)
                if s == 0:
                    x = ext_ref[pl.ds(row, rc), c0:c0 + cc]
                else:
                    x = sh_ref[s - 1, pl.ds(row, rc), c0:c0 + cc]
                acc = acc + w_ref[k:k + 1, c0:c0 + cc] * x
            a_ref[pl.ds(r0, rc), c0:c0 + cc] = acc
        return carry

    lax.fori_loop(0, tt // rc, body, 0)
    a = a_ref[...] + b_ref[...]
    mu = jnp.mean(a, axis=-1, keepdims=True)
    ac = a - mu
    var = jnp.mean(ac * ac, axis=-1, keepdims=True)
    y = ac * lax.rsqrt(var + EPS) * g_ref[...] + be_ref[...]
    o_ref[...] = _silu(y).astype(o_ref.dtype)


def _conv_a(u, hist, w_dw, b_dw, ln_g, ln_b):
    b, t, c = u.shape
    tt = min(t, 256)
    hb = tt // CONV_HALO
    vec = lambda i, j: (0, 0)
    return pl.pallas_call(
        functools.partial(_conv_a_kernel, tt=tt),
        grid=(b, t // tt),
        in_specs=[pl.BlockSpec((None, tt, c), lambda i, j: (i, j, 0)),
                  pl.BlockSpec((None, CONV_HALO, c), lambda i, j: (i, jnp.maximum(j * hb - 1, 0), 0)),
                  pl.BlockSpec((None, CONV_HALO, c), lambda i, j: (i, 0, 0)),
                  pl.BlockSpec((CONV_K, c), vec),
                  pl.BlockSpec((1, c), vec), pl.BlockSpec((1, c), vec), pl.BlockSpec((1, c), vec)],
        out_specs=pl.BlockSpec((None, tt, c), lambda i, j: (i, j, 0)),
        out_shape=jax.ShapeDtypeStruct((b, t, c), BF16),
        scratch_shapes=[pltpu.VMEM((tt + CONV_HALO, c), F32),
                        pltpu.VMEM((SUBLANES - 1, tt + CONV_HALO - SUBLANES, c), F32),
                        pltpu.VMEM((tt, c), F32)],
        compiler_params=_params("parallel", "parallel"),
        name="conv_a",
    )(u, u, hist, w_dw, b_dw.reshape(1, c), ln_g.reshape(1, c), ln_b.reshape(1, c))


def _pool_kernel(z_ref, prev_ref, hist_ref, w_ref, s_ref, gate_ref, o_ref, ext_ref, *, tt, pos0):
    j = pl.program_id(1)
    ext_ref[0:POOL_HALO, :] = jnp.where(j == 0, hist_ref[...], prev_ref[...])
    ext_ref[POOL_HALO:, :] = z_ref[...]
    pos = pos0 + j * tt + lax.broadcasted_iota(jnp.int32, (tt, 1), 0)
    for g, w in enumerate(POOL_WINDOWS):
        c0 = g * POOL_GROUP
        cur = ext_ref[:, c0:c0 + POOL_GROUP]
        sh = 1
        while sh < w:
            cur = cur + pltpu.roll(cur, sh, 0)
            sh *= 2
        win = cur[POOL_HALO:, :]
        cnt = jnp.minimum(w, pos + 1).astype(F32)
        m = (win / cnt - ext_ref[POOL_HALO:, c0:c0 + POOL_GROUP]).astype(BF16)
        cols = slice(g * POOL_OUT, (g + 1) * POOL_OUT)
        out = jnp.dot(m, w_ref[g], preferred_element_type=F32) * s_ref[:, cols]
        o_ref[:, cols] = (out * gate_ref[:, cols].astype(F32)).astype(o_ref.dtype)


def _pool(zmid, hist, pos0, w_pool, l, pool_scale, gates):
    b, t, _ = zmid.shape
    c = D_POOL
    tt = min(t, 512)
    hb = tt // POOL_HALO
    return pl.pallas_call(
        functools.partial(_pool_kernel, tt=tt, pos0=pos0),
        grid=(b, t // tt),
        in_specs=[pl.BlockSpec((None, tt, c), lambda i, j: (i, j, 0)),
                  pl.BlockSpec((None, POOL_HALO, c), lambda i, j: (i, jnp.maximum(j * hb - 1, 0), 0)),
                  pl.BlockSpec((None, POOL_HALO, c), lambda i, j: (i, 0, 0)),
                  pl.BlockSpec((None, N_POOL_GROUPS, POOL_GROUP, POOL_OUT), lambda i, j: (l, 0, 0, 0)),
                  pl.BlockSpec((1, D_MODEL), lambda i, j: (0, 0)),
                  pl.BlockSpec((None, tt, D_MODEL), lambda i, j: (i, j, 1))],
        out_specs=pl.BlockSpec((None, tt, D_MODEL), lambda i, j: (i, j, 0)),
        out_shape=jax.ShapeDtypeStruct((b, t, D_MODEL), BF16),
        scratch_shapes=[pltpu.VMEM((tt + POOL_HALO, c), F32)],
        compiler_params=_params("parallel", "parallel"),
        name="pool",
    )(zmid, zmid, hist, w_pool, pool_scale.reshape(1, D_MODEL), gates)


def _rot_half(x):
    lane = lax.broadcasted_iota(jnp.int32, x.shape, 1)
    first = (lane % ROPE_DIM) < (ROPE_DIM // 2)
    return jnp.where(first, pltpu.roll(x, LANES - ROPE_DIM // 2, 1), pltpu.roll(x, ROPE_DIM // 2, 1))


def _lat_kernel(z_ref, gq_ref, gkv_ref, cos_ref, sin_ref, ql_ref, ckv_ref, ckvb_ref, kr_ref, krb_ref):
    nb, tt, _ = z_ref.shape
    zq = z_ref[:, :, MID_Q:MID_KV]
    ql = zq * lax.rsqrt(jnp.mean(zq * zq, axis=-1, keepdims=True) + EPS) * gq_ref[...]
    ql_ref[...] = ql.astype(ql_ref.dtype)
    zkv = z_ref[:, :, MID_KV:MID_R]
    ckv = zkv * lax.rsqrt(jnp.mean(zkv * zkv, axis=-1, keepdims=True) + EPS) * gkv_ref[...]
    ckv_ref[...] = ckv
    ckvb_ref[...] = ckv.astype(BF16)
    zr = z_ref[:, :, MID_R:MID_R + LANES]
    rot = _rot_half(zr.reshape(nb * tt, LANES)).reshape(nb, tt, LANES)
    kr = zr * cos_ref[...] + rot * sin_ref[...]
    kr_ref[...] = kr[:, :, :ROPE_DIM]
    krb_ref[...] = kr.astype(BF16)


def _lat(zmid, g_q, g_kv, cos, sin):
    b, t, _ = zmid.shape
    nb, tt = _row_tiles(b, t)
    row = lambda i, j: (i, j, 0)
    vec = lambda i, j: (0, 0)
    shapes = (jax.ShapeDtypeStruct((b, t, Q_RANK), BF16),
              jax.ShapeDtypeStruct((b, t, KV_RANK), F32),
              jax.ShapeDtypeStruct((b, t, KV_RANK), BF16),
              jax.ShapeDtypeStruct((b, t, ROPE_DIM), F32),
              jax.ShapeDtypeStruct((b, t, LANES), BF16))
    return pl.pallas_call(
        _lat_kernel,
        grid=(b // nb, t // tt),
        in_specs=[pl.BlockSpec((nb, tt, MID_W), row),
                  pl.BlockSpec((1, Q_RANK), vec), pl.BlockSpec((1, KV_RANK), vec),
                  pl.BlockSpec((tt, LANES), lambda i, j: (j, 0)),
                  pl.BlockSpec((tt, LANES), lambda i, j: (j, 0))],
        out_specs=(pl.BlockSpec((nb, tt, Q_RANK), row), pl.BlockSpec((nb, tt, KV_RANK), row),
                   pl.BlockSpec((nb, tt, KV_RANK), row), pl.BlockSpec((nb, tt, ROPE_DIM), row),
                   pl.BlockSpec((nb, tt, LANES), row)),
        out_shape=shapes,
        compiler_params=_params("parallel", "parallel"),
        name="lat",
    )(zmid, g_q.reshape(1, Q_RANK), g_kv.reshape(1, KV_RANK), cos, sin)


def _qprep_kernel(q_ref, wuk_ref, cos_ref, sin_ref, qa_ref, qr_ref):
    nb, tt, _ = q_ref.shape
    rows = nb * tt
    nope = N_HEADS * NOPE_DIM
    for h in range(N_HEADS):
        qn = q_ref[:, :, h * NOPE_DIM:(h + 1) * NOPE_DIM].reshape(rows, NOPE_DIM).astype(BF16)
        qa = jnp.dot(qn, wuk_ref[h], preferred_element_type=F32) * ATTN_SCALE
        qa_ref[h] = qa.reshape(nb, tt, KV_RANK).astype(qa_ref.dtype)
    cos = cos_ref[...]
    sin = sin_ref[...]
    for c in range(N_HEADS * ROPE_DIM // LANES):
        x = q_ref[:, :, nope + c * LANES:nope + (c + 1) * LANES]
        rot = _rot_half(x.reshape(rows, LANES)).reshape(nb, tt, LANES)
        r = ((x * cos + rot * sin) * ATTN_SCALE).astype(qr_ref.dtype)
        qr_ref[2 * c] = r[:, :, :ROPE_DIM]
        qr_ref[2 * c + 1] = r[:, :, ROPE_DIM:]


def _qprep(q, w_ukt, l, cos, sin):
    b, t, qw = q.shape
    nb, tt = (1, 256) if t >= 256 else (b, t)
    return pl.pallas_call(
        _qprep_kernel,
        grid=(b // nb, t // tt),
        in_specs=[pl.BlockSpec((nb, tt, qw), lambda i, j: (i, j, 0)),
                  pl.BlockSpec((None, N_HEADS, NOPE_DIM, KV_RANK), lambda i, j: (l, 0, 0, 0)),
                  pl.BlockSpec((tt, LANES), lambda i, j: (j, 0)),
                  pl.BlockSpec((tt, LANES), lambda i, j: (j, 0))],
        out_specs=(pl.BlockSpec((N_HEADS, nb, tt, KV_RANK), lambda i, j: (0, i, j, 0)),
                   pl.BlockSpec((N_HEADS, nb, tt, ROPE_DIM), lambda i, j: (0, i, j, 0))),
        out_shape=(jax.ShapeDtypeStruct((N_HEADS, b, t, KV_RANK), BF16),
                   jax.ShapeDtypeStruct((N_HEADS, b, t, ROPE_DIM), BF16)),
        compiler_params=_params("parallel", "parallel"),
        name="qprep",
    )(q, w_ukt, cos, sin)


def _kvcat_kernel(ckv_ref, kr_ref, wuk_ref, wuv_ref, k_ref, v_ref):
    c = ckv_ref[...]
    kn = jnp.dot(c, wuk_ref[...], preferred_element_type=F32)
    vv = jnp.dot(c, wuv_ref[...], preferred_element_type=F32)
    kr = kr_ref[...]
    one_col = (lax.broadcasted_iota(jnp.int32, kr.shape, 1) == 0).astype(v_ref.dtype)
    for h in range(N_HEADS):
        k_ref[h, :, :NOPE_DIM] = kn[:, h * NOPE_DIM:(h + 1) * NOPE_DIM].astype(k_ref.dtype)
        k_ref[h, :, NOPE_DIM:] = kr
        v_ref[h, :, :V_DIM] = vv[:, h * V_DIM:(h + 1) * V_DIM].astype(v_ref.dtype)
        v_ref[h, :, V_DIM:] = one_col


def _kvcat(ckv_b, krope_b, w_uk2, w_uv2, l):
    b, t, _ = ckv_b.shape
    tt = 512
    return pl.pallas_call(
        _kvcat_kernel,
        grid=(b, t // tt),
        in_specs=[pl.BlockSpec((None, tt, KV_RANK), lambda i, j: (i, j, 0)),
                  pl.BlockSpec((None, tt, LANES), lambda i, j: (i, j, 0)),
                  pl.BlockSpec((None, KV_RANK, N_HEADS * NOPE_DIM), lambda i, j: (l, 0, 0)),
                  pl.BlockSpec((None, KV_RANK, N_HEADS * V_DIM), lambda i, j: (l, 0, 0))],
        out_specs=(pl.BlockSpec((None, N_HEADS, tt, QK_DIM), lambda i, j: (i, 0, j, 0)),
                   pl.BlockSpec((None, N_HEADS, tt, V_AUG), lambda i, j: (i, 0, j, 0))),
        out_shape=(jax.ShapeDtypeStruct((b, N_HEADS, t, QK_DIM), BF16),
                   jax.ShapeDtypeStruct((b, N_HEADS, t, V_AUG), BF16)),
        compiler_params=_params("parallel", "parallel"),
        name="kvcat",
    )(ckv_b, krope_b, w_uk2, w_uv2)


def _mha_kernel(qn_ref, qr_ref, cos_ref, sin_ref, k_ref, v_ref, o_ref, q_ref, m_ref, acc_ref, *, tb, gh):
    i = pl.program_id(2)
    dn = (((1,), (1,)), ((), ()))
    m_ref[...] = jnp.full(m_ref.shape, NEG, F32)
    acc_ref[...] = jnp.zeros(acc_ref.shape, F32)
    scale = ATTN_SCALE * LOG2E
    low = lax.broadcasted_iota(jnp.int32, (tb, LANES), 1) < ROPE_DIM
    for c in range(gh * ROPE_DIM // LANES):
        x = qr_ref[:, c * LANES:(c + 1) * LANES]
        r = (x * cos_ref[...] + _rot_half(x) * sin_ref[...]) * scale
        q_ref[2 * c, :, NOPE_DIM:] = jnp.where(low, r, 0.0).astype(q_ref.dtype)
        q_ref[2 * c + 1, :, NOPE_DIM:] = jnp.where(low, pltpu.roll(r, ROPE_DIM, 1), 0.0).astype(q_ref.dtype)
    for g in range(gh):
        q_ref[g, :, :NOPE_DIM] = (qn_ref[:, g * NOPE_DIM:(g + 1) * NOPE_DIM] * scale).astype(q_ref.dtype)

    def block(j, masked):
        k0 = pl.multiple_of(j * tb, tb)
        if masked:
            qc = lax.broadcasted_iota(jnp.int32, (tb, tb), 0) >> CHUNK_SHIFT
            kc = lax.broadcasted_iota(jnp.int32, (tb, tb), 1) >> CHUNK_SHIFT
            bias = jnp.where(kc <= qc, 0.0, NEG)
        for g in range(gh):
            s = lax.dot_general(q_ref[g], k_ref[g, pl.ds(k0, tb), :], dn, preferred_element_type=F32)
            if masked:
                s = s + bias
            m_old = m_ref[g]
            m_new = jnp.maximum(m_old, jnp.max(s, axis=-1, keepdims=True))
            alpha = jnp.exp2(m_old - m_new)
            p = jnp.exp2(s - jnp.tile(m_new, (1, tb // LANES)))
            pv = jnp.dot(p.astype(BF16), v_ref[g, pl.ds(k0, tb), :], preferred_element_type=F32)
            acc_ref[g] = jnp.tile(alpha, (1, V_AUG // LANES)) * acc_ref[g] + pv
            m_ref[g] = m_new

    def full_block(j, carry):
        block(j, False)
        return carry

    lax.fori_loop(0, i, full_block, 0)
    block(i, True)
    for g in range(gh):
        acc = acc_ref[g]
        o_ref[:, g * V_DIM:(g + 1) * V_DIM] = (acc[:, :V_DIM] / acc[:, V_DIM:V_DIM + 1]).astype(o_ref.dtype)


def _mha(q, cos, sin, kc, vc):
    b, t, _ = q.shape
    nh = kc.shape[1]
    tb = 512
    gh = 8
    rope0 = nh * NOPE_DIM // (gh * ROPE_DIM)
    resident = dict(pipeline_mode=pl.Buffered(1))
    return pl.pallas_call(
        functools.partial(_mha_kernel, tb=tb, gh=gh),
        grid=(b, nh // gh, t // tb),
        in_specs=[pl.BlockSpec((None, tb, gh * NOPE_DIM), lambda bi, hg, i: (bi, i, hg)),
                  pl.BlockSpec((None, tb, gh * ROPE_DIM), lambda bi, hg, i: (bi, i, rope0 + hg)),
                  pl.BlockSpec((tb, LANES), lambda bi, hg, i: (i, 0)),
                  pl.BlockSpec((tb, LANES), lambda bi, hg, i: (i, 0)),
                  pl.BlockSpec((None, gh, t, QK_DIM), lambda bi, hg, i: (bi, hg, 0, 0), **resident),
                  pl.BlockSpec((None, gh, t, V_AUG), lambda bi, hg, i: (bi, hg, 0, 0), **resident)],
        out_specs=pl.BlockSpec((None, tb, gh * V_DIM), lambda bi, hg, i: (bi, i, hg)),
        out_shape=jax.ShapeDtypeStruct((b, t, nh * V_DIM), BF16),
        scratch_shapes=[pltpu.VMEM((gh, tb, QK_DIM), BF16), pltpu.VMEM((gh, tb, LANES), F32),
                        pltpu.VMEM((gh, tb, V_AUG), F32)],
        compiler_params=_params("parallel", "parallel", "parallel"),
        name="mha",
    )(q, q, cos, sin, kc, vc)


def _attn_kernel(qa_ref, qr_ref, pk_ref, pkr_ref, nk_ref, nkr_ref, wuv_ref, o_ref, m_ref, l_ref, acc_ref,
                 *, tq, tk, n_past, past_len):
    i = pl.program_id(1)
    kk = pl.program_id(2)
    rows = N_HEADS * tq
    dn = (((1,), (1,)), ((), ()))

    @pl.when(kk == 0)
    def _():
        m_ref[...] = jnp.full(m_ref.shape, NEG, F32)
        l_ref[...] = jnp.zeros(l_ref.shape, F32)
        acc_ref[...] = jnp.zeros(acc_ref.shape, F32)

    def update(k, kr, ok):
        n = k.shape[0]
        s = lax.dot_general(qa_ref[...].reshape(rows, KV_RANK), k, dn, preferred_element_type=F32)
        s = s + lax.dot_general(qr_ref[...].reshape(rows, ROPE_DIM), kr, dn, preferred_element_type=F32)
        if ok is not None:
            s = jnp.where(ok[None], s.reshape(N_HEADS, tq, n), NEG).reshape(rows, n)
        m_old = m_ref[...]
        m_new = jnp.maximum(m_old, jnp.max(s, axis=-1, keepdims=True))
        alpha = jnp.exp(m_old - m_new)
        p = jnp.exp(s - m_new)
        l_ref[...] = alpha * l_ref[...] + jnp.sum(p, axis=-1, keepdims=True)
        acc_ref[...] = alpha * acc_ref[...] + jnp.dot(p.astype(BF16), k, preferred_element_type=F32)
        m_ref[...] = m_new

    @pl.when(kk < n_past)
    def _():
        ok = None
        if n_past * tk != past_len:
            ok = kk * tk + lax.broadcasted_iota(jnp.int32, (tq, tk), 1) < past_len
        update(pk_ref[...].astype(BF16), pkr_ref[...].astype(BF16), ok)

    @pl.when(kk == n_past)
    def _():
        n = nk_ref.shape[0]
        qpos = past_len + i * tq + lax.broadcasted_iota(jnp.int32, (tq, n), 0)
        kpos = past_len + lax.broadcasted_iota(jnp.int32, (tq, n), 1)
        update(nk_ref[...], nkr_ref[...], (kpos >> CHUNK_SHIFT) <= (qpos >> CHUNK_SHIFT))
        o_lat = (acc_ref[...] / l_ref[...]).astype(BF16).reshape(N_HEADS, tq, KV_RANK)
        for h in range(N_HEADS):
            o_h = jnp.dot(o_lat[h], wuv_ref[h], preferred_element_type=F32)
            o_ref[:, h * V_DIM:(h + 1) * V_DIM] = o_h.astype(o_ref.dtype)


def _attn(q_abs, q_rope, past_ckv, past_krope, l, new_ckv, new_krope, w_uv):
    _, b, t, _ = q_abs.shape
    past_len = past_ckv.shape[2]
    tq = min(t, 128)
    tk = next((c for c in (2048, 1024, 512) if past_len and past_len % c == 0), 512)
    n_past = -(-past_len // tk)
    pad = ((0, 0), (0, 0), (0, max(n_past, 1) * tk - past_len), (0, 0))
    past_ckv, past_krope = jnp.pad(past_ckv, pad), jnp.pad(past_krope, pad)
    past_idx = lambda bi, i, kk: (l, bi, jnp.minimum(kk, max(n_past - 1, 0)), 0)
    new_idx = lambda bi, i, kk: (bi, 0, 0)
    return pl.pallas_call(
        functools.partial(_attn_kernel, tq=tq, tk=tk, n_past=n_past, past_len=past_len),
        grid=(b, t // tq, n_past + 1),
        in_specs=[pl.BlockSpec((N_HEADS, None, tq, KV_RANK), lambda bi, i, kk: (0, bi, i, 0)),
                  pl.BlockSpec((N_HEADS, None, tq, ROPE_DIM), lambda bi, i, kk: (0, bi, i, 0)),
                  pl.BlockSpec((None, None, tk, KV_RANK), past_idx),
                  pl.BlockSpec((None, None, tk, ROPE_DIM), past_idx),
                  pl.BlockSpec((None, t, KV_RANK), new_idx),
                  pl.BlockSpec((None, t, ROPE_DIM), new_idx),
                  pl.BlockSpec((None, N_HEADS, KV_RANK, V_DIM), lambda bi, i, kk: (l, 0, 0, 0))],
        out_specs=pl.BlockSpec((None, tq, N_HEADS * V_DIM), lambda bi, i, kk: (bi, i, 0)),
        out_shape=jax.ShapeDtypeStruct((b, t, N_HEADS * V_DIM), BF16),
        scratch_shapes=[pltpu.VMEM((N_HEADS * tq, 1), F32), pltpu.VMEM((N_HEADS * tq, 1), F32),
                        pltpu.VMEM((N_HEADS * tq, KV_RANK), F32)],
        compiler_params=_params("parallel", "parallel", "arbitrary"),
        name="attn",
    )(q_abs, q_rope, past_ckv, past_krope, new_ckv, new_krope, w_uv)


def _norm_residual(y, x_ref, gate_ref, gn_ref, o_ref):
    nb, tt, d = x_ref.shape
    y = y.reshape(nb, tt, d)
    yn = y * lax.rsqrt(jnp.mean(y * y, axis=-1, keepdims=True) + EPS) * gn_ref[...]
    x_new = x_ref[...] + gate_ref[...] * yn
    o_ref[...] = x_new
    return x_new


def _mix_out_kernel(a_ref, w_ref, x_ref, gate_ref, gn_ref, g2n_ref, sc_ref, sh_ref, o_ref, h_ref,
                    acc0_ref, acc1_ref, *, nt):
    j = pl.program_id(1)
    nb, tt, _ = x_ref.shape
    accs = (acc0_ref, acc1_ref)

    def project(acc_ref):
        a = a_ref[...]
        acc_ref[...] = jnp.dot(a.reshape(nb * tt, a.shape[-1]), w_ref[...], preferred_element_type=F32)

    def finish(acc_ref):
        x_new = _norm_residual(acc_ref[...], x_ref, gate_ref, gn_ref, o_ref)
        _adaln(x_new, g2n_ref, sc_ref, sh_ref, h_ref)

    @pl.when(j == 0)
    def _():
        project(accs[0])

    for parity in range(2):
        @pl.when(jnp.logical_and(jnp.logical_and(j > 0, j < nt), j % 2 == parity))
        def _():
            project(accs[parity])
            finish(accs[1 - parity])

    @pl.when(j == nt)
    def _():
        finish(accs[(nt - 1) % 2])


def _mix_out(merged, w, l, x, gate, g_norm, g_next, scale_next, shift_next):
    b, t, d = x.shape
    k = w.shape[1]
    nb, tt = _row_tiles(b, t)
    nt = t // tt
    done = lambda i, j: (i, jnp.maximum(j - 1, 0), 0)
    per_batch = pl.BlockSpec((nb, 1, d), lambda i, j: (i, 0, 0))
    vec = pl.BlockSpec((1, d), lambda i, j: (0, 0))
    return pl.pallas_call(
        functools.partial(_mix_out_kernel, nt=nt),
        grid=(b // nb, nt + 1),
        in_specs=[pl.BlockSpec((nb, tt, k), lambda i, j: (i, jnp.minimum(j, nt - 1), 0)),
                  pl.BlockSpec((None, k, d), lambda i, j: (l, 0, 0), pipeline_mode=pl.Buffered(1)),
                  pl.BlockSpec((nb, tt, d), done),
                  per_batch, vec, vec, per_batch, per_batch],
        out_specs=(pl.BlockSpec((nb, tt, d), done), pl.BlockSpec((nb, tt, d), done)),
        out_shape=(jax.ShapeDtypeStruct((b, t, d), F32), jax.ShapeDtypeStruct((b, t, d), BF16)),
        scratch_shapes=[pltpu.VMEM((nb * tt, d), F32), pltpu.VMEM((nb * tt, d), F32)],
        compiler_params=_params("parallel", "arbitrary"),
        name="mm_out",
    )(merged, w, x, gate, g_norm.reshape(1, d), g_next.reshape(1, d), scale_next, shift_next)


def _adaln(x, g_ref, sc_ref, sh_ref, h_ref):
    hn = x * lax.rsqrt(jnp.mean(x * x, axis=-1, keepdims=True) + EPS) * g_ref[...]
    h_ref[...] = (hn * (1.0 + sc_ref[...]) + sh_ref[...]).astype(h_ref.dtype)


def _ffn_down_kernel(ug_ref, uv_ref, prev_ref, hist_ref, wd_ref, bd_ref, w_ref, x_ref, gate_ref, gn_ref, *rest,
                     nkb, emit_next):
    if emit_next:
        g2n_ref, sc_ref, sh_ref, o_ref, h_ref, ext_ref, act_ref, acc_ref = rest
    else:
        o_ref, ext_ref, act_ref, acc_ref = rest
    j = pl.program_id(1)
    s = pl.program_id(2)
    nb, tt, _ = x_ref.shape
    lead = FFN_HALO - (FFN_K - 1)

    def activate(slot):
        ext_ref[:, 0:FFN_HALO, :] = jnp.where(j == 0, hist_ref[...], prev_ref[...].astype(F32))
        ext_ref[:, FFN_HALO:, :] = ug_ref[...].astype(F32)
        conv = bd_ref[...]
        for k in range(FFN_K):
            conv = conv + wd_ref[k:k + 1, :] * ext_ref[:, lead + k:lead + k + tt, :]
        act = (_silu(conv) * uv_ref[...].astype(F32)).astype(BF16)
        act_ref[slot] = act.reshape(nb * tt, act.shape[-1])

    def project(slot):
        acc_ref[...] += jnp.dot(act_ref[slot], w_ref[...], preferred_element_type=F32)

    @pl.when(s == 0)
    def _():
        acc_ref[...] = jnp.zeros(acc_ref.shape, F32)
        activate(0)

    @pl.when(jnp.logical_and(s > 0, s < nkb))
    def _():
        slot = s % 2
        activate(slot)
        project(1 - slot)

    @pl.when(s == nkb)
    def _():
        project((nkb - 1) % 2)
        x_new = _norm_residual(acc_ref[...], x_ref, gate_ref, gn_ref, o_ref)
        if emit_next:
            _adaln(x_new, g2n_ref, sc_ref, sh_ref, h_ref)


def _ffn_down(up, hist, w_dw, b_dw, w, l, x, gate, g_norm, nxt=None):
    b, t, d = x.shape
    nb, tt = _row_tiles(b, t)
    tk = D_FF // 4
    nkb = D_FF // tk
    hb = tt // FFN_HALO
    full = lambda i, j, s: (i, j, 0)
    ka = lambda s: jnp.minimum(s, nkb - 1)
    kp = lambda s: jnp.maximum(s - 1, 0)
    per_batch = pl.BlockSpec((nb, 1, d), lambda i, j, s: (i, 0, 0))
    vec = pl.BlockSpec((1, d), lambda i, j, s: (0, 0))
    x_spec = pl.BlockSpec((nb, tt, d), full)
    x_shape = jax.ShapeDtypeStruct((b, t, d), F32)
    if nxt is None:
        extra_in, extra_specs, out_specs, out_shape = [], [], x_spec, x_shape
    else:
        extra_in = [nxt[0].reshape(1, d), nxt[1], nxt[2]]
        extra_specs = [vec, per_batch, per_batch]
        out_specs = (x_spec, x_spec)
        out_shape = (x_shape, jax.ShapeDtypeStruct((b, t, d), BF16))
    return pl.pallas_call(
        functools.partial(_ffn_down_kernel, nkb=nkb, emit_next=nxt is not None),
        grid=(b // nb, t // tt, nkb + 1),
        in_specs=[pl.BlockSpec((nb, tt, tk), lambda i, j, s: (i, j, ka(s))),
                  pl.BlockSpec((nb, tt, tk), lambda i, j, s: (i, j, nkb + ka(s))),
                  pl.BlockSpec((nb, FFN_HALO, tk), lambda i, j, s: (i, jnp.maximum(j * hb - 1, 0), ka(s))),
                  pl.BlockSpec((nb, FFN_HALO, tk), lambda i, j, s: (i, 0, ka(s))),
                  pl.BlockSpec((FFN_K, tk), lambda i, j, s: (0, ka(s))),
                  pl.BlockSpec((1, tk), lambda i, j, s: (0, ka(s))),
                  pl.BlockSpec((None, tk, d), lambda i, j, s: (l, kp(s), 0)),
                  x_spec, per_batch, vec] + extra_specs,
        out_specs=out_specs,
        out_shape=out_shape,
        scratch_shapes=[pltpu.VMEM((nb, tt + FFN_HALO, tk), F32), pltpu.VMEM((2, nb * tt, tk), BF16),
                        pltpu.VMEM((nb * tt, d), F32)],
        compiler_params=_params("parallel", "parallel", "arbitrary"),
        name="ffn_down",
    )(up, up, up, hist, w_dw, b_dw.reshape(1, D_FF), w, x, gate, g_norm.reshape(1, d), *extra_in)


def _rope_tables(pos0, t):
    half = ROPE_DIM // 2
    inv = ROPE_THETA ** (-jnp.arange(half, dtype=F32) / half)
    pos = (pos0 + jnp.arange(t, dtype=jnp.int32)).astype(F32)
    ang = pos[:, None] * inv[None, :]
    cos, sin = jnp.cos(ang), jnp.sin(ang)
    return (jnp.concatenate([cos, cos, cos, cos], axis=-1),
            jnp.concatenate([-sin, sin, -sin, sin], axis=-1))


def _front_pad(a, rows):
    return jnp.pad(a, ((0, 0), (rows - a.shape[1], 0), (0, 0)))


def _tail(hist, new, n, cols):
    keep = min(new.shape[1], n)
    return jnp.concatenate([hist, new[:, new.shape[1] - keep:, :cols].astype(F32)], axis=1)[:, -n:]


def _attend(g, l, q, ckv_b, krope_b, cos, sin, p):
    t = q.shape[1]
    if g["past_ckv"].shape[2] == 0 and t % 512 == 0:
        kc, vc = _kvcat(ckv_b, krope_b, p["w_uk2"], p["w_uv2"], l)
        return _mha(q, cos, sin, kc, vc)
    q_abs, q_rope = _qprep(q, p["w_ukt"], l, cos, sin)
    return _attn(q_abs, q_rope, g["past_ckv"], g["past_krope"], l, ckv_b, krope_b[:, :, :ROPE_DIM], p["w_uv"])


def _layer(groups, l, w, p, nxt):
    d = D_MODEL
    dims = [g["x"].shape[:2] for g in groups]
    flat = lambda arrs: [a.reshape(-1, a.shape[-1]) for a in arrs]
    unflat = lambda outs: [o.reshape(b, t, o.shape[-1]) for o, (b, t) in zip(outs, dims)]
    mods = [[g["mod"][:, i] for i in range(6)] for g in groups]
    tables = [_rope_tables(g["pos0"], t) for g, (_, t) in zip(groups, dims)]

    h = flat([g["h"] if g.get("h") is not None else _norm_mod(g["x"], p["g_pre_mix"], m[1], m[0])
              for g, m in zip(groups, mods)])
    tn = 512
    u_a = unflat(_mm(*h, [(p["w_a"], l, 0), (p["w_a"], l, D_CONV // tn)], D_CONV, F32,
                     tn=tn, epi="glu", name="mm_glu"))
    zmid = unflat(_mm(*h, [(p["w_mid"], l, 0)], MID_W, F32, tn=MID_TN, name="mm_mid"))
    gates2 = _mm(*h, [(p["w_g"], l, 0)], 3 * d, BF16, tn=1024, epi="sigmoid", name="mm_gates")
    gates = unflat(gates2)

    a_act = [_conv_a(u, _front_pad(g["hist_conv"], CONV_HALO), p["w_dwa"], p["b_dwa"], p["ln_a_g"], p["ln_a_b"])
             for u, g in zip(u_a, groups)]
    out_a = _mm(*flat(a_act), [(w["w_pa"], l, 0)], d, BF16, tn=2 * tn, epi="mul", extras=[(*gates2, 0)],
                name="mm_pa")

    out_b = flat([_pool(z, _front_pad(g["hist_pool"], POOL_HALO), g["pos0"], p["w_pool"], l, p["pool_scale"], gt)
                  for z, g, gt in zip(zmid, groups, gates)])

    lat = [_lat(z, p["g_q_lat"], p["g_kv_lat"], cos, sin) for z, (cos, sin) in zip(zmid, tables)]
    q = unflat(_mm(*flat([o[0] for o in lat]), [(p["w_uq"], l, 0)], p["w_uq"].shape[2], F32, tn=3 * tn,
                   name="mm_uq"))
    o = [_attend(g, l, qg, lg[2], lg[4], cos, sin, p) for g, qg, lg, (cos, sin) in zip(groups, q, lat, tables)]
    merged = unflat(_mm(*flat(o), [(w["w_oc"], l, 0)], d, BF16, tn=tn, epi="mul_add2",
                        extras=[(*gates2, 2 * d // tn), (*out_a, 0), (*out_b, 0)], name="mm_oc"))

    mixed = [_mix_out(mg, p["w_out"], l, g["x"], m[2], p["g_post_mix"], p["g_pre_ffn"], m[4], m[3])
             for mg, g, m in zip(merged, groups, mods)]
    x = [xh[0] for xh in mixed]

    h2 = flat([xh[1] for xh in mixed])
    up = unflat(_mm(*h2, [(w["w_up"], l, 0)], 2 * D_FF, BF16, tn=1024, name="mm_up"))
    nxts = [None] * len(groups) if nxt is None else [(nxt[0], mn[:, 1], mn[:, 0]) for mn in nxt[1]]
    outs = [_ffn_down(u, _front_pad(g["hist_ffn"], FFN_HALO), p["w_dwf"], p["b_dwf"], p["w_down"], l, xg, m[5],
                      p["g_post_ffn"], nx) for u, g, xg, m, nx in zip(up, groups, x, mods, nxts)]
    x = [o if nxt is None else o[0] for o in outs]
    h_next = [None if nxt is None else o[1] for o in outs]

    states = [(lg[1], lg[3],
               _tail(g["hist_conv"], u, CONV_K - 1, D_CONV),
               _tail(g["hist_pool"], z, POOL_MAX - 1, D_POOL),
               _tail(g["hist_ffn"], uu, FFN_K - 1, D_FF))
              for lg, g, u, z, uu in zip(lat, groups, u_a, zmid, up)]
    return x, h_next, states


_SMALL = ("g_pre_mix", "g_post_mix", "w_dwa", "b_dwa", "ln_a_g", "ln_a_b", "pool_scale", "g_q_lat",
          "g_kv_lat", "g_pre_ffn", "g_post_ffn", "w_dwf", "b_dwf")


def _prep_weights(w):
    nl = w["w_in"].shape[0]
    w_in = w["w_in"]
    w_uq = w["w_uq"].reshape(nl, Q_RANK, N_HEADS, NOPE_DIM + ROPE_DIM)
    w_uq = jnp.concatenate([w_uq[..., :NOPE_DIM].reshape(nl, Q_RANK, -1),
                            w_uq[..., NOPE_DIM:].reshape(nl, Q_RANK, -1)], axis=2)
    mid_pad = ((0, 0), (0, 0), (0, MID_W - (OFF_G - OFF_B)))
    return dict(
        w_a=w_in[:, :, :OFF_B].astype(BF16),
        w_mid=jnp.pad(w_in[:, :, OFF_B:OFF_G], mid_pad).astype(BF16),
        w_g=w_in[:, :, OFF_G:].astype(BF16),
        w_pool=w["w_pool"].astype(BF16),
        w_uq=w_uq.astype(BF16),
        w_ukt=jnp.transpose(w["w_uk"], (0, 2, 3, 1)).astype(BF16),
        w_uv=jnp.transpose(w["w_uv"], (0, 2, 1, 3)).astype(BF16),
        w_uk2=w["w_uk"].reshape(nl, KV_RANK, N_HEADS * NOPE_DIM).astype(BF16),
        w_uv2=w["w_uv"].reshape(nl, KV_RANK, N_HEADS * V_DIM).astype(BF16),
        w_out=w["w_out"].astype(BF16), w_down=w["w_down"].astype(BF16),
    )


def kernel(x_prompt, x_sample, cache_ckv, cache_krope, state_conv, state_pool, state_ffn, c_prompt, c_sample, w_mod, b_mod, g_pre_mix, g_post_mix, w_in, w_dwa, b_dwa, ln_a_g, ln_a_b, w_pa, w_pool, pool_scale, g_q_lat, g_kv_lat, w_uq, w_uk, w_uv, w_oc, w_out, g_pre_ffn, g_post_ffn, w_up, w_dwf, b_dwf, w_down):
    weights = dict(g_pre_mix=g_pre_mix, g_post_mix=g_post_mix, w_in=w_in, w_dwa=w_dwa, b_dwa=b_dwa,
                   ln_a_g=ln_a_g, ln_a_b=ln_a_b, w_pa=w_pa, w_pool=w_pool, pool_scale=pool_scale,
                   g_q_lat=g_q_lat, g_kv_lat=g_kv_lat, w_uq=w_uq, w_uk=w_uk, w_uv=w_uv, w_oc=w_oc,
                   w_out=w_out, g_pre_ffn=g_pre_ffn, g_post_ffn=g_post_ffn, w_up=w_up, w_dwf=w_dwf,
                   b_dwf=b_dwf, w_down=w_down)
    depth = w_mod.shape[0]
    bp, bs = x_prompt.shape[0], x_sample.shape[0]
    past_len = cache_ckv.shape[2]
    d = x_prompt.shape[-1]

    rows = -(-(bp + bs) // SUBLANES) * SUBLANES
    c_all = jnp.pad(jnp.concatenate([c_prompt, c_sample], axis=0), ((0, rows - bp - bs), (0, 0)))
    mod_all = _mod(c_all, w_mod, b_mod)

    xp, xs = x_prompt, x_sample
    st_p = [[] for _ in range(5)]
    st_s = [[] for _ in range(5)]
    stacked = _prep_weights(weights)
    mods = [(mod_all[l, :bp].reshape(bp, 6, 1, d), mod_all[l, bp:bp + bs].reshape(bs, 6, 1, d)) for l in range(depth)]
    hp = hs = None
    for l in range(depth):
        p = dict(stacked, **{name: weights[name][l] for name in _SMALL})
        nxt = (g_pre_mix[l + 1], mods[l + 1]) if l + 1 < depth else None
        prompt = dict(x=xp, h=hp, mod=mods[l][0], pos0=0,
                      past_ckv=jnp.zeros((depth, bp, 0, KV_RANK), F32),
                      past_krope=jnp.zeros((depth, bp, 0, ROPE_DIM), F32),
                      hist_conv=jnp.zeros((bp, CONV_K - 1, D_CONV), F32),
                      hist_pool=jnp.zeros((bp, POOL_MAX - 1, D_POOL), F32),
                      hist_ffn=jnp.zeros((bp, FFN_K - 1, D_FF), F32))
        sample = dict(x=xs, h=hs, mod=mods[l][1], pos0=past_len,
                      past_ckv=cache_ckv, past_krope=cache_krope, hist_conv=state_conv[l],
                      hist_pool=state_pool[l], hist_ffn=state_ffn[l])
        (xp, xs), (hp, hs), (sp, ss) = _layer((prompt, sample), l, weights, p, nxt)
        for i in range(5):
            st_p[i].append(sp[i])
            st_s[i].append(ss[i])
    return (xp, xs) + tuple(jnp.stack(s) for s in st_p) + tuple(jnp.stack(s) for s in st_s)
```

```python
import functools

import jax
import jax.numpy as jnp
from jax import lax
from jax.experimental import pallas as pl
from jax.experimental.pallas import tpu as pltpu

F32 = jnp.float32
BF16 = jnp.bfloat16

D_MODEL = 2048
CHUNK = 64
CHUNK_SHIFT = 6
assert 1 << CHUNK_SHIFT == CHUNK
D_CONV = D_MODEL // 2
CONV_K = 31
D_POOL = D_MODEL // 2
POOL_WINDOWS = (2, 4, 8, 16)
POOL_MAX = 16
N_POOL_GROUPS = 4
POOL_GROUP = D_POOL // N_POOL_GROUPS
POOL_OUT = D_MODEL // N_POOL_GROUPS
N_HEADS = D_MODEL // 128
NOPE_DIM = 128
ROPE_DIM = 64
V_DIM = 128
Q_RANK = D_MODEL // 4
KV_RANK = D_MODEL // 4
ROPE_THETA = 10000.0
ATTN_SCALE = (NOPE_DIM + ROPE_DIM) ** -0.5
LOG2E = 1.4426950408889634
QK_DIM = 256
V_AUG = 256
D_FF = 256 * ((8 * D_MODEL // 3 + 255) // 256)
FFN_K = 3
EPS = 1e-6
NEG = -1e30
OFF_B = 2 * D_CONV
OFF_R = OFF_B + D_POOL + Q_RANK + KV_RANK
OFF_G = OFF_R + ROPE_DIM

LANES = 128
SUBLANES = 8
VMEM_LIMIT_BYTES = 56 * 1024 * 1024

MID_Q = D_POOL
MID_KV = MID_Q + Q_RANK
MID_R = MID_KV + KV_RANK
MID_W = MID_R + 2 * LANES
MID_TN = MID_W // 3
CONV_HALO = 32
POOL_HALO = 16
FFN_HALO = 8


def _params(*sem):
    return pltpu.CompilerParams(dimension_semantics=sem, vmem_limit_bytes=VMEM_LIMIT_BYTES)


def _sigmoid(x):
    return 0.5 + 0.5 * jnp.tanh(0.5 * x)


def _silu(x):
    h = 0.5 * x
    return h + h * jnp.tanh(h)


def _row_tiles(b, t):
    if t >= 512:
        return 1, 512
    return b, t


def _mod_kernel(c_ref, w_ref, b_ref, o_ref):
    c = c_ref[...]
    a = _silu(c).astype(BF16)
    o_ref[...] = jnp.dot(a, w_ref[...].astype(BF16), preferred_element_type=F32) + b_ref[...]


def _mod(c_all, w_mod, b_mod):
    nl, d, n = w_mod.shape
    bp = c_all.shape[0]
    tn = 1024
    return pl.pallas_call(
        _mod_kernel,
        grid=(nl, n // tn),
        in_specs=[pl.BlockSpec((bp, d), lambda l, j: (0, 0)),
                  pl.BlockSpec((None, d, tn), lambda l, j: (l, 0, j)),
                  pl.BlockSpec((None, 1, tn), lambda l, j: (l, 0, j))],
        out_specs=pl.BlockSpec((None, bp, tn), lambda l, j: (l, 0, j)),
        out_shape=jax.ShapeDtypeStruct((nl, bp, n), F32),
        compiler_params=_params("parallel", "parallel"),
        name="mod",
    )(c_all, w_mod, b_mod.reshape(nl, 1, n))


def _norm_mod_kernel(x_ref, g_ref, sc_ref, sh_ref, o_ref):
    x = x_ref[...]
    y = x * lax.rsqrt(jnp.mean(x * x, axis=-1, keepdims=True) + EPS) * g_ref[...]
    o_ref[...] = (y * (1.0 + sc_ref[...]) + sh_ref[...]).astype(o_ref.dtype)


def _norm_mod(x, g, scale, shift):
    b, t, d = x.shape
    nb, tt = _row_tiles(b, t)
    return pl.pallas_call(
        _norm_mod_kernel,
        grid=(b // nb, t // tt),
        in_specs=[pl.BlockSpec((nb, tt, d), lambda i, j: (i, j, 0)),
                  pl.BlockSpec((1, d), lambda i, j: (0, 0)),
                  pl.BlockSpec((nb, 1, d), lambda i, j: (i, 0, 0)),
                  pl.BlockSpec((nb, 1, d), lambda i, j: (i, 0, 0))],
        out_specs=pl.BlockSpec((nb, tt, d), lambda i, j: (i, j, 0)),
        out_shape=jax.ShapeDtypeStruct((b, t, d), BF16),
        compiler_params=_params("parallel", "parallel"),
        name="norm_mod",
    )(x, g.reshape(1, d), scale, shift)


def _mm_kernel(*refs, epi, n_w, n_e):
    ap_ref, as_ref = refs[0], refs[1]
    w_refs = refs[2:2 + n_w]
    e_refs = refs[2 + n_w:2 + n_w + 2 * n_e]
    op_ref, os_ref = refs[2 + n_w + 2 * n_e], refs[3 + n_w + 2 * n_e]
    wb_refs = refs[4 + n_w + 2 * n_e:]
    m = pl.program_id(1)

    @pl.when(m == 0)
    def _():
        for w_ref, wb_ref in zip(w_refs, wb_refs):
            wb_ref[...] = w_ref[...].astype(BF16)

    def compute(a, es):
        z = jnp.dot(a, wb_refs[0][...], preferred_element_type=F32)
        if epi == "sigmoid":
            z = _sigmoid(z)
        elif epi == "glu":
            z = z * _sigmoid(jnp.dot(a, wb_refs[1][...], preferred_element_type=F32))
        elif epi == "mul":
            z = z * es[0][...].astype(F32)
        elif epi == "mul_add2":
            z = z * es[0][...].astype(F32) + es[1][...].astype(F32) + es[2][...].astype(F32)
        return z

    @pl.when(m == 0)
    def _():
        os_ref[...] = compute(as_ref[...], e_refs[1::2]).astype(os_ref.dtype)

    @pl.when(m > 0)
    def _():
        op_ref[...] = compute(ap_ref[...], e_refs[0::2]).astype(op_ref.dtype)


def _mm(a_p, a_s, ws, n, out_dtype, *, tn, epi="none", extras=(), name="mm"):
    rp, k = a_p.shape
    rs = a_s.shape[0]
    tm = min(rp, 1024)
    npt = rp // tm

    def w_spec(arr, lead, first):
        if lead is None:
            return pl.BlockSpec((k, tn), lambda j, m: (0, first + j))
        return pl.BlockSpec((None, k, tn), lambda j, m: (lead, 0, first + j))

    def e_specs(first):
        return [pl.BlockSpec((tm, tn), lambda j, m: (jnp.maximum(m - 1, 0), first + j)),
                pl.BlockSpec((rs, tn), lambda j, m: (0, first + j))]

    p_row = lambda j, m: (jnp.maximum(m - 1, 0), 0)
    p_out = lambda j, m: (jnp.maximum(m - 1, 0), j)
    e_in = [arr for e in extras for arr in e[:2]]
    return pl.pallas_call(
        functools.partial(_mm_kernel, epi=epi, n_w=len(ws), n_e=len(extras)),
        grid=(n // tn, npt + 1),
        in_specs=[pl.BlockSpec((tm, k), p_row), pl.BlockSpec((rs, k), lambda j, m: (0, 0))]
        + [w_spec(*w) for w in ws] + [s for e in extras for s in e_specs(e[2])],
        out_specs=(pl.BlockSpec((tm, tn), p_out), pl.BlockSpec((rs, tn), lambda j, m: (0, j))),
        out_shape=(jax.ShapeDtypeStruct((rp, n), out_dtype), jax.ShapeDtypeStruct((rs, n), out_dtype)),
        scratch_shapes=[pltpu.VMEM((k, tn), BF16) for _ in ws],
        compiler_params=_params("parallel", "arbitrary"),
        name=name,
    )(a_p, a_s, *[w[0] for w in ws], *e_in)


def _conv_a_kernel(u_ref, prev_ref, hist_ref, w_ref, b_ref, g_ref, be_ref, o_ref, ext_ref, sh_ref, a_ref, *, tt):
    j = pl.program_id(1)
    ext_ref[0:CONV_HALO, :] = jnp.where(j == 0, hist_ref[...], prev_ref[...])
    ext_ref[CONV_HALO:, :] = u_ref[...]
    sh_rows = sh_ref.shape[1]
    for s in range(1, SUBLANES):
        sh_ref[s - 1] = ext_ref[s:s + sh_rows, :]
    rc = 32
    cc = 512
    lead = CONV_HALO - (CONV_K - 1)

    def body(r, carry):
        r0 = pl.multiple_of(r * rc, rc)
        for c0 in range(0, D_CONV, cc):
            acc = jnp.zeros((rc, cc), F32)
            for k in range(CONV_K):
                s = (lead + k) % SUBLANES
                row = pl.multiple_of(r0 + (lead + k - s), SUBLANES)
                if s == 0:
                    x = ext_ref[pl.ds(row, rc), c0:c0 + cc]
                else:
                    x = sh_ref[s - 1, pl.ds(row, rc), c0:c0 + cc]
                acc = acc + w_ref[k:k + 1, c0:c0 + cc] * x
            a_ref[pl.ds(r0, rc), c0:c0 + cc] = acc
        return carry

    lax.fori_loop(0, tt // rc, body, 0)
    a = a_ref[...] + b_ref[...]
    mu = jnp.mean(a, axis=-1, keepdims=True)
    ac = a - mu
    var = jnp.mean(ac * ac, axis=-1, keepdims=True)
    y = ac * lax.rsqrt(var + EPS) * g_ref[...] + be_ref[...]
    o_ref[...] = _silu(y).astype(o_ref.dtype)


def _conv_a(u, hist, w_dw, b_dw, ln_g, ln_b):
    b, t, c = u.shape
    tt = min(t, 256)
    hb = tt // CONV_HALO
    vec = lambda i, j: (0, 0)
    return pl.pallas_call(
        functools.partial(_conv_a_kernel, tt=tt),
        grid=(b, t // tt),
        in_specs=[pl.BlockSpec((None, tt, c), lambda i, j: (i, j, 0)),
                  pl.BlockSpec((None, CONV_HALO, c), lambda i, j: (i, jnp.maximum(j * hb - 1, 0), 0)),
                  pl.BlockSpec((None, CONV_HALO, c), lambda i, j: (i, 0, 0)),
                  pl.BlockSpec((CONV_K, c), vec),
                  pl.BlockSpec((1, c), vec), pl.BlockSpec((1, c), vec), pl.BlockSpec((1, c), vec)],
        out_specs=pl.BlockSpec((None, tt, c), lambda i, j: (i, j, 0)),
        out_shape=jax.ShapeDtypeStruct((b, t, c), BF16),
        scratch_shapes=[pltpu.VMEM((tt + CONV_HALO, c), F32),
                        pltpu.VMEM((SUBLANES - 1, tt + CONV_HALO - SUBLANES, c), F32),
                        pltpu.VMEM((tt, c), F32)],
        compiler_params=_params("parallel", "parallel"),
        name="conv_a",
    )(u, u, hist, w_dw, b_dw.reshape(1, c), ln_g.reshape(1, c), ln_b.reshape(1, c))


def _pool_kernel(z_ref, prev_ref, hist_ref, w_ref, s_ref, gate_ref, o_ref, ext_ref, *, tt, pos0):
    j = pl.program_id(1)
    ext_ref[0:POOL_HALO, :] = jnp.where(j == 0, hist_ref[...], prev_ref[...])
    ext_ref[POOL_HALO:, :] = z_ref[...]
    pos = pos0 + j * tt + lax.broadcasted_iota(jnp.int32, (tt, 1), 0)
    for g, w in enumerate(POOL_WINDOWS):
        c0 = g * POOL_GROUP
        cur = ext_ref[:, c0:c0 + POOL_GROUP]
        sh = 1
        while sh < w:
            cur = cur + pltpu.roll(cur, sh, 0)
            sh *= 2
        win = cur[POOL_HALO:, :]
        cnt = jnp.minimum(w, pos + 1).astype(F32)
        m = (win / cnt - ext_ref[POOL_HALO:, c0:c0 + POOL_GROUP]).astype(BF16)
        cols = slice(g * POOL_OUT, (g + 1) * POOL_OUT)
        out = jnp.dot(m, w_ref[g], preferred_element_type=F32) * s_ref[:, cols]
        o_ref[:, cols] = (out * gate_ref[:, cols].astype(F32)).astype(o_ref.dtype)


def _pool(zmid, hist, pos0, w_pool, l, pool_scale, gates):
    b, t, _ = zmid.shape
    c = D_POOL
    tt = min(t, 512)
    hb = tt // POOL_HALO
    return pl.pallas_call(
        functools.partial(_pool_kernel, tt=tt, pos0=pos0),
        grid=(b, t // tt),
        in_specs=[pl.BlockSpec((None, tt, c), lambda i, j: (i, j, 0)),
                  pl.BlockSpec((None, POOL_HALO, c), lambda i, j: (i, jnp.maximum(j * hb - 1, 0), 0)),
                  pl.BlockSpec((None, POOL_HALO, c), lambda i, j: (i, 0, 0)),
                  pl.BlockSpec((None, N_POOL_GROUPS, POOL_GROUP, POOL_OUT), lambda i, j: (l, 0, 0, 0)),
                  pl.BlockSpec((1, D_MODEL), lambda i, j: (0, 0)),
                  pl.BlockSpec((None, tt, D_MODEL), lambda i, j: (i, j, 1))],
        out_specs=pl.BlockSpec((None, tt, D_MODEL), lambda i, j: (i, j, 0)),
        out_shape=jax.ShapeDtypeStruct((b, t, D_MODEL), BF16),
        scratch_shapes=[pltpu.VMEM((tt + POOL_HALO, c), F32)],
        compiler_params=_params("parallel", "parallel"),
        name="pool",
    )(zmid, zmid, hist, w_pool, pool_scale.reshape(1, D_MODEL), gates)


def _rot_half(x):
    lane = lax.broadcasted_iota(jnp.int32, x.shape, 1)
    first = (lane % ROPE_DIM) < (ROPE_DIM // 2)
    return jnp.where(first, pltpu.roll(x, LANES - ROPE_DIM // 2, 1), pltpu.roll(x, ROPE_DIM // 2, 1))


def _lat_kernel(z_ref, gq_ref, gkv_ref, cos_ref, sin_ref, ql_ref, ckv_ref, ckvb_ref, kr_ref, krb_ref):
    nb, tt, _ = z_ref.shape
    zq = z_ref[:, :, MID_Q:MID_KV]
    ql = zq * lax.rsqrt(jnp.mean(zq * zq, axis=-1, keepdims=True) + EPS) * gq_ref[...]
    ql_ref[...] = ql.astype(ql_ref.dtype)
    zkv = z_ref[:, :, MID_KV:MID_R]
    ckv = zkv * lax.rsqrt(jnp.mean(zkv * zkv, axis=-1, keepdims=True) + EPS) * gkv_ref[...]
    ckv_ref[...] = ckv
    ckvb_ref[...] = ckv.astype(BF16)
    zr = z_ref[:, :, MID_R:MID_R + LANES]
    rot = _rot_half(zr.reshape(nb * tt, LANES)).reshape(nb, tt, LANES)
    kr = zr * cos_ref[...] + rot * sin_ref[...]
    kr_ref[...] = kr[:, :, :ROPE_DIM]
    krb_ref[...] = kr.astype(BF16)


def _lat(zmid, g_q, g_kv, cos, sin):
    b, t, _ = zmid.shape
    nb, tt = _row_tiles(b, t)
    row = lambda i, j: (i, j, 0)
    vec = lambda i, j: (0, 0)
    shapes = (jax.ShapeDtypeStruct((b, t, Q_RANK), BF16),
              jax.ShapeDtypeStruct((b, t, KV_RANK), F32),
              jax.ShapeDtypeStruct((b, t, KV_RANK), BF16),
              jax.ShapeDtypeStruct((b, t, ROPE_DIM), F32),
              jax.ShapeDtypeStruct((b, t, LANES), BF16))
    return pl.pallas_call(
        _lat_kernel,
        grid=(b // nb, t // tt),
        in_specs=[pl.BlockSpec((nb, tt, MID_W), row),
                  pl.BlockSpec((1, Q_RANK), vec), pl.BlockSpec((1, KV_RANK), vec),
                  pl.BlockSpec((tt, LANES), lambda i, j: (j, 0)),
                  pl.BlockSpec((tt, LANES), lambda i, j: (j, 0))],
        out_specs=(pl.BlockSpec((nb, tt, Q_RANK), row), pl.BlockSpec((nb, tt, KV_RANK), row),
                   pl.BlockSpec((nb, tt, KV_RANK), row), pl.BlockSpec((nb, tt, ROPE_DIM), row),
                   pl.BlockSpec((nb, tt, LANES), row)),
        out_shape=shapes,
        compiler_params=_params("parallel", "parallel"),
        name="lat",
    )(zmid, g_q.reshape(1, Q_RANK), g_kv.reshape(1, KV_RANK), cos, sin)


def _qprep_kernel(q_ref, wuk_ref, cos_ref, sin_ref, qa_ref, qr_ref):
    nb, tt, _ = q_ref.shape
    rows = nb * tt
    nope = N_HEADS * NOPE_DIM
    for h in range(N_HEADS):
        qn = q_ref[:, :, h * NOPE_DIM:(h + 1) * NOPE_DIM].reshape(rows, NOPE_DIM).astype(BF16)
        qa = jnp.dot(qn, wuk_ref[h], preferred_element_type=F32) * ATTN_SCALE
        qa_ref[h] = qa.reshape(nb, tt, KV_RANK).astype(qa_ref.dtype)
    cos = cos_ref[...]
    sin = sin_ref[...]
    for c in range(N_HEADS * ROPE_DIM // LANES):
        x = q_ref[:, :, nope + c * LANES:nope + (c + 1) * LANES]
        rot = _rot_half(x.reshape(rows, LANES)).reshape(nb, tt, LANES)
        r = ((x * cos + rot * sin) * ATTN_SCALE).astype(qr_ref.dtype)
        qr_ref[2 * c] = r[:, :, :ROPE_DIM]
        qr_ref[2 * c + 1] = r[:, :, ROPE_DIM:]


def _qprep(q, w_ukt, l, cos, sin):
    b, t, qw = q.shape
    nb, tt = (1, 256) if t >= 256 else (b, t)
    return pl.pallas_call(
        _qprep_kernel,
        grid=(b // nb, t // tt),
        in_specs=[pl.BlockSpec((nb, tt, qw), lambda i, j: (i, j, 0)),
                  pl.BlockSpec((None, N_HEADS, NOPE_DIM, KV_RANK), lambda i, j: (l, 0, 0, 0)),
                  pl.BlockSpec((tt, LANES), lambda i, j: (j, 0)),
                  pl.BlockSpec((tt, LANES), lambda i, j: (j, 0))],
        out_specs=(pl.BlockSpec((N_HEADS, nb, tt, KV_RANK), lambda i, j: (0, i, j, 0)),
                   pl.BlockSpec((N_HEADS, nb, tt, ROPE_DIM), lambda i, j: (0, i, j, 0))),
        out_shape=(jax.ShapeDtypeStruct((N_HEADS, b, t, KV_RANK), BF16),
                   jax.ShapeDtypeStruct((N_HEADS, b, t, ROPE_DIM), BF16)),
        compiler_params=_params("parallel", "parallel"),
        name="qprep",
    )(q, w_ukt, cos, sin)


def _kvcat_kernel(ckv_ref, kr_ref, wuk_ref, wuv_ref, k_ref, v_ref):
    c = ckv_ref[...]
    kn = jnp.dot(c, wuk_ref[...], preferred_element_type=F32)
    vv = jnp.dot(c, wuv_ref[...], preferred_element_type=F32)
    kr = kr_ref[...]
    one_col = (lax.broadcasted_iota(jnp.int32, kr.shape, 1) == 0).astype(v_ref.dtype)
    for h in range(N_HEADS):
        k_ref[h, :, :NOPE_DIM] = kn[:, h * NOPE_DIM:(h + 1) * NOPE_DIM].astype(k_ref.dtype)
        k_ref[h, :, NOPE_DIM:] = kr
        v_ref[h, :, :V_DIM] = vv[:, h * V_DIM:(h + 1) * V_DIM].astype(v_ref.dtype)
        v_ref[h, :, V_DIM:] = one_col


def _kvcat(ckv_b, krope_b, w_uk2, w_uv2, l):
    b, t, _ = ckv_b.shape
    tt = 512
    return pl.pallas_call(
        _kvcat_kernel,
        grid=(b, t // tt),
        in_specs=[pl.BlockSpec((None, tt, KV_RANK), lambda i, j: (i, j, 0)),
                  pl.BlockSpec((None, tt, LANES), lambda i, j: (i, j, 0)),
                  pl.BlockSpec((None, KV_RANK, N_HEADS * NOPE_DIM), lambda i, j: (l, 0, 0)),
                  pl.BlockSpec((None, KV_RANK, N_HEADS * V_DIM), lambda i, j: (l, 0, 0))],
        out_specs=(pl.BlockSpec((None, N_HEADS, tt, QK_DIM), lambda i, j: (i, 0, j, 0)),
                   pl.BlockSpec((None, N_HEADS, tt, V_AUG), lambda i, j: (i, 0, j, 0))),
        out_shape=(jax.ShapeDtypeStruct((b, N_HEADS, t, QK_DIM), BF16),
                   jax.ShapeDtypeStruct((b, N_HEADS, t, V_AUG), BF16)),
        compiler_params=_params("parallel", "parallel"),
        name="kvcat",
    )(ckv_b, krope_b, w_uk2, w_uv2)


def _mha_kernel(qn_ref, qr_ref, cos_ref, sin_ref, k_ref, v_ref, o_ref, q_ref, m_ref, acc_ref, *, tb, gh):
    i = pl.program_id(2)
    dn = (((1,), (1,)), ((), ()))
    m_ref[...] = jnp.full(m_ref.shape, NEG, F32)
    acc_ref[...] = jnp.zeros(acc_ref.shape, F32)
    scale = ATTN_SCALE * LOG2E
    low = lax.broadcasted_iota(jnp.int32, (tb, LANES), 1) < ROPE_DIM
    for c in range(gh * ROPE_DIM // LANES):
        x = qr_ref[:, c * LANES:(c + 1) * LANES]
        r = (x * cos_ref[...] + _rot_half(x) * sin_ref[...]) * scale
        q_ref[2 * c, :, NOPE_DIM:] = jnp.where(low, r, 0.0).astype(q_ref.dtype)
        q_ref[2 * c + 1, :, NOPE_DIM:] = jnp.where(low, pltpu.roll(r, ROPE_DIM, 1), 0.0).astype(q_ref.dtype)
    for g in range(gh):
        q_ref[g, :, :NOPE_DIM] = (qn_ref[:, g * NOPE_DIM:(g + 1) * NOPE_DIM] * scale).astype(q_ref.dtype)

    def block(j, masked):
        k0 = pl.multiple_of(j * tb, tb)
        if masked:
            qc = lax.broadcasted_iota(jnp.int32, (tb, tb), 0) >> CHUNK_SHIFT
            kc = lax.broadcasted_iota(jnp.int32, (tb, tb), 1) >> CHUNK_SHIFT
            bias = jnp.where(kc <= qc, 0.0, NEG)
        for g in range(gh):
            s = lax.dot_general(q_ref[g], k_ref[g, pl.ds(k0, tb), :], dn, preferred_element_type=F32)
            if masked:
                s = s + bias
            m_old = m_ref[g]
            m_new = jnp.maximum(m_old, jnp.max(s, axis=-1, keepdims=True))
            alpha = jnp.exp2(m_old - m_new)
            p = jnp.exp2(s - jnp.tile(m_new, (1, tb // LANES)))
            pv = jnp.dot(p.astype(BF16), v_ref[g, pl.ds(k0, tb), :], preferred_element_type=F32)
            acc_ref[g] = jnp.tile(alpha, (1, V_AUG // LANES)) * acc_ref[g] + pv
            m_ref[g] = m_new

    def full_block(j, carry):
        block(j, False)
        return carry

    lax.fori_loop(0, i, full_block, 0)
    block(i, True)
    for g in range(gh):
        acc = acc_ref[g]
        o_ref[:, g * V_DIM:(g + 1) * V_DIM] = (acc[:, :V_DIM] / acc[:, V_DIM:V_DIM + 1]).astype(o_ref.dtype)


def _mha(q, cos, sin, kc, vc):
    b, t, _ = q.shape
    nh = kc.shape[1]
    tb = 512
    gh = 8
    rope0 = nh * NOPE_DIM // (gh * ROPE_DIM)
    resident = dict(pipeline_mode=pl.Buffered(1))
    return pl.pallas_call(
        functools.partial(_mha_kernel, tb=tb, gh=gh),
        grid=(b, nh // gh, t // tb),
        in_specs=[pl.BlockSpec((None, tb, gh * NOPE_DIM), lambda bi, hg, i: (bi, i, hg)),
                  pl.BlockSpec((None, tb, gh * ROPE_DIM), lambda bi, hg, i: (bi, i, rope0 + hg)),
                  pl.BlockSpec((tb, LANES), lambda bi, hg, i: (i, 0)),
                  pl.BlockSpec((tb, LANES), lambda bi, hg, i: (i, 0)),
                  pl.BlockSpec((None, gh, t, QK_DIM), lambda bi, hg, i: (bi, hg, 0, 0), **resident),
                  pl.BlockSpec((None, gh, t, V_AUG), lambda bi, hg, i: (bi, hg, 0, 0), **resident)],
        out_specs=pl.BlockSpec((None, tb, gh * V_DIM), lambda bi, hg, i: (bi, i, hg)),
        out_shape=jax.ShapeDtypeStruct((b, t, nh * V_DIM), BF16),
        scratch_shapes=[pltpu.VMEM((gh, tb, QK_DIM), BF16), pltpu.VMEM((gh, tb, LANES), F32),
                        pltpu.VMEM((gh, tb, V_AUG), F32)],
        compiler_params=_params("parallel", "parallel", "parallel"),
        name="mha",
    )(q, q, cos, sin, kc, vc)


def _attn_kernel(qa_ref, qr_ref, pk_ref, pkr_ref, nk_ref, nkr_ref, wuv_ref, o_ref, m_ref, l_ref, acc_ref,
                 *, tq, tk, n_past, past_len):
    i = pl.program_id(1)
    kk = pl.program_id(2)
    rows = N_HEADS * tq
    dn = (((1,), (1,)), ((), ()))

    @pl.when(kk == 0)
    def _():
        m_ref[...] = jnp.full(m_ref.shape, NEG, F32)
        l_ref[...] = jnp.zeros(l_ref.shape, F32)
        acc_ref[...] = jnp.zeros(acc_ref.shape, F32)

    def update(k, kr, ok):
        n = k.shape[0]
        s = lax.dot_general(qa_ref[...].reshape(rows, KV_RANK), k, dn, preferred_element_type=F32)
        s = s + lax.dot_general(qr_ref[...].reshape(rows, ROPE_DIM), kr, dn, preferred_element_type=F32)
        if ok is not None:
            s = jnp.where(ok[None], s.reshape(N_HEADS, tq, n), NEG).reshape(rows, n)
        m_old = m_ref[...]
        m_new = jnp.maximum(m_old, jnp.max(s, axis=-1, keepdims=True))
        alpha = jnp.exp(m_old - m_new)
        p = jnp.exp(s - m_new)
        l_ref[...] = alpha * l_ref[...] + jnp.sum(p, axis=-1, keepdims=True)
        acc_ref[...] = alpha * acc_ref[...] + jnp.dot(p.astype(BF16), k, preferred_element_type=F32)
        m_ref[...] = m_new

    @pl.when(kk < n_past)
    def _():
        ok = None
        if n_past * tk != past_len:
            ok = kk * tk + lax.broadcasted_iota(jnp.int32, (tq, tk), 1) < past_len
        update(pk_ref[...].astype(BF16), pkr_ref[...].astype(BF16), ok)

    @pl.when(kk == n_past)
    def _():
        n = nk_ref.shape[0]
        qpos = past_len + i * tq + lax.broadcasted_iota(jnp.int32, (tq, n), 0)
        kpos = past_len + lax.broadcasted_iota(jnp.int32, (tq, n), 1)
        update(nk_ref[...], nkr_ref[...], (kpos >> CHUNK_SHIFT) <= (qpos >> CHUNK_SHIFT))
        o_lat = (acc_ref[...] / l_ref[...]).astype(BF16).reshape(N_HEADS, tq, KV_RANK)
        for h in range(N_HEADS):
            o_h = jnp.dot(o_lat[h], wuv_ref[h], preferred_element_type=F32)
            o_ref[:, h * V_DIM:(h + 1) * V_DIM] = o_h.astype(o_ref.dtype)


def _attn(q_abs, q_rope, past_ckv, past_krope, l, new_ckv, new_krope, w_uv):
    _, b, t, _ = q_abs.shape
    past_len = past_ckv.shape[2]
    tq = min(t, 128)
    tk = next((c for c in (2048, 1024, 512) if past_len and past_len % c == 0), 512)
    n_past = -(-past_len // tk)
    pad = ((0, 0), (0, 0), (0, max(n_past, 1) * tk - past_len), (0, 0))
    past_ckv, past_krope = jnp.pad(past_ckv, pad), jnp.pad(past_krope, pad)
    past_idx = lambda bi, i, kk: (l, bi, jnp.minimum(kk, max(n_past - 1, 0)), 0)
    new_idx = lambda bi, i, kk: (bi, 0, 0)
    return pl.pallas_call(
        functools.partial(_attn_kernel, tq=tq, tk=tk, n_past=n_past, past_len=past_len),
        grid=(b, t // tq, n_past + 1),
        in_specs=[pl.BlockSpec((N_HEADS, None, tq, KV_RANK), lambda bi, i, kk: (0, bi, i, 0)),
                  pl.BlockSpec((N_HEADS, None, tq, ROPE_DIM), lambda bi, i, kk: (0, bi, i, 0)),
                  pl.BlockSpec((None, None, tk, KV_RANK), past_idx),
                  pl.BlockSpec((None, None, tk, ROPE_DIM), past_idx),
                  pl.BlockSpec((None, t, KV_RANK), new_idx),
                  pl.BlockSpec((None, t, ROPE_DIM), new_idx),
                  pl.BlockSpec((None, N_HEADS, KV_RANK, V_DIM), lambda bi, i, kk: (l, 0, 0, 0))],
        out_specs=pl.BlockSpec((None, tq, N_HEADS * V_DIM), lambda bi, i, kk: (bi, i, 0)),
        out_shape=jax.ShapeDtypeStruct((b, t, N_HEADS * V_DIM), BF16),
        scratch_shapes=[pltpu.VMEM((N_HEADS * tq, 1), F32), pltpu.VMEM((N_HEADS * tq, 1), F32),
                        pltpu.VMEM((N_HEADS * tq, KV_RANK), F32)],
        compiler_params=_params("parallel", "parallel", "arbitrary"),
        name="attn",
    )(q_abs, q_rope, past_ckv, past_krope, new_ckv, new_krope, w_uv)


def _norm_residual(y, x_ref, gate_ref, gn_ref, o_ref):
    nb, tt, d = x_ref.shape
    y = y.reshape(nb, tt, d)
    yn = y * lax.rsqrt(jnp.mean(y * y, axis=-1, keepdims=True) + EPS) * gn_ref[...]
    x_new = x_ref[...] + gate_ref[...] * yn
    o_ref[...] = x_new
    return x_new


def _mix_out_kernel(a_ref, w_ref, x_ref, gate_ref, gn_ref, g2n_ref, sc_ref, sh_ref, o_ref, h_ref,
                    acc0_ref, acc1_ref, *, nt):
    j = pl.program_id(1)
    nb, tt, _ = x_ref.shape
    accs = (acc0_ref, acc1_ref)

    def project(acc_ref):
        a = a_ref[...]
        acc_ref[...] = jnp.dot(a.reshape(nb * tt, a.shape[-1]), w_ref[...], preferred_element_type=F32)

    def finish(acc_ref):
        x_new = _norm_residual(acc_ref[...], x_ref, gate_ref, gn_ref, o_ref)
        _adaln(x_new, g2n_ref, sc_ref, sh_ref, h_ref)

    @pl.when(j == 0)
    def _():
        project(accs[0])

    for parity in range(2):
        @pl.when(jnp.logical_and(jnp.logical_and(j > 0, j < nt), j % 2 == parity))
        def _():
            project(accs[parity])
            finish(accs[1 - parity])

    @pl.when(j == nt)
    def _():
        finish(accs[(nt - 1) % 2])


def _mix_out(merged, w, l, x, gate, g_norm, g_next, scale_next, shift_next):
    b, t, d = x.shape
    k = w.shape[1]
    nb, tt = _row_tiles(b, t)
    nt = t // tt
    done = lambda i, j: (i, jnp.maximum(j - 1, 0), 0)
    per_batch = pl.BlockSpec((nb, 1, d), lambda i, j: (i, 0, 0))
    vec = pl.BlockSpec((1, d), lambda i, j: (0, 0))
    return pl.pallas_call(
        functools.partial(_mix_out_kernel, nt=nt),
        grid=(b // nb, nt + 1),
        in_specs=[pl.BlockSpec((nb, tt, k), lambda i, j: (i, jnp.minimum(j, nt - 1), 0)),
                  pl.BlockSpec((None, k, d), lambda i, j: (l, 0, 0), pipeline_mode=pl.Buffered(1)),
                  pl.BlockSpec((nb, tt, d), done),
                  per_batch, vec, vec, per_batch, per_batch],
        out_specs=(pl.BlockSpec((nb, tt, d), done), pl.BlockSpec((nb, tt, d), done)),
        out_shape=(jax.ShapeDtypeStruct((b, t, d), F32), jax.ShapeDtypeStruct((b, t, d), BF16)),
        scratch_shapes=[pltpu.VMEM((nb * tt, d), F32), pltpu.VMEM((nb * tt, d), F32)],
        compiler_params=_params("parallel", "arbitrary"),
        name="mm_out",
    )(merged, w, x, gate, g_norm.reshape(1, d), g_next.reshape(1, d), scale_next, shift_next)


def _adaln(x, g_ref, sc_ref, sh_ref, h_ref):
    hn = x * lax.rsqrt(jnp.mean(x * x, axis=-1, keepdims=True) + EPS) * g_ref[...]
    h_ref[...] = (hn * (1.0 + sc_ref[...]) + sh_ref[...]).astype(h_ref.dtype)


def _ffn_down_kernel(ug_ref, uv_ref, prev_ref, hist_ref, wd_ref, bd_ref, w_ref, x_ref, gate_ref, gn_ref, *rest,
                     nkb, emit_next):
    if emit_next:
        g2n_ref, sc_ref, sh_ref, o_ref, h_ref, ext_ref, act_ref, acc_ref = rest
    else:
        o_ref, ext_ref, act_ref, acc_ref = rest
    j = pl.program_id(1)
    s = pl.program_id(2)
    nb, tt, _ = x_ref.shape
    lead = FFN_HALO - (FFN_K - 1)

    def activate(slot):
        ext_ref[:, 0:FFN_HALO, :] = jnp.where(j == 0, hist_ref[...], prev_ref[...].astype(F32))
        ext_ref[:, FFN_HALO:, :] = ug_ref[...].astype(F32)
        conv = bd_ref[...]
        for k in range(FFN_K):
            conv = conv + wd_ref[k:k + 1, :] * ext_ref[:, lead + k:lead + k + tt, :]
        act = (_silu(conv) * uv_ref[...].astype(F32)).astype(BF16)
        act_ref[slot] = act.reshape(nb * tt, act.shape[-1])

    def project(slot):
        acc_ref[...] += jnp.dot(act_ref[slot], w_ref[...], preferred_element_type=F32)

    @pl.when(s == 0)
    def _():
        acc_ref[...] = jnp.zeros(acc_ref.shape, F32)
        activate(0)

    @pl.when(jnp.logical_and(s > 0, s < nkb))
    def _():
        slot = s % 2
        activate(slot)
        project(1 - slot)

    @pl.when(s == nkb)
    def _():
        project((nkb - 1) % 2)
        x_new = _norm_residual(acc_ref[...], x_ref, gate_ref, gn_ref, o_ref)
        if emit_next:
            _adaln(x_new, g2n_ref, sc_ref, sh_ref, h_ref)


def _ffn_down(up, hist, w_dw, b_dw, w, l, x, gate, g_norm, nxt=None):
    b, t, d = x.shape
    nb, tt = _row_tiles(b, t)
    tk = D_FF // 4
    nkb = D_FF // tk
    hb = tt // FFN_HALO
    full = lambda i, j, s: (i, j, 0)
    ka = lambda s: jnp.minimum(s, nkb - 1)
    kp = lambda s: jnp.maximum(s - 1, 0)
    per_batch = pl.BlockSpec((nb, 1, d), lambda i, j, s: (i, 0, 0))
    vec = pl.BlockSpec((1, d), lambda i, j, s: (0, 0))
    x_spec = pl.BlockSpec((nb, tt, d), full)
    x_shape = jax.ShapeDtypeStruct((b, t, d), F32)
    if nxt is None:
        extra_in, extra_specs, out_specs, out_shape = [], [], x_spec, x_shape
    else:
        extra_in = [nxt[0].reshape(1, d), nxt[1], nxt[2]]
        extra_specs = [vec, per_batch, per_batch]
        out_specs = (x_spec, x_spec)
        out_shape = (x_shape, jax.ShapeDtypeStruct((b, t, d), BF16))
    return pl.pallas_call(
        functools.partial(_ffn_down_kernel, nkb=nkb, emit_next=nxt is not None),
        grid=(b // nb, t // tt, nkb + 1),
        in_specs=[pl.BlockSpec((nb, tt, tk), lambda i, j, s: (i, j, ka(s))),
                  pl.BlockSpec((nb, tt, tk), lambda i, j, s: (i, j, nkb + ka(s))),
                  pl.BlockSpec((nb, FFN_HALO, tk), lambda i, j, s: (i, jnp.maximum(j * hb - 1, 0), ka(s))),
                  pl.BlockSpec((nb, FFN_HALO, tk), lambda i, j, s: (i, 0, ka(s))),
                  pl.BlockSpec((FFN_K, tk), lambda i, j, s: (0, ka(s))),
                  pl.BlockSpec((1, tk), lambda i, j, s: (0, ka(s))),
                  pl.BlockSpec((None, tk, d), lambda i, j, s: (l, kp(s), 0)),
                  x_spec, per_batch, vec] + extra_specs,
        out_specs=out_specs,
        out_shape=out_shape,
        scratch_shapes=[pltpu.VMEM((nb, tt + FFN_HALO, tk), F32), pltpu.VMEM((2, nb * tt, tk), BF16),
                        pltpu.VMEM((nb * tt, d), F32)],
        compiler_params=_params("parallel", "parallel", "arbitrary"),
        name="ffn_down",
    )(up, up, up, hist, w_dw, b_dw.reshape(1, D_FF), w, x, gate, g_norm.reshape(1, d), *extra_in)


def _rope_tables(pos0, t):
    half = ROPE_DIM // 2
    inv = ROPE_THETA ** (-jnp.arange(half, dtype=F32) / half)
    pos = (pos0 + jnp.arange(t, dtype=jnp.int32)).astype(F32)
    ang = pos[:, None] * inv[None, :]
    cos, sin = jnp.cos(ang), jnp.sin(ang)
    return (jnp.concatenate([cos, cos, cos, cos], axis=-1),
            jnp.concatenate([-sin, sin, -sin, sin], axis=-1))


def _front_pad(a, rows):
    return jnp.pad(a, ((0, 0), (rows - a.shape[1], 0), (0, 0)))


def _tail(hist, new, n, cols):
    keep = min(new.shape[1], n)
    return jnp.concatenate([hist, new[:, new.shape[1] - keep:, :cols].astype(F32)], axis=1)[:, -n:]


def _attend(g, l, q, ckv_b, krope_b, cos, sin, p):
    t = q.shape[1]
    if g["past_ckv"].shape[2] == 0 and t % 512 == 0:
        kc, vc = _kvcat(ckv_b, krope_b, p["w_uk2"], p["w_uv2"], l)
        return _mha(q, cos, sin, kc, vc)
    q_abs, q_rope = _qprep(q, p["w_ukt"], l, cos, sin)
    return _attn(q_abs, q_rope, g["past_ckv"], g["past_krope"], l, ckv_b, krope_b[:, :, :ROPE_DIM], p["w_uv"])


def _layer(groups, l, w, p, nxt):
    d = D_MODEL
    dims = [g["x"].shape[:2] for g in groups]
    flat = lambda arrs: [a.reshape(-1, a.shape[-1]) for a in arrs]
    unflat = lambda outs: [o.reshape(b, t, o.shape[-1]) for o, (b, t) in zip(outs, dims)]
    mods = [[g["mod"][:, i] for i in range(6)] for g in groups]
    tables = [_rope_tables(g["pos0"], t) for g, (_, t) in zip(groups, dims)]

    h = flat([g["h"] if g.get("h") is not None else _norm_mod(g["x"], p["g_pre_mix"], m[1], m[0])
              for g, m in zip(groups, mods)])
    tn = 512
    u_a = unflat(_mm(*h, [(p["w_a"], l, 0), (p["w_a"], l, D_CONV // tn)], D_CONV, F32,
                     tn=tn, epi="glu", name="mm_glu"))
    zmid = unflat(_mm(*h, [(p["w_mid"], l, 0)], MID_W, F32, tn=MID_TN, name="mm_mid"))
    gates2 = _mm(*h, [(p["w_g"], l, 0)], 3 * d, BF16, tn=1024, epi="sigmoid", name="mm_gates")
    gates = unflat(gates2)

    a_act = [_conv_a(u, _front_pad(g["hist_conv"], CONV_HALO), p["w_dwa"], p["b_dwa"], p["ln_a_g"], p["ln_a_b"])
             for u, g in zip(u_a, groups)]
    out_a = _mm(*flat(a_act), [(w["w_pa"], l, 0)], d, BF16, tn=2 * tn, epi="mul", extras=[(*gates2, 0)],
                name="mm_pa")

    out_b = flat([_pool(z, _front_pad(g["hist_pool"], POOL_HALO), g["pos0"], p["w_pool"], l, p["pool_scale"], gt)
                  for z, g, gt in zip(zmid, groups, gates)])

    lat = [_lat(z, p["g_q_lat"], p["g_kv_lat"], cos, sin) for z, (cos, sin) in zip(zmid, tables)]
    q = unflat(_mm(*flat([o[0] for o in lat]), [(p["w_uq"], l, 0)], p["w_uq"].shape[2], F32, tn=3 * tn,
                   name="mm_uq"))
    o = [_attend(g, l, qg, lg[2], lg[4], cos, sin, p) for g, qg, lg, (cos, sin) in zip(groups, q, lat, tables)]
    merged = unflat(_mm(*flat(o), [(w["w_oc"], l, 0)], d, BF16, tn=2 * tn, epi="mul_add2",
                        extras=[(*gates2, 2 * d // (2 * tn)), (*out_a, 0), (*out_b, 0)], name="mm_oc"))

    mixed = [_mix_out(mg, p["w_out"], l, g["x"], m[2], p["g_post_mix"], p["g_pre_ffn"], m[4], m[3])
             for mg, g, m in zip(merged, groups, mods)]
    x = [xh[0] for xh in mixed]

    h2 = flat([xh[1] for xh in mixed])
    up = unflat(_mm(*h2, [(w["w_up"], l, 0)], 2 * D_FF, BF16, tn=1024, name="mm_up"))
    nxts = [None] * len(groups) if nxt is None else [(nxt[0], mn[:, 1], mn[:, 0]) for mn in nxt[1]]
    outs = [_ffn_down(u, _front_pad(g["hist_ffn"], FFN_HALO), p["w_dwf"], p["b_dwf"], p["w_down"], l, xg, m[5],
                      p["g_post_ffn"], nx) for u, g, xg, m, nx in zip(up, groups, x, mods, nxts)]
    x = [o if nxt is None else o[0] for o in outs]
    h_next = [None if nxt is None else o[1] for o in outs]

    states = [(lg[1], lg[3],
               _tail(g["hist_conv"], u, CONV_K - 1, D_CONV),
               _tail(g["hist_pool"], z, POOL_MAX - 1, D_POOL),
               _tail(g["hist_ffn"], uu, FFN_K - 1, D_FF))
              for lg, g, u, z, uu in zip(lat, groups, u_a, zmid, up)]
    return x, h_next, states


_SMALL = ("g_pre_mix", "g_post_mix", "w_dwa", "b_dwa", "ln_a_g", "ln_a_b", "pool_scale", "g_q_lat",
          "g_kv_lat", "g_pre_ffn", "g_post_ffn", "w_dwf", "b_dwf")


def _prep_weights(w):
    nl = w["w_in"].shape[0]
    w_in = w["w_in"]
    w_uq = w["w_uq"].reshape(nl, Q_RANK, N_HEADS, NOPE_DIM + ROPE_DIM)
    w_uq = jnp.concatenate([w_uq[..., :NOPE_DIM].reshape(nl, Q_RANK, -1),
                            w_uq[..., NOPE_DIM:].reshape(nl, Q_RANK, -1)], axis=2)
    mid_pad = ((0, 0), (0, 0), (0, MID_W - (OFF_G - OFF_B)))
    return dict(
        w_a=w_in[:, :, :OFF_B].astype(BF16),
        w_mid=jnp.pad(w_in[:, :, OFF_B:OFF_G], mid_pad).astype(BF16),
        w_g=w_in[:, :, OFF_G:].astype(BF16),
        w_pool=w["w_pool"].astype(BF16),
        w_uq=w_uq.astype(BF16),
        w_ukt=jnp.transpose(w["w_uk"], (0, 2, 3, 1)).astype(BF16),
        w_uv=jnp.transpose(w["w_uv"], (0, 2, 1, 3)).astype(BF16),
        w_uk2=w["w_uk"].reshape(nl, KV_RANK, N_HEADS * NOPE_DIM).astype(BF16),
        w_uv2=w["w_uv"].reshape(nl, KV_RANK, N_HEADS * V_DIM).astype(BF16),
        w_out=w["w_out"].astype(BF16), w_down=w["w_down"].astype(BF16),
    )


def kernel(x_prompt, x_sample, cache_ckv, cache_krope, state_conv, state_pool, state_ffn, c_prompt, c_sample, w_mod, b_mod, g_pre_mix, g_post_mix, w_in, w_dwa, b_dwa, ln_a_g, ln_a_b, w_pa, w_pool, pool_scale, g_q_lat, g_kv_lat, w_uq, w_uk, w_uv, w_oc, w_out, g_pre_ffn, g_post_ffn, w_up, w_dwf, b_dwf, w_down):
    weights = dict(g_pre_mix=g_pre_mix, g_post_mix=g_post_mix, w_in=w_in, w_dwa=w_dwa, b_dwa=b_dwa,
                   ln_a_g=ln_a_g, ln_a_b=ln_a_b, w_pa=w_pa, w_pool=w_pool, pool_scale=pool_scale,
                   g_q_lat=g_q_lat, g_kv_lat=g_kv_lat, w_uq=w_uq, w_uk=w_uk, w_uv=w_uv, w_oc=w_oc,
                   w_out=w_out, g_pre_ffn=g_pre_ffn, g_post_ffn=g_post_ffn, w_up=w_up, w_dwf=w_dwf,
                   b_dwf=b_dwf, w_down=w_down)
    depth = w_mod.shape[0]
    bp, bs = x_prompt.shape[0], x_sample.shape[0]
    past_len = cache_ckv.shape[2]
    d = x_prompt.shape[-1]

    rows = -(-(bp + bs) // SUBLANES) * SUBLANES
    c_all = jnp.pad(jnp.concatenate([c_prompt, c_sample], axis=0), ((0, rows - bp - bs), (0, 0)))
    mod_all = _mod(c_all, w_mod, b_mod)

    xp, xs = x_prompt, x_sample
    st_p = [[] for _ in range(5)]
    st_s = [[] for _ in range(5)]
    stacked = _prep_weights(weights)
    mods = [(mod_all[l, :bp].reshape(bp, 6, 1, d), mod_all[l, bp:bp + bs].reshape(bs, 6, 1, d)) for l in range(depth)]
    hp = hs = None
    for l in range(depth):
        p = dict(stacked, **{name: weights[name][l] for name in _SMALL})
        nxt = (g_pre_mix[l + 1], mods[l + 1]) if l + 1 < depth else None
        prompt = dict(x=xp, h=hp, mod=mods[l][0], pos0=0,
                      past_ckv=jnp.zeros((depth, bp, 0, KV_RANK), F32),
                      past_krope=jnp.zeros((depth, bp, 0, ROPE_DIM), F32),
                      hist_conv=jnp.zeros((bp, CONV_K - 1, D_CONV), F32),
                      hist_pool=jnp.zeros((bp, POOL_MAX - 1, D_POOL), F32),
                      hist_ffn=jnp.zeros((bp, FFN_K - 1, D_FF), F32))
        sample = dict(x=xs, h=hs, mod=mods[l][1], pos0=past_len,
                      past_ckv=cache_ckv, past_krope=cache_krope, hist_conv=state_conv[l],
                      hist_pool=state_pool[l], hist_ffn=state_ffn[l])
        (xp, xs), (hp, hs), (sp, ss) = _layer((prompt, sample), l, weights, p, nxt)
        for i in range(5):
            st_p[i].append(sp[i])
            st_s[i].append(ss[i])
    return (xp, xs) + tuple(jnp.stack(s) for s in st_p) + tuple(jnp.stack(s) for s in st_s)
```
